```python
import math
import jax, jax.numpy as jnp
from jax import lax
import numpy as np

D_MODEL = 2048
BATCH = 1
SEQ = 8192
DEPTH = 2
DEC_BATCH = 128
DEC_SEQ = 4
PAST_LEN = 8192
PAGE_SIZE = 128

WINDOW = 128
HD_A = 64
N_HEADS_A = (D_MODEL // 2) // HD_A
N_KV_A = max(1, N_HEADS_A // 8)
W_A = N_HEADS_A * HD_A
Q_BLOCK = WINDOW
HD_B = 128
N_HEADS_B = (D_MODEL // 2) // HD_B
W_B = N_HEADS_B * HD_B
RET_CHUNK = 128
D_MIX = W_A + W_B
D_PROJ = 2 * W_A + 2 * N_KV_A * HD_A + 4 * W_B
ROPE_THETA = 10000.0
EPS = 1e-6

kernel_name = "hymba_swa_sink_retention_step"


def rmsnorm(x, w):
    xf = x.astype(jnp.float32)
    y = xf * lax.rsqrt(jnp.mean(xf * xf, axis=-1, keepdims=True) + EPS)
    return (y * w.astype(jnp.float32)).astype(x.dtype)


def rope(x, pos):
    d = x.shape[-1]
    half = d // 2
    inv = ROPE_THETA ** (-jnp.arange(half, dtype=jnp.float32) / half)
    ang = pos.astype(jnp.float32)[:, None] * inv[None, :]
    cos = jnp.cos(ang)[None, :, None, :]
    sin = jnp.sin(ang)[None, :, None, :]
    xf = x.astype(jnp.float32)
    x1, x2 = xf[..., :half], xf[..., half:]
    return jnp.concatenate([x1 * cos - x2 * sin, x2 * cos + x1 * sin], axis=-1).astype(x.dtype)


def project(x, pos, norm_w, w_in):
    B, T, _ = x.shape
    h = rmsnorm(x, norm_w)
    p = jnp.einsum('btd,de->bte', h, w_in)
    cuts = list(np.cumsum([W_A, N_KV_A * HD_A, N_KV_A * HD_A, W_A, W_B, W_B, W_B]))
    qa, ka, va, ga, qb, kb, vb, gb = jnp.split(p, cuts, axis=-1)
    qa = rope(qa.reshape(B, T, N_HEADS_A, HD_A), pos)
    ka = rope(ka.reshape(B, T, N_KV_A, HD_A), pos)
    va = va.reshape(B, T, N_KV_A, HD_A)
    qb = rope(qb.reshape(B, T, N_HEADS_B, HD_B), pos)
    kb = rope(kb.reshape(B, T, N_HEADS_B, HD_B), pos) * (HD_B ** -0.5)
    vb = vb.reshape(B, T, N_HEADS_B, HD_B)
    return qa, ka, va, ga, qb, kb, vb, gb


def sink_window_attention(q, k, v, q_pos, k_pos, sinks):
    B, N, Tq, H, hd = q.shape
    KV = k.shape[3]
    G = H // KV
    qg = q.reshape(B, N, Tq, KV, G, hd)
    s = jnp.einsum('bnqkgd,bnskd->bnkgqs', qg, k).astype(jnp.float32) * (hd ** -0.5)
    delta = q_pos[:, :, None] - k_pos[:, None, :]
    valid = (k_pos[:, None, :] >= 0) & (delta >= 0) & (delta < WINDOW)
    s = jnp.where(valid[None, :, None, None], s, -jnp.inf)
    sink = sinks.astype(jnp.float32).reshape(1, 1, KV, G, 1, 1)
    m = jnp.maximum(jnp.max(s, axis=-1, keepdims=True), sink)
    e = jnp.exp(s - m)
    denom = jnp.sum(e, axis=-1, keepdims=True) + jnp.exp(sink - m)
    prob = (e / denom).astype(v.dtype)
    o = jnp.einsum('bnkgqs,bnskd->bnqkgd', prob, v)
    return o.reshape(B, N, Tq, H * hd)


def prompt_window_attention(qa, ka, va, sinks):
    B, T, H, hd = qa.shape
    nb = T // Q_BLOCK
    q = qa.reshape(B, nb, Q_BLOCK, H, hd)
    pad = jnp.zeros((B, Q_BLOCK, N_KV_A, hd), ka.dtype)
    kp = jnp.concatenate([pad, ka], axis=1).reshape(B, nb + 1, Q_BLOCK, N_KV_A, hd)
    vp = jnp.concatenate([pad.astype(va.dtype), va], axis=1).reshape(B, nb + 1, Q_BLOCK, N_KV_A, hd)
    k = jnp.concatenate([kp[:, :-1], kp[:, 1:]], axis=2)
    v = jnp.concatenate([vp[:, :-1], vp[:, 1:]], axis=2)
    q_pos = jnp.arange(T, dtype=jnp.int32).reshape(nb, Q_BLOCK)
    kpp = jnp.arange(-Q_BLOCK, T, dtype=jnp.int32).reshape(nb + 1, Q_BLOCK)
    k_pos = jnp.concatenate([kpp[:-1], kpp[1:]], axis=1)
    o = sink_window_attention(q, k, v, q_pos, k_pos, sinks)
    return o.reshape(B, T, H * hd)


def sample_window_attention(qa, ka, va, k_buf, v_buf, sinks):
    B, T, H, hd = qa.shape
    L = k_buf.shape[1]
    k = jnp.concatenate([k_buf.astype(ka.dtype), ka], axis=1)
    v = jnp.concatenate([v_buf.astype(va.dtype), va], axis=1)
    q_pos = (PAST_LEN + jnp.arange(T, dtype=jnp.int32))[None]
    k_pos = jnp.concatenate([PAST_LEN - L + jnp.arange(L, dtype=jnp.int32),
                             PAST_LEN + jnp.arange(T, dtype=jnp.int32)])[None]
    o = sink_window_attention(qa[:, None], k[:, None], v[:, None], q_pos, k_pos, sinks)
    return o.reshape(B, T, H * hd), k[:, -L:], v[:, -L:]


def retention(q, k, v, r0):
    B, T, H, dk = q.shape
    dv = v.shape[-1]
    C = math.gcd(T, RET_CHUNK)
    nc = T // C
    log_g = jnp.log1p(-jnp.exp2(-5.0 - jnp.arange(H, dtype=jnp.float32)))
    qf = q.astype(jnp.float32).reshape(B, nc, C, H, dk)
    kf = k.astype(jnp.float32).reshape(B, nc, C, H, dk)
    vf = v.astype(jnp.float32).reshape(B, nc, C, H, dv)
    idx = jnp.arange(C, dtype=jnp.float32)
    diff = idx[:, None] - idx[None, :]
    dmask = jnp.where(diff >= 0, jnp.exp(log_g[:, None, None] * jnp.maximum(diff, 0.0)), 0.0)
    scores = jnp.einsum('bcnhd,bcmhd->bchnm', qf, kf) * dmask[None, None]
    intra = jnp.einsum('bchnm,bcmhe->bcnhe', scores, vf)
    k_dec = jnp.exp(log_g[None, :] * (C - 1 - idx)[:, None])
    u = jnp.einsum('bcmhd,mh,bcmhe->bchde', kf, k_dec, vf)
    g_chunk = jnp.exp(log_g * C)[None, :, None, None]

    def step(r, u_c):
        return g_chunk * r + u_c, r

    r_fin, r_prev = lax.scan(step, r0.astype(jnp.float32), jnp.moveaxis(u, 1, 0))
    r_prev = jnp.moveaxis(r_prev, 0, 1)
    q_dec = jnp.exp(log_g[None, :] * (idx + 1.0)[:, None])
    inter = jnp.einsum('bcnhd,nh,bchde->bcnhe', qf, q_dec, r_prev)
    return (intra + inter).reshape(B, T, H, dv), r_fin


def head_norm(o, w):
    B, T, H, dv = o.shape
    mu = jnp.mean(o, axis=-1, keepdims=True)
    var = jnp.mean(jnp.square(o - mu), axis=-1, keepdims=True)
    y = ((o - mu) * lax.rsqrt(var + EPS)).reshape(B, T, H * dv)
    return y * w.astype(jnp.float32)


def merge(x, oa, ga, ob, gb, gn_w, w_out):
    ya = oa * jax.nn.silu(ga)
    yb = (head_norm(ob, gn_w) * jax.nn.silu(gb.astype(jnp.float32))).astype(x.dtype)
    y = jnp.concatenate([ya, yb], axis=-1)
    return x + jnp.einsum('bte,ed->btd', y, w_out)


def setup_inputs(seed: int = 0) -> dict:
    key = jax.random.key(seed)
    ks = jax.random.split(key, 11)
    w_buf = min(WINDOW, PAST_LEN)
    f32 = jnp.float32
    return {
        "x_prompt": jax.random.normal(ks[0], (BATCH, SEQ, D_MODEL), f32),
        "x_sample": jax.random.normal(ks[1], (DEC_BATCH, DEC_SEQ, D_MODEL), f32),
        "cache_k_win": jax.random.normal(ks[2], (DEPTH, DEC_BATCH, w_buf, N_KV_A, HD_A), f32),
        "cache_v_win": jax.random.normal(ks[3], (DEPTH, DEC_BATCH, w_buf, N_KV_A, HD_A), f32),
        "state_ret": 0.1 * jax.random.normal(ks[4], (DEPTH, DEC_BATCH, N_HEADS_B, HD_B, HD_B), f32),
        "norm_w": 1.0 + 0.02 * jax.random.normal(ks[5], (DEPTH, D_MODEL), f32),
        "w_in": jax.random.normal(ks[6], (DEPTH, D_MODEL, D_PROJ), f32) * (D_MODEL ** -0.5),
        "attn_sinks": 0.5 * jax.random.normal(ks[7], (DEPTH, N_HEADS_A), f32),
        "ret_norm_w": 1.0 + 0.02 * jax.random.normal(ks[8], (DEPTH, W_B), f32),
        "w_out": jax.random.normal(ks[9], (DEPTH, D_MIX, D_MODEL), f32) * (0.5 * D_MIX ** -0.5),
        "final_norm_w": 1.0 + 0.02 * jax.random.normal(ks[10], (D_MODEL,), f32),
    }


def reference(x_prompt, x_sample, cache_k_win, cache_v_win, state_ret,
              norm_w, w_in, attn_sinks, ret_norm_w, w_out, final_norm_w):
    xp, xs = x_prompt, x_sample
    Bp, Tp, _ = xp.shape
    Ts = xs.shape[1]
    pos_p = jnp.arange(Tp, dtype=jnp.int32)
    pos_s = PAST_LEN + jnp.arange(Ts, dtype=jnp.int32)
    keep_p = min(WINDOW, Tp)
    kp_l, vp_l, rp_l, ks_l, vs_l, rs_l = [], [], [], [], [], []
    for d in range(DEPTH):
        qa, ka, va, ga, qb, kb, vb, gb = project(xp, pos_p, norm_w[d], w_in[d])
        oa = prompt_window_attention(qa, ka, va, attn_sinks[d])
        r0 = jnp.zeros((Bp, N_HEADS_B, HD_B, HD_B), jnp.float32)
        ob, r_fin = retention(qb, kb, vb, r0)
        xp = merge(xp, oa, ga, ob, gb, ret_norm_w[d], w_out[d])
        kp_l.append(ka[:, Tp - keep_p:].astype(cache_k_win.dtype))
        vp_l.append(va[:, Tp - keep_p:].astype(cache_v_win.dtype))
        rp_l.append(r_fin.astype(state_ret.dtype))
        qa, ka, va, ga, qb, kb, vb, gb = project(xs, pos_s, norm_w[d], w_in[d])
        oa, k_new, v_new = sample_window_attention(qa, ka, va, cache_k_win[d], cache_v_win[d], attn_sinks[d])
        ob, r_fin = retention(qb, kb, vb, state_ret[d])
        xs = merge(xs, oa, ga, ob, gb, ret_norm_w[d], w_out[d])
        ks_l.append(k_new.astype(cache_k_win.dtype))
        vs_l.append(v_new.astype(cache_v_win.dtype))
        rs_l.append(r_fin.astype(state_ret.dtype))
    y_prompt = rmsnorm(xp, final_norm_w)
    y_sample = rmsnorm(xs, final_norm_w)
    return (y_prompt, y_sample,
            jnp.stack(kp_l), jnp.stack(vp_l), jnp.stack(rp_l),
            jnp.stack(ks_l), jnp.stack(vs_l), jnp.stack(rs_l))
```

```python
import functools
import math

import jax
import jax.numpy as jnp
from jax import lax
from jax.experimental import pallas as pl
from jax.experimental.pallas import tpu as pltpu

F32 = jnp.float32
BF16 = jnp.bfloat16

D_MODEL = 2048
DEPTH = 2
PAST_LEN = 8192
WINDOW = 128
HD_A = 64
N_HEADS_A = 16
N_KV_A = 2
GROUP_A = N_HEADS_A // N_KV_A
W_A = N_HEADS_A * HD_A
HD_B = 128
N_HEADS_B = 8
W_B = N_HEADS_B * HD_B
RET_CHUNK = 128
ROPE_THETA = 10000.0
EPS = 1e-6

LANES = 128
COL_TILE = 512
SEC_QA, SEC_GA, SEC_QB, SEC_KB, SEC_VB, SEC_GB = 0, 1, 2, 3, 4, 5
N_SEC_TILES = W_A // COL_TILE
KV_TILE = 6 * N_SEC_TILES
D_PROJ_R = (KV_TILE + 1) * COL_TILE
VMEM_LIMIT = 56 * 1024 * 1024

LOG_G = [math.log1p(-(2.0 ** (-5.0 - h))) for h in range(N_HEADS_B)]


def _silu(g):
    return g * (1.0 / (1.0 + jnp.exp(-g)))


def _rope_a(x, c, s):
    lane = lax.broadcasted_iota(jnp.int32, x.shape, 1)
    partner = jnp.where((lane % HD_A) < HD_A // 2,
                        pltpu.roll(x, LANES - HD_A // 2, 1), pltpu.roll(x, HD_A // 2, 1))
    return x * c + partner * s


def _rope_b(x, c, s):
    return x * c + pltpu.roll(x, HD_B // 2, 1) * s


def _in_proj_kernel(x_ref, nw_ref, w_ref, ca_ref, sa_ref, cb_ref, sb_ref,
                    o_ref, tail_ref, h_ref, acc_ref, *, row_chunk):
    j = pl.program_id(1)
    tm = x_ref.shape[0]
    n_chunks = tm // row_chunk
    n_groups = COL_TILE // LANES

    @pl.when(j == 0)
    def _():
        def body(r, carry):
            rows = pl.ds(pl.multiple_of(r * row_chunk, row_chunk), row_chunk)
            x = x_ref[rows, :]
            ms = jnp.mean(x * x, axis=-1, keepdims=True)
            h_ref[rows, :] = (x * lax.rsqrt(ms + EPS) * nw_ref[...]).astype(BF16)
            return carry
        lax.fori_loop(0, n_chunks, body, 0)

    acc_ref[...] = jnp.dot(h_ref[...], w_ref[...], preferred_element_type=F32)

    def epilogue(group_fn):
        def body(r, carry):
            rows = pl.ds(pl.multiple_of(r * row_chunk, row_chunk), row_chunk)
            for g in range(n_groups):
                cols = slice(g * LANES, (g + 1) * LANES)
                o_ref[rows, cols] = group_fn(acc_ref[rows, cols], rows, g).astype(o_ref.dtype)
            return carry
        lax.fori_loop(0, n_chunks, body, 0)

    def plain(x, rows, g):
        return x

    def rope_a(x, rows, g):
        return _rope_a(x, ca_ref[rows, :], sa_ref[rows, :])

    def rope_b(x, rows, g):
        return _rope_b(x, cb_ref[rows, :], sb_ref[rows, :])

    def rope_b_scaled(x, rows, g):
        return rope_b(x, rows, g) * (HD_B ** -0.5)

    def kv_tile(x, rows, g):
        return rope_a(x, rows, g) if g < n_groups // 2 else x

    def sec(s):
        return (j >= s * N_SEC_TILES) & (j < (s + 1) * N_SEC_TILES)

    pl.when(sec(SEC_QA))(lambda: epilogue(rope_a))
    pl.when(sec(SEC_QB))(lambda: epilogue(rope_b))
    pl.when(sec(SEC_KB))(lambda: epilogue(rope_b_scaled))
    pl.when(sec(SEC_GA) | sec(SEC_VB) | sec(SEC_GB))(lambda: epilogue(plain))

    @pl.when(j == KV_TILE)
    def _():
        epilogue(kv_tile)
        rows = slice(tm - WINDOW, tm)
        for g in range(n_groups):
            cols = slice(g * LANES, (g + 1) * LANES)
            x = acc_ref[rows, cols]
            if g < n_groups // 2:
                x = _rope_a(x, ca_ref[rows, :], sa_ref[rows, :])
            tail_ref[:, cols] = x


def _in_proj(x, norm_w, w, tables, out_dtype, tm):
    m = x.shape[0]
    n_tiles = D_PROJ_R // COL_TILE
    tab_spec = pl.BlockSpec((tm, LANES), lambda i, j: (i, 0))
    return pl.pallas_call(
        functools.partial(_in_proj_kernel, row_chunk=min(tm, 256)),
        grid=(m // tm, n_tiles),
        in_specs=[
            pl.BlockSpec((tm, D_MODEL), lambda i, j: (i, 0)),
            pl.BlockSpec((1, D_MODEL), lambda i, j: (0, 0)),
            pl.BlockSpec((D_MODEL, COL_TILE), lambda i, j: (0, j)),
            tab_spec, tab_spec, tab_spec, tab_spec,
        ],
        out_specs=[
            pl.BlockSpec((tm, COL_TILE), lambda i, j: (i, j)),
            pl.BlockSpec((WINDOW, COL_TILE), lambda i, j: (0, 0)),
        ],
        out_shape=[
            jax.ShapeDtypeStruct((m, D_PROJ_R), out_dtype),
            jax.ShapeDtypeStruct((WINDOW, COL_TILE), F32),
        ],
        scratch_shapes=[
            pltpu.VMEM((tm, D_MODEL), BF16),
            pltpu.VMEM((tm, COL_TILE), F32),
        ],
        compiler_params=pltpu.CompilerParams(
            dimension_semantics=("arbitrary", "arbitrary"),
            vmem_limit_bytes=VMEM_LIMIT),
        name="in_proj",
    )(x, norm_w, w, *tables)


def _out_proj_kernel(y_ref, x_ref, w_ref, fw_ref, o_ref, *, final):
    acc = x_ref[...] + jnp.dot(y_ref[...], w_ref[...], preferred_element_type=F32)
    if final:
        ms = jnp.mean(acc * acc, axis=-1, keepdims=True)
        acc = acc * lax.rsqrt(ms + EPS) * fw_ref[...]
    o_ref[...] = acc


def _out_proj(y, x, w, final_w, final, tm):
    m = x.shape[0]
    return pl.pallas_call(
        functools.partial(_out_proj_kernel, final=final),
        grid=(m // tm,),
        in_specs=[
            pl.BlockSpec((tm, D_MODEL), lambda i: (i, 0)),
            pl.BlockSpec((tm, D_MODEL), lambda i: (i, 0)),
            pl.BlockSpec((D_MODEL, D_MODEL), lambda i: (0, 0)),
            pl.BlockSpec((1, D_MODEL), lambda i: (0, 0)),
        ],
        out_specs=pl.BlockSpec((tm, D_MODEL), lambda i: (i, 0)),
        out_shape=jax.ShapeDtypeStruct((m, D_MODEL), F32),
        compiler_params=pltpu.CompilerParams(
            dimension_semantics=("arbitrary",),
            vmem_limit_bytes=VMEM_LIMIT),
        name="out_proj",
    )(y, x, w, final_w)


def _nt_dot(a, b):
    return lax.dot_general(a, b, (((1,), (1,)), ((), ())), preferred_element_type=F32)


def _tn_dot(a, b):
    return lax.dot_general(a, b, (((0,), (0,)), ((), ())), preferred_element_type=F32)


def _mix_prompt_kernel(sink_ref, qa_ref, ga_ref, qb_ref, kb_ref, vb_ref, gb_ref,
                       kvc_ref, kvp_ref, gnw_ref, y_ref, r_ref,
                       dmask_ref, qdec_ref, kdec_ref):
    c = pl.program_id(0)
    C = RET_CHUNK
    row = lax.broadcasted_iota(jnp.int32, (C, LANES), 0)
    lane = lax.broadcasted_iota(jnp.int32, (C, LANES), 1)

    @pl.when(c == 0)
    def _():
        r_ref[...] = jnp.zeros(r_ref.shape, F32)
        rowf = row.astype(F32)
        diff = rowf - lane.astype(F32)
        for h in range(N_HEADS_B):
            dmask_ref[h] = jnp.where(diff >= 0, jnp.exp(LOG_G[h] * jnp.maximum(diff, 0.0)), 0.0)
            qdec_ref[h] = jnp.exp(LOG_G[h] * (rowf + 1.0))
            kdec_ref[h] = jnp.exp(LOG_G[h] * (C - 1.0 - rowf))

    tri = lane <= row
    lo = lane < HD_A
    has_prev = c > 0
    neg_inf = jnp.float32(-jnp.inf)
    for g in range(N_KV_A):
        gc = slice(g * LANES, (g + 1) * LANES)
        vcol = slice((N_KV_A + g) * LANES, (N_KV_A + g + 1) * LANES)
        k_cur, k_prev = kvc_ref[:, gc], kvp_ref[:, gc]
        v_cur, v_prev = kvc_ref[:, vcol], kvp_ref[:, vcol]
        for pi in range(GROUP_A // 2):
            pc = slice((g * (GROUP_A // 2) + pi) * LANES, (g * (GROUP_A // 2) + pi + 1) * LANES)
            q_pair = qa_ref[:, pc].astype(F32)
            halves = []
            for half in range(2):
                h = g * GROUP_A + 2 * pi + half
                q = jnp.where(lo if half == 0 else ~lo, q_pair, 0.0).astype(BF16)
                s_prev = jnp.where(has_prev, _nt_dot(q, k_prev), neg_inf)
                s = jnp.where(tri, _nt_dot(q, k_cur), s_prev) * (HD_A ** -0.5)
                sink = sink_ref[h]
                m = jnp.maximum(jnp.max(s, axis=-1, keepdims=True), sink)
                e = jnp.exp(s - m)
                denom = jnp.sum(e, axis=-1, keepdims=True) + jnp.exp(sink - m)
                p = e / denom
                o = (jnp.dot(jnp.where(tri, p, 0.0).astype(BF16), v_cur, preferred_element_type=F32)
                     + jnp.dot(jnp.where(tri, 0.0, p).astype(BF16), v_prev, preferred_element_type=F32))
                halves.append(o)
            o_pair = jnp.where(lo, halves[0], halves[1])
            y_ref[:, pc] = (o_pair * _silu(ga_ref[:, pc].astype(F32))).astype(y_ref.dtype)

    for h in range(N_HEADS_B):
        hc = slice(h * HD_B, (h + 1) * HD_B)
        q, k, v = qb_ref[:, hc], kb_ref[:, hc], vb_ref[:, hc]
        sc = (_nt_dot(q, k) * dmask_ref[h]).astype(BF16)
        intra = jnp.dot(sc, v, preferred_element_type=F32)
        r = r_ref[h]
        inter = jnp.dot(q, r.astype(BF16), preferred_element_type=F32) * qdec_ref[h]
        kd = (k.astype(F32) * kdec_ref[h]).astype(BF16)
        r_ref[h] = math.exp(LOG_G[h] * C) * r + _tn_dot(kd, v)
        o = intra + inter
        mu = jnp.mean(o, axis=-1, keepdims=True)
        d = o - mu
        var = jnp.mean(d * d, axis=-1, keepdims=True)
        yh = d * lax.rsqrt(var + EPS) * gnw_ref[:, hc] * _silu(gb_ref[:, hc].astype(F32))
        y_ref[:, W_A + h * HD_B: W_A + (h + 1) * HD_B] = yh.astype(y_ref.dtype)


def _mix_prompt(p, sinks, gn_w):
    t = p.shape[0]
    C = RET_CHUNK
    sec_spec = lambda s: pl.BlockSpec((C, W_A), lambda c, s=s: (c, s))
    return pl.pallas_call(
        _mix_prompt_kernel,
        grid=(t // C,),
        in_specs=[
            pl.BlockSpec(memory_space=pltpu.SMEM),
            sec_spec(SEC_QA), sec_spec(SEC_GA), sec_spec(SEC_QB),
            sec_spec(SEC_KB), sec_spec(SEC_VB), sec_spec(SEC_GB),
            pl.BlockSpec((C, COL_TILE), lambda c: (c, KV_TILE)),
            pl.BlockSpec((C, COL_TILE), lambda c: (jnp.maximum(c - 1, 0), KV_TILE)),
            pl.BlockSpec((1, W_B), lambda c: (0, 0)),
        ],
        out_specs=[
            pl.BlockSpec((C, D_MODEL), lambda c: (c, 0)),
            pl.BlockSpec((N_HEADS_B, HD_B, HD_B), lambda c: (0, 0, 0)),
        ],
        out_shape=[
            jax.ShapeDtypeStruct((t, D_MODEL), BF16),
            jax.ShapeDtypeStruct((N_HEADS_B, HD_B, HD_B), F32),
        ],
        scratch_shapes=[pltpu.VMEM((N_HEADS_B, C, LANES), F32)] * 3,
        compiler_params=pltpu.CompilerParams(
            dimension_semantics=("arbitrary",),
            vmem_limit_bytes=VMEM_LIMIT),
        name="mix_prompt",
    )(sinks, p, p, p, p, p, p, p, p, gn_w)


SAMPLE_BB = 8
LANE_BATCH = LANES // 4


def _mix_sample_kernel(sinkrow_ref, q3_ref, ga3_ref, qb_ref, kb_ref, vb_ref, gb_ref, kvn_ref,
                       kbt_ref, vbg_ref, ck_ref, cv_ref, st_ref, gnw_ref,
                       ya3_ref, yb_ref, nk_ref, nv_ref, nst_ref, *, t_new):
    i = pl.program_id(0)
    T = t_new
    L = ck_ref.shape[1]
    n_keys = L + T
    lo = lax.broadcasted_iota(jnp.int32, (T, LANES), 1) < HD_A

    rows_q = lax.broadcasted_iota(jnp.int32, (T * GROUP_A, n_keys), 0) // GROUP_A
    key = lax.broadcasted_iota(jnp.int32, (T * GROUP_A, n_keys), 1)
    delta = jnp.where(key < L, rows_q + L - key, rows_q - (key - L))
    valid = (delta >= 0) & (delta < WINDOW)
    for bb in range(SAMPLE_BB):
        tr = slice(bb * T, (bb + 1) * T)
        kvn = kvn_ref[tr, :]
        k_new = jnp.where(lo, kvn[:, 0:LANES], kvn[:, LANES:2 * LANES])
        v_new = jnp.where(lo, kvn[:, 2 * LANES:3 * LANES], kvn[:, 3 * LANES:4 * LANES])
        k_all = jnp.concatenate([ck_ref[bb], k_new], axis=0)
        v_all = jnp.concatenate([cv_ref[bb], v_new], axis=0)
        nk_ref[bb] = k_all[T:, :]
        nv_ref[bb] = v_all[T:, :]
        for g in range(N_KV_A):
            hs = slice(g * GROUP_A, (g + 1) * GROUP_A)
            gl = slice(g * HD_A, (g + 1) * HD_A)
            q = q3_ref[tr, hs, :].reshape(T * GROUP_A, HD_A).astype(BF16)
            s = _nt_dot(q, k_all[:, gl].astype(BF16)) * (HD_A ** -0.5)
            s = jnp.where(valid, s, -jnp.inf)
            sink = sinkrow_ref[g]
            m = jnp.maximum(jnp.max(s, axis=-1, keepdims=True), sink)
            e = jnp.exp(s - m)
            denom = jnp.sum(e, axis=-1, keepdims=True) + jnp.exp(sink - m)
            p = (e / denom).astype(BF16)
            o = jnp.dot(p, v_all[:, gl].astype(BF16), preferred_element_type=F32)
            gate = _silu(ga3_ref[tr, hs, :].reshape(T * GROUP_A, HD_A))
            ya3_ref[tr, hs, :] = (o * gate).reshape(T, GROUP_A, HD_A)

    tpos = lax.broadcasted_iota(jnp.int32, (2 * T, 1), 0) % T
    tposf = tpos.astype(F32)
    lane_t = (lax.broadcasted_iota(jnp.int32, (1, LANES), 1) % T).astype(F32)
    lane_b = lax.broadcasted_iota(jnp.int32, (HD_B, LANES), 1) // T
    upper = lax.broadcasted_iota(jnp.int32, (2 * T, 1), 0) < T
    for pair in range(SAMPLE_BB // 2):
        pr = slice(pair * 2 * T, (pair + 1) * 2 * T)
        for h in range(N_HEADS_B):
            hc = slice(h * HD_B, (h + 1) * HD_B)
            lg = LOG_G[h]
            q8, k8, v8 = qb_ref[pr, hc], kb_ref[pr, hc], vb_ref[pr, hc]
            q8b = q8.astype(BF16)
            intra = jnp.zeros((2 * T, HD_B), F32)
            for mm in range(T):
                k_m = jnp.where(upper, k8[mm:mm + 1, :], k8[T + mm:T + mm + 1, :])
                v_m = jnp.where(upper, v8[mm:mm + 1, :], v8[T + mm:T + mm + 1, :])
                k_m = k_m.astype(BF16).astype(F32)
                v_m = v_m.astype(BF16).astype(F32)
                sc = jnp.sum(q8b.astype(F32) * k_m, axis=-1, keepdims=True)
                dm = jnp.where(tpos >= mm, jnp.exp(lg * jnp.maximum(tposf - mm, 0.0)), 0.0)
                intra = intra + (sc * dm).astype(BF16).astype(F32) * v_m
            qdec = jnp.exp(lg * (tposf + 1.0))
            outs = []
            for e in range(2):
                bb = 2 * pair + e
                r = st_ref[bb, h]
                outs.append(jnp.dot(q8b, r.astype(BF16), preferred_element_type=F32))
                slot = (i % (LANE_BATCH // SAMPLE_BB)) * SAMPLE_BB + bb
                kdec = jnp.exp(lg * (T - 1.0 - lane_t))
                kt = jnp.where(lane_b == slot, kbt_ref[hc, :] * kdec, 0.0).astype(BF16)
                u = jnp.dot(kt, vbg_ref[:, hc].astype(BF16), preferred_element_type=F32)
                nst_ref[bb, h] = math.exp(lg * T) * r + u
            inter = jnp.where(upper, outs[0], outs[1]) * qdec
            o = intra + inter
            mu = jnp.mean(o, axis=-1, keepdims=True)
            d = o - mu
            var = jnp.mean(d * d, axis=-1, keepdims=True)
            yh = d * lax.rsqrt(var + EPS) * gnw_ref[:, hc] * _silu(gb_ref[pr, hc])
            yb_ref[pr, hc] = yh


def _mix_sample(ps, sinks, gn_w, cache_k, cache_v, state, t_new):
    m = ps.shape[0]
    nb = m // t_new
    L = cache_k.shape[1]
    BB = SAMPLE_BB
    rows = BB * t_new
    qa = ps[:, SEC_QA * W_A:(SEC_QA + 1) * W_A].reshape(m, N_HEADS_A, HD_A)
    ga = ps[:, SEC_GA * W_A:(SEC_GA + 1) * W_A].reshape(m, N_HEADS_A, HD_A)
    kbt = ps[:, SEC_KB * W_A:(SEC_KB + 1) * W_A].T
    sinkrow = jnp.broadcast_to(
        jnp.tile(sinks.reshape(N_KV_A, 1, GROUP_A), (1, t_new, 1)).reshape(N_KV_A, t_new * GROUP_A, 1),
        (N_KV_A, t_new * GROUP_A, 1))
    sec_spec = lambda s: pl.BlockSpec((rows, W_A), lambda i, s=s: (i, s))
    steps_per_lane_group = LANE_BATCH // BB
    return pl.pallas_call(
        functools.partial(_mix_sample_kernel, t_new=t_new),
        grid=(nb // BB,),
        in_specs=[
            pl.BlockSpec((N_KV_A, t_new * GROUP_A, 1), lambda i: (0, 0, 0)),
            pl.BlockSpec((rows, N_HEADS_A, HD_A), lambda i: (i, 0, 0)),
            pl.BlockSpec((rows, N_HEADS_A, HD_A), lambda i: (i, 0, 0)),
            sec_spec(SEC_QB), sec_spec(SEC_KB), sec_spec(SEC_VB), sec_spec(SEC_GB),
            pl.BlockSpec((rows, COL_TILE), lambda i: (i, KV_TILE)),
            pl.BlockSpec((W_B, LANES), lambda i: (0, i // steps_per_lane_group)),
            pl.BlockSpec((LANES, W_B), lambda i: (i // steps_per_lane_group, SEC_VB)),
            pl.BlockSpec((BB, L, LANES), lambda i: (i, 0, 0)),
            pl.BlockSpec((BB, L, LANES), lambda i: (i, 0, 0)),
            pl.BlockSpec((BB, N_HEADS_B, HD_B, HD_B), lambda i: (i, 0, 0, 0)),
            pl.BlockSpec((1, W_B), lambda i: (0, 0)),
        ],
        out_specs=[
            pl.BlockSpec((rows, N_HEADS_A, HD_A), lambda i: (i, 0, 0)),
            pl.BlockSpec((rows, W_B), lambda i: (i, 0)),
            pl.BlockSpec((BB, L, LANES), lambda i: (i, 0, 0)),
            pl.BlockSpec((BB, L, LANES), lambda i: (i, 0, 0)),
            pl.BlockSpec((BB, N_HEADS_B, HD_B, HD_B), lambda i: (i, 0, 0, 0)),
        ],
        out_shape=[
            jax.ShapeDtypeStruct((m, N_HEADS_A, HD_A), F32),
            jax.ShapeDtypeStruct((m, W_B), F32),
            jax.ShapeDtypeStruct((nb, L, LANES), F32),
            jax.ShapeDtypeStruct((nb, L, LANES), F32),
            jax.ShapeDtypeStruct(state.shape, F32),
        ],
        compiler_params=pltpu.CompilerParams(
            dimension_semantics=("arbitrary",),
            vmem_limit_bytes=VMEM_LIMIT),
        name="mix_sample",
    )(sinkrow, qa, ga, ps, ps, ps, ps, ps, kbt, ps, cache_k, cache_v, state, gn_w)


def _prep_w_in(w_in):
    cuts = [0, W_A, W_A + N_KV_A * HD_A, W_A + 2 * N_KV_A * HD_A]
    qa = w_in[..., cuts[0]:cuts[1]]
    ka = w_in[..., cuts[1]:cuts[2]]
    va = w_in[..., cuts[2]:cuts[3]]
    rest = w_in[..., cuts[3]:]
    ga, qb, kb, vb, gb = [rest[..., s * W_A:(s + 1) * W_A] for s in range(5)]
    dup = lambda a: [a[..., g * HD_A:(g + 1) * HD_A] for g in range(N_KV_A) for _ in range(2)]
    return jnp.concatenate([qa, ga, qb, kb, vb, gb] + dup(ka) + dup(va), axis=-1).astype(BF16)


def _rope_tables(pos):
    def tab(hd):
        half = hd // 2
        inv = ROPE_THETA ** (-jnp.arange(half, dtype=F32) / half)
        ang = pos.astype(F32)[:, None] * inv[None, :]
        cos, sin = jnp.cos(ang), jnp.sin(ang)
        reps = LANES // hd
        return (jnp.tile(jnp.concatenate([cos, cos], axis=1), (1, reps)),
                jnp.tile(jnp.concatenate([-sin, sin], axis=1), (1, reps)))
    return tab(HD_A) + tab(HD_B)


def kernel(x_prompt, x_sample, cache_k_win, cache_v_win, state_ret, norm_w, w_in, attn_sinks,
           ret_norm_w, w_out, final_norm_w):
    bp, tp, _ = x_prompt.shape
    bs, ts, _ = x_sample.shape
    assert bp == 1 and tp % RET_CHUNK == 0 and min(WINDOW, tp) == WINDOW
    L = cache_k_win.shape[2]
    depth = w_in.shape[0]

    w_in_r = _prep_w_in(w_in)
    w_out_b = w_out.astype(BF16)
    tab_p = _rope_tables(jnp.arange(tp, dtype=jnp.int32))
    tab_s = _rope_tables(jnp.tile(PAST_LEN + jnp.arange(ts, dtype=jnp.int32), bs))
    fw = final_norm_w.reshape(1, D_MODEL)

    xp = x_prompt.reshape(tp, D_MODEL)
    xs = x_sample.reshape(bs * ts, D_MODEL)
    ck = cache_k_win.reshape(depth, bs, L, N_KV_A * HD_A)
    cv = cache_v_win.reshape(depth, bs, L, N_KV_A * HD_A)

    kp_l, vp_l, rp_l, ks_l, vs_l, rs_l = [], [], [], [], [], []
    for d in range(depth):
        final = d == depth - 1
        nw = norm_w[d].reshape(1, D_MODEL)
        gnw = ret_norm_w[d].reshape(1, W_B)
        p, tail = _in_proj(xp, nw, w_in_r[d], tab_p, BF16, tm=1024)
        y, r_fin = _mix_prompt(p, attn_sinks[d], gnw)
        xp = _out_proj(y, xp, w_out_b[d], fw, final, tm=512)
        kp_l.append(jnp.stack([tail[:, 0:HD_A], tail[:, LANES:LANES + HD_A]], axis=1))
        vp_l.append(jnp.stack([tail[:, 2 * LANES:2 * LANES + HD_A], tail[:, 3 * LANES:3 * LANES + HD_A]], axis=1))
        rp_l.append(r_fin)
        ps, _ = _in_proj(xs, nw, w_in_r[d], tab_s, F32, tm=bs * ts)
        ya3, yb, nk, nv, nst = _mix_sample(ps, attn_sinks[d], gnw, ck[d], cv[d], state_ret[d], ts)
        ys = jnp.concatenate([ya3.reshape(bs * ts, W_A), yb], axis=1).astype(BF16)
        xs = _out_proj(ys, xs, w_out_b[d], fw, final, tm=bs * ts)
        ks_l.append(nk.reshape(bs, L, N_KV_A, HD_A))
        vs_l.append(nv.reshape(bs, L, N_KV_A, HD_A))
        rs_l.append(nst)

    keep = min(WINDOW, tp)
    return (xp.reshape(bp, tp, D_MODEL), xs.reshape(bs, ts, D_MODEL),
            jnp.stack(kp_l).reshape(depth, bp, keep, N_KV_A, HD_A),
            jnp.stack(vp_l).reshape(depth, bp, keep, N_KV_A, HD_A),
            jnp.stack(rp_l).reshape(depth, bp, N_HEADS_B, HD_B, HD_B),
            jnp.stack(ks_l), jnp.stack(vs_l), jnp.stack(rs_l))
```

```python
import functools
import math

import jax
import jax.numpy as jnp
from jax import lax
from jax.experimental import pallas as pl
from jax.experimental.pallas import tpu as pltpu

F32 = jnp.float32
BF16 = jnp.bfloat16

D_MODEL = 2048
DEPTH = 2
PAST_LEN = 8192
WINDOW = 128
HD_A = 64
N_HEADS_A = 16
N_KV_A = 2
GROUP_A = N_HEADS_A // N_KV_A
W_A = N_HEADS_A * HD_A
HD_B = 128
N_HEADS_B = 8
W_B = N_HEADS_B * HD_B
RET_CHUNK = 128
ROPE_THETA = 10000.0
EPS = 1e-6

LANES = 128
COL_TILE = 512
SEC_QA, SEC_GA, SEC_QB, SEC_KB, SEC_VB, SEC_GB = 0, 1, 2, 3, 4, 5
N_SEC_TILES = W_A // COL_TILE
KV_TILE = 6 * N_SEC_TILES
D_PROJ_R = (KV_TILE + 1) * COL_TILE
VMEM_LIMIT = 56 * 1024 * 1024

LOG_G = [math.log1p(-(2.0 ** (-5.0 - h))) for h in range(N_HEADS_B)]


def _silu(g):
    return g * (1.0 / (1.0 + jnp.exp(-g)))


def _rope_a(x, c, s):
    lane = lax.broadcasted_iota(jnp.int32, x.shape, 1)
    partner = jnp.where((lane % HD_A) < HD_A // 2,
                        pltpu.roll(x, LANES - HD_A // 2, 1), pltpu.roll(x, HD_A // 2, 1))
    return x * c + partner * s


def _rope_b(x, c, s):
    return x * c + pltpu.roll(x, HD_B // 2, 1) * s


def _in_proj_kernel(x_ref, nw_ref, wa_ref, wb_ref, wkv_ref, ca_ref, sa_ref, cb_ref, sb_ref,
                    o_ref, tail_ref, h_ref, acc_ref, *, row_chunk):
    j = pl.program_id(1)
    tm = x_ref.shape[0]
    n_chunks = tm // row_chunk
    n_groups = COL_TILE // LANES

    @pl.when(j == 0)
    def _():
        def body(r, carry):
            rows = pl.ds(pl.multiple_of(r * row_chunk, row_chunk), row_chunk)
            x = x_ref[rows, :]
            ms = jnp.mean(x * x, axis=-1, keepdims=True)
            h_ref[rows, :] = (x * lax.rsqrt(ms + EPS) * nw_ref[...]).astype(BF16)
            return carry
        lax.fori_loop(0, n_chunks, body, 0)

    @pl.when(j < KV_TILE)
    def _():
        half = COL_TILE // 2
        acc_ref[:, :half] = jnp.dot(h_ref[...], wa_ref[...], preferred_element_type=F32)
        acc_ref[:, half:] = jnp.dot(h_ref[...], wb_ref[...], preferred_element_type=F32)

    @pl.when(j == KV_TILE)
    def _():
        acc_ref[...] = jnp.dot(h_ref[...], wkv_ref[...], preferred_element_type=F32)

    def epilogue(group_fn):
        def body(r, carry):
            rows = pl.ds(pl.multiple_of(r * row_chunk, row_chunk), row_chunk)
            for g in range(n_groups):
                cols = slice(g * LANES, (g + 1) * LANES)
                o_ref[rows, cols] = group_fn(acc_ref[rows, cols], rows, g).astype(o_ref.dtype)
            return carry
        lax.fori_loop(0, n_chunks, body, 0)

    def plain(x, rows, g):
        return x

    def rope_a(x, rows, g):
        return _rope_a(x, ca_ref[rows, :], sa_ref[rows, :])

    def rope_b(x, rows, g):
        return _rope_b(x, cb_ref[rows, :], sb_ref[rows, :])

    def rope_b_scaled(x, rows, g):
        return rope_b(x, rows, g) * (HD_B ** -0.5)

    def kv_tile(x, rows, g):
        return rope_a(x, rows, g) if g < n_groups // 2 else x

    def sec(s):
        return (j >= s * N_SEC_TILES) & (j < (s + 1) * N_SEC_TILES)

    pl.when(sec(SEC_QA))(lambda: epilogue(rope_a))
    pl.when(sec(SEC_QB))(lambda: epilogue(rope_b))
    pl.when(sec(SEC_KB))(lambda: epilogue(rope_b_scaled))
    pl.when(sec(SEC_GA) | sec(SEC_VB) | sec(SEC_GB))(lambda: epilogue(plain))

    @pl.when(j == KV_TILE)
    def _():
        epilogue(kv_tile)
        rows = slice(tm - WINDOW, tm)
        for g in range(n_groups):
            cols = slice(g * LANES, (g + 1) * LANES)
            x = acc_ref[rows, cols]
            if g < n_groups // 2:
                x = _rope_a(x, ca_ref[rows, :], sa_ref[rows, :])
            tail_ref[:, cols] = x


def _in_proj(x, norm_w, w_all, wkv_all, d, tables, out_dtype, tm):
    m = x.shape[0]
    n_tiles = D_PROJ_R // COL_TILE
    half = COL_TILE // 2
    n_src_blocks = w_all.shape[2] // half
    tab_spec = pl.BlockSpec((tm, LANES), lambda i, j: (i, 0))

    def src_block(j, e):
        sec = j // N_SEC_TILES
        blk = sec * (W_A // half) + jnp.minimum(sec, 1) + (j % N_SEC_TILES) * 2 + e
        return jnp.minimum(blk, n_src_blocks - 1)

    return pl.pallas_call(
        functools.partial(_in_proj_kernel, row_chunk=min(tm, 256)),
        grid=(m // tm, n_tiles),
        in_specs=[
            pl.BlockSpec((tm, D_MODEL), lambda i, j: (i, 0)),
            pl.BlockSpec((1, D_MODEL), lambda i, j: (0, 0)),
            pl.BlockSpec((None, D_MODEL, half), lambda i, j: (d, 0, src_block(j, 0))),
            pl.BlockSpec((None, D_MODEL, half), lambda i, j: (d, 0, src_block(j, 1))),
            pl.BlockSpec((None, D_MODEL, COL_TILE), lambda i, j: (d, 0, 0)),
            tab_spec, tab_spec, tab_spec, tab_spec,
        ],
        out_specs=[
            pl.BlockSpec((tm, COL_TILE), lambda i, j: (i, j)),
            pl.BlockSpec((WINDOW, COL_TILE), lambda i, j: (0, 0)),
        ],
        out_shape=[
            jax.ShapeDtypeStruct((m, D_PROJ_R), out_dtype),
            jax.ShapeDtypeStruct((WINDOW, COL_TILE), F32),
        ],
        scratch_shapes=[
            pltpu.VMEM((tm, D_MODEL), BF16),
            pltpu.VMEM((tm, COL_TILE), F32),
        ],
        compiler_params=pltpu.CompilerParams(
            dimension_semantics=("arbitrary", "arbitrary"),
            vmem_limit_bytes=VMEM_LIMIT),
        name="in_proj",
    )(x, norm_w, w_all, w_all, wkv_all, *tables)


def _out_proj_kernel(y_ref, x_ref, w_ref, fw_ref, o_ref, *, final):
    acc = x_ref[...] + jnp.dot(y_ref[...], w_ref[...], preferred_element_type=F32)
    if final:
        ms = jnp.mean(acc * acc, axis=-1, keepdims=True)
        acc = acc * lax.rsqrt(ms + EPS) * fw_ref[...]
    o_ref[...] = acc


def _out_proj(y, x, w, final_w, final, tm):
    m = x.shape[0]
    return pl.pallas_call(
        functools.partial(_out_proj_kernel, final=final),
        grid=(m // tm,),
        in_specs=[
            pl.BlockSpec((tm, D_MODEL), lambda i: (i, 0)),
            pl.BlockSpec((tm, D_MODEL), lambda i: (i, 0)),
            pl.BlockSpec((D_MODEL, D_MODEL), lambda i: (0, 0)),
            pl.BlockSpec((1, D_MODEL), lambda i: (0, 0)),
        ],
        out_specs=pl.BlockSpec((tm, D_MODEL), lambda i: (i, 0)),
        out_shape=jax.ShapeDtypeStruct((m, D_MODEL), F32),
        compiler_params=pltpu.CompilerParams(
            dimension_semantics=("arbitrary",),
            vmem_limit_bytes=VMEM_LIMIT),
        name="out_proj",
    )(y, x, w, final_w)


def _nt_dot(a, b):
    return lax.dot_general(a, b, (((1,), (1,)), ((), ())), preferred_element_type=F32)


def _tn_dot(a, b):
    return lax.dot_general(a, b, (((0,), (0,)), ((), ())), preferred_element_type=F32)


def _mix_prompt_kernel(sink_ref, qa_ref, ga_ref, qb_ref, kb_ref, vb_ref, gb_ref,
                       kvc_ref, kvp_ref, gnw_ref, y_ref, r_ref,
                       dmask_ref, qdec_ref, kdec_ref,
                       qs_ref, s_ref, p_ref, o_ref, scb_ref, inter_ref):
    c = pl.program_id(0)
    C = RET_CHUNK
    row = lax.broadcasted_iota(jnp.int32, (C, LANES), 0)
    lane = lax.broadcasted_iota(jnp.int32, (C, LANES), 1)

    @pl.when(c == 0)
    def _():
        r_ref[...] = jnp.zeros(r_ref.shape, F32)
        rowf = row.astype(F32)
        diff = rowf - lane.astype(F32)
        for h in range(N_HEADS_B):
            dmask_ref[h] = jnp.where(diff >= 0, jnp.exp(LOG_G[h] * jnp.maximum(diff, 0.0)), 0.0)
            qdec_ref[h] = jnp.exp(LOG_G[h] * (rowf + 1.0))
            kdec_ref[h] = jnp.exp(LOG_G[h] * (C - 1.0 - rowf))

    tri = lane <= row
    lo = lane < HD_A
    has_prev = c > 0
    neg_inf = jnp.float32(-jnp.inf)
    half_group = GROUP_A // 2
    kcol = lambda g: slice(g * LANES, (g + 1) * LANES)
    vcol = lambda g: slice((N_KV_A + g) * LANES, (N_KV_A + g + 1) * LANES)
    pair_col = lambda g, pi: slice((g * half_group + pi) * LANES, (g * half_group + pi + 1) * LANES)
    head_rows = lambda hh: slice(hh * C, (hh + 1) * C)

    for g in range(N_KV_A):
        for pi in range(half_group):
            q_pair = qa_ref[:, pair_col(g, pi)].astype(F32)
            qs_ref[g, head_rows(2 * pi), :] = jnp.where(lo, q_pair, 0.0).astype(BF16)
            qs_ref[g, head_rows(2 * pi + 1), :] = jnp.where(lo, 0.0, q_pair).astype(BF16)
        qs = qs_ref[g]
        s_ref[g, 0] = _nt_dot(qs, kvc_ref[:, kcol(g)])
        s_ref[g, 1] = _nt_dot(qs, kvp_ref[:, kcol(g)])

    for h in range(N_HEADS_B):
        hc = slice(h * HD_B, (h + 1) * HD_B)
        q, k, v = qb_ref[:, hc], kb_ref[:, hc], vb_ref[:, hc]
        scb_ref[h] = (_nt_dot(q, k) * dmask_ref[h]).astype(BF16)
        r = r_ref[h]
        inter_ref[h] = jnp.dot(q, r.astype(BF16), preferred_element_type=F32) * qdec_ref[h]
        kd = (k.astype(F32) * kdec_ref[h]).astype(BF16)
        r_ref[h] = math.exp(LOG_G[h] * C) * r + _tn_dot(kd, v)

    for g in range(N_KV_A):
        for hh in range(GROUP_A):
            rows = head_rows(hh)
            s_prev = jnp.where(has_prev, s_ref[g, 1, rows, :], neg_inf)
            s = jnp.where(tri, s_ref[g, 0, rows, :], s_prev) * (HD_A ** -0.5)
            sink = sink_ref[g * GROUP_A + hh]
            m = jnp.maximum(jnp.max(s, axis=-1, keepdims=True), sink)
            e = jnp.exp(s - m)
            denom = jnp.sum(e, axis=-1, keepdims=True) + jnp.exp(sink - m)
            p = e / denom
            p_ref[g, 0, rows, :] = jnp.where(tri, p, 0.0).astype(BF16)
            p_ref[g, 1, rows, :] = jnp.where(tri, 0.0, p).astype(BF16)

    for g in range(N_KV_A):
        o_ref[g] = (jnp.dot(p_ref[g, 0], kvc_ref[:, vcol(g)], preferred_element_type=F32)
                    + jnp.dot(p_ref[g, 1], kvp_ref[:, vcol(g)], preferred_element_type=F32))

    for h in range(N_HEADS_B):
        hc = slice(h * HD_B, (h + 1) * HD_B)
        o = jnp.dot(scb_ref[h], vb_ref[:, hc], preferred_element_type=F32) + inter_ref[h]
        mu = jnp.mean(o, axis=-1, keepdims=True)
        d = o - mu
        var = jnp.mean(d * d, axis=-1, keepdims=True)
        yh = d * lax.rsqrt(var + EPS) * gnw_ref[:, hc] * _silu(gb_ref[:, hc].astype(F32))
        y_ref[:, W_A + h * HD_B: W_A + (h + 1) * HD_B] = yh.astype(y_ref.dtype)

    for g in range(N_KV_A):
        for pi in range(half_group):
            pc = pair_col(g, pi)
            o_pair = jnp.where(lo, o_ref[g, head_rows(2 * pi), :], o_ref[g, head_rows(2 * pi + 1), :])
            y_ref[:, pc] = (o_pair * _silu(ga_ref[:, pc].astype(F32))).astype(y_ref.dtype)


def _mix_prompt(p, sinks, gn_w):
    t = p.shape[0]
    C = RET_CHUNK
    sec_spec = lambda s: pl.BlockSpec((C, W_A), lambda c, s=s: (c, s))
    return pl.pallas_call(
        _mix_prompt_kernel,
        grid=(t // C,),
        in_specs=[
            pl.BlockSpec(memory_space=pltpu.SMEM),
            sec_spec(SEC_QA), sec_spec(SEC_GA), sec_spec(SEC_QB),
            sec_spec(SEC_KB), sec_spec(SEC_VB), sec_spec(SEC_GB),
            pl.BlockSpec((C, COL_TILE), lambda c: (c, KV_TILE)),
            pl.BlockSpec((C, COL_TILE), lambda c: (jnp.maximum(c - 1, 0), KV_TILE)),
            pl.BlockSpec((1, W_B), lambda c: (0, 0)),
        ],
        out_specs=[
            pl.BlockSpec((C, D_MODEL), lambda c: (c, 0)),
            pl.BlockSpec((N_HEADS_B, HD_B, HD_B), lambda c: (0, 0, 0)),
        ],
        out_shape=[
            jax.ShapeDtypeStruct((t, D_MODEL), BF16),
            jax.ShapeDtypeStruct((N_HEADS_B, HD_B, HD_B), F32),
        ],
        scratch_shapes=[pltpu.VMEM((N_HEADS_B, C, LANES), F32)] * 3 + [
            pltpu.VMEM((N_KV_A, GROUP_A * C, LANES), BF16),
            pltpu.VMEM((N_KV_A, 2, GROUP_A * C, LANES), F32),
            pltpu.VMEM((N_KV_A, 2, GROUP_A * C, LANES), BF16),
            pltpu.VMEM((N_KV_A, GROUP_A * C, LANES), F32),
            pltpu.VMEM((N_HEADS_B, C, HD_B), BF16),
            pltpu.VMEM((N_HEADS_B, C, HD_B), F32),
        ],
        compiler_params=pltpu.CompilerParams(
            dimension_semantics=("arbitrary",),
            vmem_limit_bytes=VMEM_LIMIT),
        name="mix_prompt",
    )(sinks, p, p, p, p, p, p, p, p, gn_w)


SAMPLE_BB = 8
LANE_BATCH = LANES // 4


def _mix_sample_kernel(sinkrow_ref, q3_ref, ga3_ref, qb_ref, kb_ref, vb_ref, gb_ref, kvn_ref,
                       kbt_ref, vbg_ref, ck_ref, cv_ref, st_ref, gnw_ref, *rest, t_new):
    ya3_ref, yb_ref, nk_ref, nv_ref, nst_ref = rest[-5:]
    i = pl.program_id(0)
    T = t_new
    L = ck_ref.shape[1]
    n_keys = L + T
    lo = lax.broadcasted_iota(jnp.int32, (T, LANES), 1) < HD_A

    rows_q = lax.broadcasted_iota(jnp.int32, (T * GROUP_A, n_keys), 0) // GROUP_A
    key = lax.broadcasted_iota(jnp.int32, (T * GROUP_A, n_keys), 1)
    delta = jnp.where(key < L, rows_q + L - key, rows_q - (key - L))
    valid = (delta >= 0) & (delta < WINDOW)
    for bb in range(SAMPLE_BB):
        tr = slice(bb * T, (bb + 1) * T)
        kvn = kvn_ref[tr, :]
        k_new = jnp.where(lo, kvn[:, 0:LANES], kvn[:, LANES:2 * LANES])
        v_new = jnp.where(lo, kvn[:, 2 * LANES:3 * LANES], kvn[:, 3 * LANES:4 * LANES])
        k_all = jnp.concatenate([ck_ref[bb], k_new], axis=0)
        v_all = jnp.concatenate([cv_ref[bb], v_new], axis=0)
        nk_ref[bb] = k_all[T:, :]
        nv_ref[bb] = v_all[T:, :]
        for g in range(N_KV_A):
            hs = slice(g * GROUP_A, (g + 1) * GROUP_A)
            gl = slice(g * HD_A, (g + 1) * HD_A)
            q = q3_ref[tr, hs, :].reshape(T * GROUP_A, HD_A).astype(BF16)
            s = _nt_dot(q, k_all[:, gl].astype(BF16)) * (HD_A ** -0.5)
            s = jnp.where(valid, s, -jnp.inf)
            sink = sinkrow_ref[g]
            m = jnp.maximum(jnp.max(s, axis=-1, keepdims=True), sink)
            e = jnp.exp(s - m)
            denom = jnp.sum(e, axis=-1, keepdims=True) + jnp.exp(sink - m)
            p = (e / denom).astype(BF16)
            o = jnp.dot(p, v_all[:, gl].astype(BF16), preferred_element_type=F32)
            gate = _silu(ga3_ref[tr, hs, :].reshape(T * GROUP_A, HD_A))
            ya3_ref[tr, hs, :] = (o * gate).reshape(T, GROUP_A, HD_A)

    tpos = lax.broadcasted_iota(jnp.int32, (2 * T, 1), 0) % T
    tposf = tpos.astype(F32)
    lane_t = (lax.broadcasted_iota(jnp.int32, (1, LANES), 1) % T).astype(F32)
    lane_b = lax.broadcasted_iota(jnp.int32, (HD_B, LANES), 1) // T
    upper = lax.broadcasted_iota(jnp.int32, (2 * T, 1), 0) < T
    for pair in range(SAMPLE_BB // 2):
        pr = slice(pair * 2 * T, (pair + 1) * 2 * T)
        for h in range(N_HEADS_B):
            hc = slice(h * HD_B, (h + 1) * HD_B)
            lg = LOG_G[h]
            q8, k8, v8 = qb_ref[pr, hc], kb_ref[pr, hc], vb_ref[pr, hc]
            q8b = q8.astype(BF16)
            intra = jnp.zeros((2 * T, HD_B), F32)
            for mm in range(T):
                k_m = jnp.where(upper, k8[mm:mm + 1, :], k8[T + mm:T + mm + 1, :])
                v_m = jnp.where(upper, v8[mm:mm + 1, :], v8[T + mm:T + mm + 1, :])
                k_m = k_m.astype(BF16).astype(F32)
                v_m = v_m.astype(BF16).astype(F32)
                sc = jnp.sum(q8b.astype(F32) * k_m, axis=-1, keepdims=True)
                dm = jnp.where(tpos >= mm, jnp.exp(lg * jnp.maximum(tposf - mm, 0.0)), 0.0)
                intra = intra + (sc * dm).astype(BF16).astype(F32) * v_m
            qdec = jnp.exp(lg * (tposf + 1.0))
            outs = []
            for e in range(2):
                bb = 2 * pair + e
                r = st_ref[bb, h]
                outs.append(jnp.dot(q8b, r.astype(BF16), preferred_element_type=F32))
                slot = (i % (LANE_BATCH // SAMPLE_BB)) * SAMPLE_BB + bb
                kdec = jnp.exp(lg * (T - 1.0 - lane_t))
                kt = jnp.where(lane_b == slot, kbt_ref[hc, :] * kdec, 0.0).astype(BF16)
                u = jnp.dot(kt, vbg_ref[:, hc].astype(BF16), preferred_element_type=F32)
                nst_ref[bb, h] = math.exp(lg * T) * r + u
            inter = jnp.where(upper, outs[0], outs[1]) * qdec
            o = intra + inter
            mu = jnp.mean(o, axis=-1, keepdims=True)
            d = o - mu
            var = jnp.mean(d * d, axis=-1, keepdims=True)
            yh = d * lax.rsqrt(var + EPS) * gnw_ref[:, hc] * _silu(gb_ref[pr, hc])
            yb_ref[pr, hc] = yh


def _mix_sample(ps, sinks, gn_w, cache_k, cache_v, state, d, prev_out, t_new):
    m = ps.shape[0]
    nb = m // t_new
    L = cache_k.shape[2]
    BB = SAMPLE_BB
    n_alias = len(prev_out)
    n_in = 14
    rows = BB * t_new
    qa = ps[:, SEC_QA * W_A:(SEC_QA + 1) * W_A].reshape(m, N_HEADS_A, HD_A)
    ga = ps[:, SEC_GA * W_A:(SEC_GA + 1) * W_A].reshape(m, N_HEADS_A, HD_A)
    kbt = ps[:, SEC_KB * W_A:(SEC_KB + 1) * W_A].T
    sinkrow = jnp.broadcast_to(
        jnp.tile(sinks.reshape(N_KV_A, 1, GROUP_A), (1, t_new, 1)).reshape(N_KV_A, t_new * GROUP_A, 1),
        (N_KV_A, t_new * GROUP_A, 1))
    sec_spec = lambda s: pl.BlockSpec((rows, W_A), lambda i, s=s: (i, s))
    steps_per_lane_group = LANE_BATCH // BB
    return pl.pallas_call(
        functools.partial(_mix_sample_kernel, t_new=t_new),
        grid=(nb // BB,),
        in_specs=[
            pl.BlockSpec((N_KV_A, t_new * GROUP_A, 1), lambda i: (0, 0, 0)),
            pl.BlockSpec((rows, N_HEADS_A, HD_A), lambda i: (i, 0, 0)),
            pl.BlockSpec((rows, N_HEADS_A, HD_A), lambda i: (i, 0, 0)),
            sec_spec(SEC_QB), sec_spec(SEC_KB), sec_spec(SEC_VB), sec_spec(SEC_GB),
            pl.BlockSpec((rows, COL_TILE), lambda i: (i, KV_TILE)),
            pl.BlockSpec((W_B, LANES), lambda i: (0, i // steps_per_lane_group)),
            pl.BlockSpec((LANES, W_B), lambda i: (i // steps_per_lane_group, SEC_VB)),
            pl.BlockSpec((None, BB, L, LANES), lambda i: (d, i, 0, 0)),
            pl.BlockSpec((None, BB, L, LANES), lambda i: (d, i, 0, 0)),
            pl.BlockSpec((None, BB, N_HEADS_B, HD_B, HD_B), lambda i: (d, i, 0, 0, 0)),
            pl.BlockSpec((1, W_B), lambda i: (0, 0)),
        ] + [pl.BlockSpec(memory_space=pl.ANY)] * n_alias,
        out_specs=[
            pl.BlockSpec((rows, N_HEADS_A, HD_A), lambda i: (i, 0, 0)),
            pl.BlockSpec((rows, W_B), lambda i: (i, 0)),
            pl.BlockSpec((None, BB, L, LANES), lambda i: (d, i, 0, 0)),
            pl.BlockSpec((None, BB, L, LANES), lambda i: (d, i, 0, 0)),
            pl.BlockSpec((None, BB, N_HEADS_B, HD_B, HD_B), lambda i: (d, i, 0, 0, 0)),
        ],
        out_shape=[
            jax.ShapeDtypeStruct((m, N_HEADS_A, HD_A), F32),
            jax.ShapeDtypeStruct((m, W_B), F32),
            jax.ShapeDtypeStruct(cache_k.shape, F32),
            jax.ShapeDtypeStruct(cache_v.shape, F32),
            jax.ShapeDtypeStruct(state.shape, F32),
        ],
        input_output_aliases={n_in + a: 2 + a for a in range(n_alias)},
        compiler_params=pltpu.CompilerParams(
            dimension_semantics=("arbitrary",),
            vmem_limit_bytes=VMEM_LIMIT),
        name="mix_sample",
    )(sinkrow, qa, ga, ps, ps, ps, ps, ps, kbt, ps, cache_k, cache_v, state, gn_w, *prev_out)


def _prep_w_kv(w_in):
    kv_width = N_KV_A * HD_A
    ka = w_in[..., W_A:W_A + kv_width]
    va = w_in[..., W_A + kv_width:W_A + 2 * kv_width]
    dup = lambda a: [a[..., g * HD_A:(g + 1) * HD_A] for g in range(N_KV_A) for _ in range(2)]
    return jnp.concatenate(dup(ka) + dup(va), axis=-1).astype(BF16)


def _rope_tables(pos):
    def tab(hd):
        half = hd // 2
        inv = ROPE_THETA ** (-jnp.arange(half, dtype=F32) / half)
        ang = pos.astype(F32)[:, None] * inv[None, :]
        cos, sin = jnp.cos(ang), jnp.sin(ang)
        reps = LANES // hd
        return (jnp.tile(jnp.concatenate([cos, cos], axis=1), (1, reps)),
                jnp.tile(jnp.concatenate([-sin, sin], axis=1), (1, reps)))
    return tab(HD_A) + tab(HD_B)


def kernel(x_prompt, x_sample, cache_k_win, cache_v_win, state_ret, norm_w, w_in, attn_sinks,
           ret_norm_w, w_out, final_norm_w):
    bp, tp, _ = x_prompt.shape
    bs, ts, _ = x_sample.shape
    assert bp == 1 and tp % RET_CHUNK == 0 and min(WINDOW, tp) == WINDOW
    L = cache_k_win.shape[2]
    depth = w_in.shape[0]

    w_in_b = w_in.astype(BF16)
    w_kv = _prep_w_kv(w_in)
    w_out_b = w_out.astype(BF16)
    tab_p = _rope_tables(jnp.arange(tp, dtype=jnp.int32))
    tab_s = _rope_tables(jnp.tile(PAST_LEN + jnp.arange(ts, dtype=jnp.int32), bs))
    fw = final_norm_w.reshape(1, D_MODEL)

    xp = x_prompt.reshape(tp, D_MODEL)
    xs = x_sample.reshape(bs * ts, D_MODEL)
    ck = cache_k_win.reshape(depth, bs, L, N_KV_A * HD_A)
    cv = cache_v_win.reshape(depth, bs, L, N_KV_A * HD_A)

    kp_l, vp_l, rp_l = [], [], []
    sample_out = ()
    for d in range(depth):
        final = d == depth - 1
        nw = norm_w[d].reshape(1, D_MODEL)
        gnw = ret_norm_w[d].reshape(1, W_B)
        p, tail = _in_proj(xp, nw, w_in_b, w_kv, d, tab_p, BF16, tm=1024)
        y, r_fin = _mix_prompt(p, attn_sinks[d], gnw)
        xp = _out_proj(y, xp, w_out_b[d], fw, final, tm=512)
        kp_l.append(jnp.stack([tail[:, 0:HD_A], tail[:, LANES:LANES + HD_A]], axis=1))
        vp_l.append(jnp.stack([tail[:, 2 * LANES:2 * LANES + HD_A], tail[:, 3 * LANES:3 * LANES + HD_A]], axis=1))
        rp_l.append(r_fin)
        ps, _ = _in_proj(xs, nw, w_in_b, w_kv, d, tab_s, F32, tm=bs * ts)
        ya3, yb, *sample_out = _mix_sample(ps, attn_sinks[d], gnw, ck, cv, state_ret, d, sample_out, ts)
        ys = jnp.concatenate([ya3.reshape(bs * ts, W_A), yb], axis=1).astype(BF16)
        xs = _out_proj(ys, xs, w_out_b[d], fw, final, tm=bs * ts)

    keep = min(WINDOW, tp)
    nk, nv, nst = sample_out
    return (xp.reshape(bp, tp, D_MODEL), xs.reshape(bs, ts, D_MODEL),
            jnp.stack(kp_l).reshape(depth, bp, keep, N_KV_A, HD_A),
            jnp.stack(vp_l).reshape(depth, bp, keep, N_KV_A, HD_A),
            jnp.stack(rp_l).reshape(depth, bp, N_HEADS_B, HD_B, HD_B),
            nk.reshape(depth, bs, L, N_KV_A, HD_A), nv.reshape(depth, bs, L, N_KV_A, HD_A), nst)
```

```python
import functools
import math

import jax
import jax.numpy as jnp
from jax import lax
from jax.experimental import pallas as pl
from jax.experimental.pallas import tpu as pltpu

F32 = jnp.float32
BF16 = jnp.bfloat16

D_MODEL = 2048
DEPTH = 2
PAST_LEN = 8192
WINDOW = 128
HD_A = 64
N_HEADS_A = 16
N_KV_A = 2
GROUP_A = N_HEADS_A // N_KV_A
W_A = N_HEADS_A * HD_A
HD_B = 128
N_HEADS_B = 8
W_B = N_HEADS_B * HD_B
RET_CHUNK = 128
ROPE_THETA = 10000.0
EPS = 1e-6

LANES = 128
COL_TILE = 512
SEC_QA, SEC_GA, SEC_QB, SEC_KB, SEC_VB, SEC_GB = 0, 1, 2, 3, 4, 5
N_SEC_TILES = W_A // COL_TILE
KV_TILE = 6 * N_SEC_TILES
D_PROJ_R = (KV_TILE + 1) * COL_TILE
VMEM_LIMIT = 56 * 1024 * 1024

LOG_G = [math.log1p(-(2.0 ** (-5.0 - h))) for h in range(N_HEADS_B)]


def _silu(g):
    return g * (1.0 / (1.0 + jnp.exp(-g)))


def _rope_a(x, c, s):
    lane = lax.broadcasted_iota(jnp.int32, x.shape, 1)
    partner = jnp.where((lane % HD_A) < HD_A // 2,
                        pltpu.roll(x, LANES - HD_A // 2, 1), pltpu.roll(x, HD_A // 2, 1))
    return x * c + partner * s


def _rope_b(x, c, s):
    return x * c + pltpu.roll(x, HD_B // 2, 1) * s


def _in_proj_kernel(x_ref, nw_ref, wa_ref, wb_ref, wkv_ref, ca_ref, sa_ref, cb_ref, sb_ref,
                    o_ref, tail_ref, h_ref, acc_ref, *, row_chunk):
    j = pl.program_id(1)
    tm = x_ref.shape[0]
    n_chunks = tm // row_chunk
    n_groups = COL_TILE // LANES

    @pl.when(j == 0)
    def _():
        def body(r, carry):
            rows = pl.ds(pl.multiple_of(r * row_chunk, row_chunk), row_chunk)
            x = x_ref[rows, :]
            ms = jnp.mean(x * x, axis=-1, keepdims=True)
            h_ref[rows, :] = (x * lax.rsqrt(ms + EPS) * nw_ref[...]).astype(BF16)
            return carry
        lax.fori_loop(0, n_chunks, body, 0)

    @pl.when(j < KV_TILE)
    def _():
        half = COL_TILE // 2
        acc_ref[:, :half] = jnp.dot(h_ref[...], wa_ref[...], preferred_element_type=F32)
        acc_ref[:, half:] = jnp.dot(h_ref[...], wb_ref[...], preferred_element_type=F32)

    @pl.when(j == KV_TILE)
    def _():
        acc_ref[...] = jnp.dot(h_ref[...], wkv_ref[...], preferred_element_type=F32)

    def epilogue(group_fn):
        def body(r, carry):
            rows = pl.ds(pl.multiple_of(r * row_chunk, row_chunk), row_chunk)
            for g in range(n_groups):
                cols = slice(g * LANES, (g + 1) * LANES)
                o_ref[rows, cols] = group_fn(acc_ref[rows, cols], rows, g).astype(o_ref.dtype)
            return carry
        lax.fori_loop(0, n_chunks, body, 0)

    def plain(x, rows, g):
        return x

    def rope_a(x, rows, g):
        return _rope_a(x, ca_ref[rows, :], sa_ref[rows, :])

    def rope_b(x, rows, g):
        return _rope_b(x, cb_ref[rows, :], sb_ref[rows, :])

    def rope_b_scaled(x, rows, g):
        return rope_b(x, rows, g) * (HD_B ** -0.5)

    def kv_tile(x, rows, g):
        return rope_a(x, rows, g) if g < n_groups // 2 else x

    def sec(s):
        return (j >= s * N_SEC_TILES) & (j < (s + 1) * N_SEC_TILES)

    pl.when(sec(SEC_QA))(lambda: epilogue(rope_a))
    pl.when(sec(SEC_QB))(lambda: epilogue(rope_b))
    pl.when(sec(SEC_KB))(lambda: epilogue(rope_b_scaled))
    pl.when(sec(SEC_GA) | sec(SEC_VB) | sec(SEC_GB))(lambda: epilogue(plain))

    @pl.when(j == KV_TILE)
    def _():
        epilogue(kv_tile)
        rows = slice(tm - WINDOW, tm)
        for g in range(n_groups):
            cols = slice(g * LANES, (g + 1) * LANES)
            x = acc_ref[rows, cols]
            if g < n_groups // 2:
                x = _rope_a(x, ca_ref[rows, :], sa_ref[rows, :])
            tail_ref[:, cols] = x


def _in_proj(x, norm_w, w_all, wkv_all, d, tables, out_dtype, tm):
    m = x.shape[0]
    n_tiles = D_PROJ_R // COL_TILE
    half = COL_TILE // 2
    n_src_blocks = w_all.shape[2] // half
    tab_spec = pl.BlockSpec((tm, LANES), lambda i, j: (i, 0))

    def src_block(j, e):
        sec = j // N_SEC_TILES
        blk = sec * (W_A // half) + jnp.minimum(sec, 1) + (j % N_SEC_TILES) * 2 + e
        return jnp.minimum(blk, n_src_blocks - 1)

    return pl.pallas_call(
        functools.partial(_in_proj_kernel, row_chunk=min(tm, 256)),
        grid=(m // tm, n_tiles),
        in_specs=[
            pl.BlockSpec((tm, D_MODEL), lambda i, j: (i, 0)),
            pl.BlockSpec((1, D_MODEL), lambda i, j: (0, 0)),
            pl.BlockSpec((None, D_MODEL, half), lambda i, j: (d, 0, src_block(j, 0))),
            pl.BlockSpec((None, D_MODEL, half), lambda i, j: (d, 0, src_block(j, 1))),
            pl.BlockSpec((None, D_MODEL, COL_TILE), lambda i, j: (d, 0, 0)),
            tab_spec, tab_spec, tab_spec, tab_spec,
        ],
        out_specs=[
            pl.BlockSpec((tm, COL_TILE), lambda i, j: (i, j)),
            pl.BlockSpec((WINDOW, COL_TILE), lambda i, j: (0, 0)),
        ],
        out_shape=[
            jax.ShapeDtypeStruct((m, D_PROJ_R), out_dtype),
            jax.ShapeDtypeStruct((WINDOW, COL_TILE), F32),
        ],
        scratch_shapes=[
            pltpu.VMEM((tm, D_MODEL), BF16),
            pltpu.VMEM((tm, COL_TILE), F32),
        ],
        compiler_params=pltpu.CompilerParams(
            dimension_semantics=("arbitrary", "arbitrary"),
            vmem_limit_bytes=VMEM_LIMIT),
        name="in_proj",
    )(x, norm_w, w_all, w_all, wkv_all, *tables)


def _out_proj_kernel(y_ref, x_ref, w_ref, fw_ref, o_ref, *, final):
    acc = x_ref[...] + jnp.dot(y_ref[...], w_ref[...], preferred_element_type=F32)
    if final:
        ms = jnp.mean(acc * acc, axis=-1, keepdims=True)
        acc = acc * lax.rsqrt(ms + EPS) * fw_ref[...]
    o_ref[...] = acc


def _out_proj(y, x, w, final_w, final, tm):
    m = x.shape[0]
    return pl.pallas_call(
        functools.partial(_out_proj_kernel, final=final),
        grid=(m // tm,),
        in_specs=[
            pl.BlockSpec((tm, D_MODEL), lambda i: (i, 0)),
            pl.BlockSpec((tm, D_MODEL), lambda i: (i, 0)),
            pl.BlockSpec((D_MODEL, D_MODEL), lambda i: (0, 0)),
            pl.BlockSpec((1, D_MODEL), lambda i: (0, 0)),
        ],
        out_specs=pl.BlockSpec((tm, D_MODEL), lambda i: (i, 0)),
        out_shape=jax.ShapeDtypeStruct((m, D_MODEL), F32),
        compiler_params=pltpu.CompilerParams(
            dimension_semantics=("arbitrary",),
            vmem_limit_bytes=VMEM_LIMIT),
        name="out_proj",
    )(y, x, w, final_w)


def _nt_dot(a, b):
    return lax.dot_general(a, b, (((1,), (1,)), ((), ())), preferred_element_type=F32)


def _tn_dot(a, b):
    return lax.dot_general(a, b, (((0,), (0,)), ((), ())), preferred_element_type=F32)


def _mix_prompt_kernel(sink_ref, qa_ref, ga_ref, qb_ref, kb_ref, vb_ref, gb_ref,
                       kvc_ref, kvp_ref, gnw_ref, y_ref, r_ref,
                       dmask_ref, qdec_ref, kdec_ref,
                       qs_ref, s_ref, p_ref, o_ref, scb_ref, inter_ref):
    c = pl.program_id(0)
    C = RET_CHUNK
    row = lax.broadcasted_iota(jnp.int32, (C, LANES), 0)
    lane = lax.broadcasted_iota(jnp.int32, (C, LANES), 1)

    @pl.when(c == 0)
    def _():
        r_ref[...] = jnp.zeros(r_ref.shape, F32)
        rowf = row.astype(F32)
        diff = rowf - lane.astype(F32)
        for h in range(N_HEADS_B):
            dmask_ref[h] = jnp.where(diff >= 0, jnp.exp(LOG_G[h] * jnp.maximum(diff, 0.0)), 0.0)
            qdec_ref[h] = jnp.exp(LOG_G[h] * (rowf + 1.0))
            kdec_ref[h] = jnp.exp(LOG_G[h] * (C - 1.0 - rowf))

    tri = lane <= row
    lo = lane < HD_A
    has_prev = c > 0
    neg_inf = jnp.float32(-jnp.inf)
    half_group = GROUP_A // 2
    kcol = lambda g: slice(g * LANES, (g + 1) * LANES)
    vcol = lambda g: slice((N_KV_A + g) * LANES, (N_KV_A + g + 1) * LANES)
    pair_col = lambda g, pi: slice((g * half_group + pi) * LANES, (g * half_group + pi + 1) * LANES)
    head_rows = lambda hh: slice(hh * C, (hh + 1) * C)

    for g in range(N_KV_A):
        for pi in range(half_group):
            q_pair = qa_ref[:, pair_col(g, pi)].astype(F32)
            qs_ref[g, head_rows(2 * pi), :] = jnp.where(lo, q_pair, 0.0).astype(BF16)
            qs_ref[g, head_rows(2 * pi + 1), :] = jnp.where(lo, 0.0, q_pair).astype(BF16)
        qs = qs_ref[g]
        s_ref[g, 0] = _nt_dot(qs, kvc_ref[:, kcol(g)])
        s_ref[g, 1] = _nt_dot(qs, kvp_ref[:, kcol(g)])

    for h in range(N_HEADS_B):
        hc = slice(h * HD_B, (h + 1) * HD_B)
        q, k, v = qb_ref[:, hc], kb_ref[:, hc], vb_ref[:, hc]
        scb_ref[h] = (_nt_dot(q, k) * dmask_ref[h]).astype(BF16)
        r = r_ref[h]
        inter_ref[h] = jnp.dot(q, r.astype(BF16), preferred_element_type=F32) * qdec_ref[h]
        kd = (k.astype(F32) * kdec_ref[h]).astype(BF16)
        r_ref[h] = math.exp(LOG_G[h] * C) * r + _tn_dot(kd, v)

    for g in range(N_KV_A):
        for hh in range(GROUP_A):
            rows = head_rows(hh)
            s_prev = jnp.where(has_prev, s_ref[g, 1, rows, :], neg_inf)
            s = jnp.where(tri, s_ref[g, 0, rows, :], s_prev) * (HD_A ** -0.5)
            sink = sink_ref[g * GROUP_A + hh]
            m = jnp.maximum(jnp.max(s, axis=-1, keepdims=True), sink)
            e = jnp.exp(s - m)
            denom = jnp.sum(e, axis=-1, keepdims=True) + jnp.exp(sink - m)
            p = e / denom
            p_ref[g, 0, rows, :] = jnp.where(tri, p, 0.0).astype(BF16)
            p_ref[g, 1, rows, :] = jnp.where(tri, 0.0, p).astype(BF16)

    for g in range(N_KV_A):
        o_ref[g] = (jnp.dot(p_ref[g, 0], kvc_ref[:, vcol(g)], preferred_element_type=F32)
                    + jnp.dot(p_ref[g, 1], kvp_ref[:, vcol(g)], preferred_element_type=F32))

    for h in range(N_HEADS_B):
        hc = slice(h * HD_B, (h + 1) * HD_B)
        o = jnp.dot(scb_ref[h], vb_ref[:, hc], preferred_element_type=F32) + inter_ref[h]
        mu = jnp.mean(o, axis=-1, keepdims=True)
        d = o - mu
        var = jnp.mean(d * d, axis=-1, keepdims=True)
        yh = d * lax.rsqrt(var + EPS) * gnw_ref[:, hc] * _silu(gb_ref[:, hc].astype(F32))
        y_ref[:, W_A + h * HD_B: W_A + (h + 1) * HD_B] = yh.astype(y_ref.dtype)

    for g in range(N_KV_A):
        for pi in range(half_group):
            pc = pair_col(g, pi)
            o_pair = jnp.where(lo, o_ref[g, head_rows(2 * pi), :], o_ref[g, head_rows(2 * pi + 1), :])
            y_ref[:, pc] = (o_pair * _silu(ga_ref[:, pc].astype(F32))).astype(y_ref.dtype)


def _mix_prompt(p, sinks, gn_w):
    t = p.shape[0]
    C = RET_CHUNK
    sec_spec = lambda s: pl.BlockSpec((C, W_A), lambda c, s=s: (c, s))
    return pl.pallas_call(
        _mix_prompt_kernel,
        grid=(t // C,),
        in_specs=[
            pl.BlockSpec(memory_space=pltpu.SMEM),
            sec_spec(SEC_QA), sec_spec(SEC_GA), sec_spec(SEC_QB),
            sec_spec(SEC_KB), sec_spec(SEC_VB), sec_spec(SEC_GB),
            pl.BlockSpec((C, COL_TILE), lambda c: (c, KV_TILE)),
            pl.BlockSpec((C, COL_TILE), lambda c: (jnp.maximum(c - 1, 0), KV_TILE)),
            pl.BlockSpec((1, W_B), lambda c: (0, 0)),
        ],
        out_specs=[
            pl.BlockSpec((C, D_MODEL), lambda c: (c, 0)),
            pl.BlockSpec((N_HEADS_B, HD_B, HD_B), lambda c: (0, 0, 0)),
        ],
        out_shape=[
            jax.ShapeDtypeStruct((t, D_MODEL), BF16),
            jax.ShapeDtypeStruct((N_HEADS_B, HD_B, HD_B), F32),
        ],
        scratch_shapes=[pltpu.VMEM((N_HEADS_B, C, LANES), F32)] * 3 + [
            pltpu.VMEM((N_KV_A, GROUP_A * C, LANES), BF16),
            pltpu.VMEM((N_KV_A, 2, GROUP_A * C, LANES), F32),
            pltpu.VMEM((N_KV_A, 2, GROUP_A * C, LANES), BF16),
            pltpu.VMEM((N_KV_A, GROUP_A * C, LANES), F32),
            pltpu.VMEM((N_HEADS_B, C, HD_B), BF16),
            pltpu.VMEM((N_HEADS_B, C, HD_B), F32),
        ],
        compiler_params=pltpu.CompilerParams(
            dimension_semantics=("arbitrary",),
            vmem_limit_bytes=VMEM_LIMIT),
        name="mix_prompt",
    )(sinks, p, p, p, p, p, p, p, p, gn_w)


SAMPLE_BB = 8
LANE_BATCH = LANES // 4


def _mix_sample_kernel(sinkrow_ref, q3_ref, ga3_ref, qb_ref, kb_ref, vb_ref, gb_ref, kvn_ref,
                       kbt_ref, vbg_ref, ck_ref, cv_ref, st_ref, gnw_ref, *rest, t_new):
    ya3_ref, yb_ref, nk_ref, nv_ref, nst_ref, s_ref, p_ref, vall_ref, kt_ref, x_ref, sc_ref = rest[-11:]
    i = pl.program_id(0)
    T = t_new
    BB = SAMPLE_BB
    L = ck_ref.shape[1]
    n_keys = L + T
    QR = T * GROUP_A
    R = BB * T
    lo = lax.broadcasted_iota(jnp.int32, (T, LANES), 1) < HD_A
    head_sl = lambda g: slice(g * GROUP_A, (g + 1) * GROUP_A)
    lane_sl = lambda g: slice(g * HD_A, (g + 1) * HD_A)
    tok_rows = lambda bb: slice(bb * T, (bb + 1) * T)


    for bb in range(BB):
        kvn = kvn_ref[tok_rows(bb), :]
        k_new = jnp.where(lo, kvn[:, 0:LANES], kvn[:, LANES:2 * LANES])
        v_new = jnp.where(lo, kvn[:, 2 * LANES:3 * LANES], kvn[:, 3 * LANES:4 * LANES])
        k_all = jnp.concatenate([ck_ref[bb], k_new], axis=0)
        v_all = jnp.concatenate([cv_ref[bb], v_new], axis=0)
        nk_ref[bb] = k_all[T:, :]
        nv_ref[bb] = v_all[T:, :]
        vall_ref[bb] = v_all.astype(BF16)
        k_all_b = k_all.astype(BF16)
        for g in range(N_KV_A):
            q = q3_ref[tok_rows(bb), head_sl(g), :].reshape(T * GROUP_A, HD_A).astype(BF16)
            s_ref[g, bb * QR:(bb + 1) * QR, :] = _nt_dot(q, k_all_b[:, lane_sl(g)])

    lane_t = (lax.broadcasted_iota(jnp.int32, (1, LANES), 1) % T).astype(F32)
    lane_b = lax.broadcasted_iota(jnp.int32, (HD_B, LANES), 1) // T
    slot0 = (i % (LANE_BATCH // BB)) * BB
    for h in range(N_HEADS_B):
        hc = slice(h * HD_B, (h + 1) * HD_B)
        lg = LOG_G[h]
        kdec = jnp.exp(lg * (T - 1.0 - lane_t))
        kt_dec = kbt_ref[hc, :] * kdec
        for bb in range(BB):
            kt_ref[h, bb * HD_B:(bb + 1) * HD_B, :] = jnp.where(lane_b == slot0 + bb, kt_dec, 0.0).astype(BF16)
        u = jnp.dot(kt_ref[h], vbg_ref[:, hc].astype(BF16), preferred_element_type=F32)
        for bb in range(BB):
            r = st_ref[bb, h]
            pr = slice((bb // 2) * 2 * T, (bb // 2 + 1) * 2 * T)
            x_ref[bb * N_HEADS_B + h] = jnp.dot(qb_ref[pr, hc].astype(BF16), r.astype(BF16),
                                                 preferred_element_type=F32)
            nst_ref[bb, h] = math.exp(lg * T) * r + u[bb * HD_B:(bb + 1) * HD_B, :]

    ri = lax.broadcasted_iota(jnp.int32, (R, R), 0)
    ci = lax.broadcasted_iota(jnp.int32, (R, R), 1)
    same_b = (ri // T) == (ci // T)
    dt = (ri % T - ci % T).astype(F32)
    keep = same_b & (dt >= 0)
    for h in range(N_HEADS_B):
        hc = slice(h * HD_B, (h + 1) * HD_B)
        dmask = jnp.where(keep, jnp.exp(LOG_G[h] * jnp.maximum(dt, 0.0)), 0.0)
        sc = _nt_dot(qb_ref[:, hc].astype(BF16), kb_ref[:, hc].astype(BF16)) * dmask
        sc_ref[h] = sc.astype(BF16)

    rows_q = (lax.broadcasted_iota(jnp.int32, (BB * QR, n_keys), 0) % QR) // GROUP_A
    key = lax.broadcasted_iota(jnp.int32, (BB * QR, n_keys), 1)
    delta = jnp.where(key < L, rows_q + L - key, rows_q - (key - L))
    valid = (delta >= 0) & (delta < WINDOW)
    for g in range(N_KV_A):
        s = jnp.where(valid, s_ref[g] * (HD_A ** -0.5), -jnp.inf)
        sink = sinkrow_ref[g]
        m = jnp.maximum(jnp.max(s, axis=-1, keepdims=True), sink)
        e = jnp.exp(s - m)
        denom = jnp.sum(e, axis=-1, keepdims=True) + jnp.exp(sink - m)
        p_ref[g] = (e / denom).astype(BF16)

    tposf = (lax.broadcasted_iota(jnp.int32, (R, 1), 0) % T).astype(F32)
    upper = (lax.broadcasted_iota(jnp.int32, (2 * T, 1), 0) < T)
    for h in range(N_HEADS_B):
        hc = slice(h * HD_B, (h + 1) * HD_B)
        intra = jnp.dot(sc_ref[h], vb_ref[:, hc].astype(BF16), preferred_element_type=F32)
        inter = jnp.concatenate(
            [jnp.where(upper, x_ref[(2 * pair) * N_HEADS_B + h], x_ref[(2 * pair + 1) * N_HEADS_B + h])
             for pair in range(BB // 2)], axis=0)
        o = intra + inter * jnp.exp(LOG_G[h] * (tposf + 1.0))
        mu = jnp.mean(o, axis=-1, keepdims=True)
        d = o - mu
        var = jnp.mean(d * d, axis=-1, keepdims=True)
        yb_ref[:, hc] = d * lax.rsqrt(var + EPS) * gnw_ref[:, hc] * _silu(gb_ref[:, hc])

    for bb in range(BB):
        for g in range(N_KV_A):
            o = jnp.dot(p_ref[g, bb * QR:(bb + 1) * QR, :], vall_ref[bb, :, lane_sl(g)],
                        preferred_element_type=F32)
            gate = _silu(ga3_ref[tok_rows(bb), head_sl(g), :].reshape(QR, HD_A))
            ya3_ref[tok_rows(bb), head_sl(g), :] = (o * gate).reshape(T, GROUP_A, HD_A)


def _mix_sample(ps, sinks, gn_w, cache_k, cache_v, state, d, prev_out, t_new):
    m = ps.shape[0]
    nb = m // t_new
    L = cache_k.shape[2]
    BB = SAMPLE_BB
    n_alias = len(prev_out)
    n_in = 14
    rows = BB * t_new
    qa = ps[:, SEC_QA * W_A:(SEC_QA + 1) * W_A].reshape(m, N_HEADS_A, HD_A)
    ga = ps[:, SEC_GA * W_A:(SEC_GA + 1) * W_A].reshape(m, N_HEADS_A, HD_A)
    kbt = ps[:, SEC_KB * W_A:(SEC_KB + 1) * W_A].T
    sinkrow = jnp.tile(sinks.reshape(N_KV_A, 1, GROUP_A), (1, BB * t_new, 1)).reshape(N_KV_A, -1, 1)
    sec_spec = lambda s: pl.BlockSpec((rows, W_A), lambda i, s=s: (i, s))
    steps_per_lane_group = LANE_BATCH // BB
    return pl.pallas_call(
        functools.partial(_mix_sample_kernel, t_new=t_new),
        grid=(nb // BB,),
        in_specs=[
            pl.BlockSpec((N_KV_A, BB * t_new * GROUP_A, 1), lambda i: (0, 0, 0)),
            pl.BlockSpec((rows, N_HEADS_A, HD_A), lambda i: (i, 0, 0)),
            pl.BlockSpec((rows, N_HEADS_A, HD_A), lambda i: (i, 0, 0)),
            sec_spec(SEC_QB), sec_spec(SEC_KB), sec_spec(SEC_VB), sec_spec(SEC_GB),
            pl.BlockSpec((rows, COL_TILE), lambda i: (i, KV_TILE)),
            pl.BlockSpec((W_B, LANES), lambda i: (0, i // steps_per_lane_group)),
            pl.BlockSpec((LANES, W_B), lambda i: (i // steps_per_lane_group, SEC_VB)),
            pl.BlockSpec((None, BB, L, LANES), lambda i: (d, i, 0, 0)),
            pl.BlockSpec((None, BB, L, LANES), lambda i: (d, i, 0, 0)),
            pl.BlockSpec((None, BB, N_HEADS_B, HD_B, HD_B), lambda i: (d, i, 0, 0, 0)),
            pl.BlockSpec((1, W_B), lambda i: (0, 0)),
        ] + [pl.BlockSpec(memory_space=pl.ANY)] * n_alias,
        out_specs=[
            pl.BlockSpec((rows, N_HEADS_A, HD_A), lambda i: (i, 0, 0)),
            pl.BlockSpec((rows, W_B), lambda i: (i, 0)),
            pl.BlockSpec((None, BB, L, LANES), lambda i: (d, i, 0, 0)),
            pl.BlockSpec((None, BB, L, LANES), lambda i: (d, i, 0, 0)),
            pl.BlockSpec((None, BB, N_HEADS_B, HD_B, HD_B), lambda i: (d, i, 0, 0, 0)),
        ],
        out_shape=[
            jax.ShapeDtypeStruct((m, N_HEADS_A, HD_A), F32),
            jax.ShapeDtypeStruct((m, W_B), F32),
            jax.ShapeDtypeStruct(cache_k.shape, F32),
            jax.ShapeDtypeStruct(cache_v.shape, F32),
            jax.ShapeDtypeStruct(state.shape, F32),
        ],
        input_output_aliases={n_in + a: 2 + a for a in range(n_alias)},
        scratch_shapes=[
            pltpu.VMEM((N_KV_A, BB * t_new * GROUP_A, L + t_new), F32),
            pltpu.VMEM((N_KV_A, BB * t_new * GROUP_A, L + t_new), BF16),
            pltpu.VMEM((BB, L + t_new, LANES), BF16),
            pltpu.VMEM((N_HEADS_B, BB * HD_B, LANES), BF16),
            pltpu.VMEM((BB * N_HEADS_B, 2 * t_new, HD_B), F32),
            pltpu.VMEM((N_HEADS_B, rows, rows), BF16),
        ],
        compiler_params=pltpu.CompilerParams(
            dimension_semantics=("arbitrary",),
            vmem_limit_bytes=VMEM_LIMIT),
        name="mix_sample",
    )(sinkrow, qa, ga, ps, ps, ps, ps, ps, kbt, ps, cache_k, cache_v, state, gn_w, *prev_out)


def _prep_w_kv(w_in):
    kv_width = N_KV_A * HD_A
    ka = w_in[..., W_A:W_A + kv_width]
    va = w_in[..., W_A + kv_width:W_A + 2 * kv_width]
    dup = lambda a: [a[..., g * HD_A:(g + 1) * HD_A] for g in range(N_KV_A) for _ in range(2)]
    return jnp.concatenate(dup(ka) + dup(va), axis=-1).astype(BF16)


def _rope_tables(pos):
    def tab(hd):
        half = hd // 2
        inv = ROPE_THETA ** (-jnp.arange(half, dtype=F32) / half)
        ang = pos.astype(F32)[:, None] * inv[None, :]
        cos, sin = jnp.cos(ang), jnp.sin(ang)
        reps = LANES // hd
        return (jnp.tile(jnp.concatenate([cos, cos], axis=1), (1, reps)),
                jnp.tile(jnp.concatenate([-sin, sin], axis=1), (1, reps)))
    return tab(HD_A) + tab(HD_B)


def kernel(x_prompt, x_sample, cache_k_win, cache_v_win, state_ret, norm_w, w_in, attn_sinks,
           ret_norm_w, w_out, final_norm_w):
    bp, tp, _ = x_prompt.shape
    bs, ts, _ = x_sample.shape
    assert bp == 1 and tp % RET_CHUNK == 0 and min(WINDOW, tp) == WINDOW
    L = cache_k_win.shape[2]
    depth = w_in.shape[0]

    w_in_b = w_in.astype(BF16)
    w_kv = _prep_w_kv(w_in)
    w_out_b = w_out.astype(BF16)
    tab_p = _rope_tables(jnp.arange(tp, dtype=jnp.int32))
    tab_s = _rope_tables(jnp.tile(PAST_LEN + jnp.arange(ts, dtype=jnp.int32), bs))
    fw = final_norm_w.reshape(1, D_MODEL)

    xp = x_prompt.reshape(tp, D_MODEL)
    xs = x_sample.reshape(bs * ts, D_MODEL)
    ck = cache_k_win.reshape(depth, bs, L, N_KV_A * HD_A)
    cv = cache_v_win.reshape(depth, bs, L, N_KV_A * HD_A)

    kp_l, vp_l, rp_l = [], [], []
    sample_out = ()
    for d in range(depth):
        final = d == depth - 1
        nw = norm_w[d].reshape(1, D_MODEL)
        gnw = ret_norm_w[d].reshape(1, W_B)
        p, tail = _in_proj(xp, nw, w_in_b, w_kv, d, tab_p, BF16, tm=1024)
        y, r_fin = _mix_prompt(p, attn_sinks[d], gnw)
        xp = _out_proj(y, xp, w_out_b[d], fw, final, tm=512)
        kp_l.append(jnp.stack([tail[:, 0:HD_A], tail[:, LANES:LANES + HD_A]], axis=1))
        vp_l.append(jnp.stack([tail[:, 2 * LANES:2 * LANES + HD_A], tail[:, 3 * LANES:3 * LANES + HD_A]], axis=1))
        rp_l.append(r_fin)
        ps, _ = _in_proj(xs, nw, w_in_b, w_kv, d, tab_s, F32, tm=bs * ts)
        ya3, yb, *sample_out = _mix_sample(ps, attn_sinks[d], gnw, ck, cv, state_ret, d, sample_out, ts)
        ys = jnp.concatenate([ya3.reshape(bs * ts, W_A), yb], axis=1).astype(BF16)
        xs = _out_proj(ys, xs, w_out_b[d], fw, final, tm=bs * ts)

    keep = min(WINDOW, tp)
    nk, nv, nst = sample_out
    return (xp.reshape(bp, tp, D_MODEL), xs.reshape(bs, ts, D_MODEL),
            jnp.stack(kp_l).reshape(depth, bp, keep, N_KV_A, HD_A),
            jnp.stack(vp_l).reshape(depth, bp, keep, N_KV_A, HD_A),
            jnp.stack(rp_l).reshape(depth, bp, N_HEADS_B, HD_B, HD_B),
            nk.reshape(depth, bs, L, N_KV_A, HD_A), nv.reshape(depth, bs, L, N_KV_A, HD_A), nst)
```

```python
import functools
import math

import jax
import jax.numpy as jnp
from jax import lax
from jax.experimental import pallas as pl
from jax.experimental.pallas import tpu as pltpu

F32 = jnp.float32
BF16 = jnp.bfloat16

D_MODEL = 2048
DEPTH = 2
PAST_LEN = 8192
WINDOW = 128
HD_A = 64
N_HEADS_A = 16
N_KV_A = 2
GROUP_A = N_HEADS_A // N_KV_A
W_A = N_HEADS_A * HD_A
HD_B = 128
N_HEADS_B = 8
W_B = N_HEADS_B * HD_B
RET_CHUNK = 128
ROPE_THETA = 10000.0
EPS = 1e-6

LANES = 128
COL_TILE = 512
SEC_QA, SEC_GA, SEC_QB, SEC_KB, SEC_VB, SEC_GB = 0, 1, 2, 3, 4, 5
N_SEC_TILES = W_A // COL_TILE
KV_TILE = 6 * N_SEC_TILES
D_PROJ_R = (KV_TILE + 1) * COL_TILE
VMEM_LIMIT = 56 * 1024 * 1024

LOG_G = [math.log1p(-(2.0 ** (-5.0 - h))) for h in range(N_HEADS_B)]


def _silu(g):
    return g * (1.0 / (1.0 + jnp.exp(-g)))


EPI_ROPE_A, EPI_ROPE_B, EPI_ROPE_B_SCALED, EPI_PLAIN = 0, 1, 2, 3
N_EPI = 4
_SEC_EPI = (EPI_ROPE_A, EPI_PLAIN, EPI_ROPE_B, EPI_ROPE_B_SCALED, EPI_PLAIN, EPI_PLAIN)


def _tile_epi(tile, hi):
    sec = tile // N_SEC_TILES
    t = jnp.int32(EPI_PLAIN if hi else EPI_ROPE_A)
    for s, e in enumerate(_SEC_EPI):
        t = jnp.where(sec == s, e, t)
    return t


def _in_proj_kernel(x_ref, nw_ref, wa_ref, wb_ref, wkv_ref, c0_ref, s0_ref, c1_ref, s1_ref,
                    o_ref, tail_ref, h_ref, acc_ref, *, row_chunk):
    i = pl.program_id(0)
    j = pl.program_id(1)
    tm = x_ref.shape[0]
    n_chunks = tm // row_chunk
    n_groups = COL_TILE // LANES
    half = COL_TILE // 2
    slot = j % 2
    prev_slot = 1 - slot
    prev_tile = j - 1

    @pl.when(j == 0)
    def _():
        def body(r, carry):
            rows = pl.ds(pl.multiple_of(r * row_chunk, row_chunk), row_chunk)
            x = x_ref[rows, :]
            ms = jnp.mean(x * x, axis=-1, keepdims=True)
            h_ref[rows, :] = (x * lax.rsqrt(ms + EPS) * nw_ref[...]).astype(BF16)
            return carry
        lax.fori_loop(0, n_chunks, body, 0)

        @pl.when(i == 0)
        def _():
            acc_ref[1] = jnp.zeros(acc_ref.shape[1:], F32)

    def epilogue():
        lane = lax.broadcasted_iota(jnp.int32, (row_chunk, LANES), 1)
        first_half = (lane % HD_A) < HD_A // 2
        for hi in range(2):
            c_ref, s_ref = (c0_ref, s0_ref) if hi == 0 else (c1_ref, s1_ref)
            is_a = _tile_epi(prev_tile, hi) == EPI_ROPE_A
            shift_up = jnp.where(is_a, LANES - HD_A // 2, HD_B // 2)
            shift_dn = jnp.where(is_a, HD_A // 2, HD_B // 2)
            for r in range(n_chunks):
                rows = slice(r * row_chunk, (r + 1) * row_chunk)
                c, s = c_ref[rows, :], s_ref[rows, :]
                for g in range(hi * n_groups // 2, (hi + 1) * n_groups // 2):
                    cols = slice(g * LANES, (g + 1) * LANES)
                    x = acc_ref[prev_slot, rows, cols]
                    partner = jnp.where(first_half, pltpu.roll(x, shift_up, 1), pltpu.roll(x, shift_dn, 1))
                    out = x * c + partner * s
                    o_ref[rows, cols] = out.astype(o_ref.dtype)
                    if r == n_chunks - 1:
                        tail_ref[:, cols] = out[row_chunk - WINDOW:, :]

    @pl.when(j < KV_TILE)
    def _():
        epilogue()
        acc_ref[slot, :, :half] = jnp.dot(h_ref[...], wa_ref[...], preferred_element_type=F32)
        acc_ref[slot, :, half:] = jnp.dot(h_ref[...], wb_ref[...], preferred_element_type=F32)

    @pl.when(j == KV_TILE)
    def _():
        epilogue()
        acc_ref[slot] = jnp.dot(h_ref[...], wkv_ref[...], preferred_element_type=F32)

    @pl.when(j == KV_TILE + 1)
    def _():
        epilogue()


def _in_proj(x, norm_w, w_all, wkv_all, d, tables, out_dtype, tm):
    m = x.shape[0]
    n_tiles = D_PROJ_R // COL_TILE
    half = COL_TILE // 2
    n_src_blocks = w_all.shape[2] // half
    c_tab, s_tab = tables

    def src_block(j, e):
        sec = j // N_SEC_TILES
        blk = sec * (W_A // half) + jnp.minimum(sec, 1) + (j % N_SEC_TILES) * 2 + e
        return jnp.minimum(blk, n_src_blocks - 1)

    def tab_spec(hi):
        return pl.BlockSpec((None, tm, LANES), lambda i, j: (_tile_epi(jnp.maximum(j - 1, 0), hi), i, 0))

    return pl.pallas_call(
        functools.partial(_in_proj_kernel, row_chunk=min(tm, 256)),
        grid=(m // tm, n_tiles + 1),
        in_specs=[
            pl.BlockSpec((tm, D_MODEL), lambda i, j: (i, 0)),
            pl.BlockSpec((1, D_MODEL), lambda i, j: (0, 0)),
            pl.BlockSpec((None, D_MODEL, half), lambda i, j: (d, 0, src_block(j, 0))),
            pl.BlockSpec((None, D_MODEL, half), lambda i, j: (d, 0, src_block(j, 1))),
            pl.BlockSpec((None, D_MODEL, COL_TILE), lambda i, j: (d, 0, 0)),
            tab_spec(0), tab_spec(0), tab_spec(1), tab_spec(1),
        ],
        out_specs=[
            pl.BlockSpec((tm, COL_TILE), lambda i, j: (i, jnp.maximum(j - 1, 0))),
            pl.BlockSpec((WINDOW, COL_TILE), lambda i, j: (0, 0)),
        ],
        out_shape=[
            jax.ShapeDtypeStruct((m, D_PROJ_R), out_dtype),
            jax.ShapeDtypeStruct((WINDOW, COL_TILE), F32),
        ],
        scratch_shapes=[
            pltpu.VMEM((tm, D_MODEL), BF16),
            pltpu.VMEM((2, tm, COL_TILE), F32),
        ],
        compiler_params=pltpu.CompilerParams(
            dimension_semantics=("arbitrary", "arbitrary"),
            vmem_limit_bytes=VMEM_LIMIT),
        name="in_proj",
    )(x, norm_w, w_all, w_all, wkv_all, c_tab, s_tab, c_tab, s_tab)


def _out_proj_kernel(y_ref, x_ref, w_ref, fw_ref, o_ref, *, final):
    acc = x_ref[...] + jnp.dot(y_ref[...], w_ref[...], preferred_element_type=F32)
    if final:
        ms = jnp.mean(acc * acc, axis=-1, keepdims=True)
        acc = acc * lax.rsqrt(ms + EPS) * fw_ref[...]
    o_ref[...] = acc


def _out_proj(y, x, w, final_w, final, tm):
    m = x.shape[0]
    return pl.pallas_call(
        functools.partial(_out_proj_kernel, final=final),
        grid=(m // tm,),
        in_specs=[
            pl.BlockSpec((tm, D_MODEL), lambda i: (i, 0)),
            pl.BlockSpec((tm, D_MODEL), lambda i: (i, 0)),
            pl.BlockSpec((D_MODEL, D_MODEL), lambda i: (0, 0)),
            pl.BlockSpec((1, D_MODEL), lambda i: (0, 0)),
        ],
        out_specs=pl.BlockSpec((tm, D_MODEL), lambda i: (i, 0)),
        out_shape=jax.ShapeDtypeStruct((m, D_MODEL), F32),
        compiler_params=pltpu.CompilerParams(
            dimension_semantics=("arbitrary",),
            vmem_limit_bytes=VMEM_LIMIT),
        name="out_proj",
    )(y, x, w, final_w)


def _nt_dot(a, b):
    return lax.dot_general(a, b, (((1,), (1,)), ((), ())), preferred_element_type=F32)


def _tn_dot(a, b):
    return lax.dot_general(a, b, (((0,), (0,)), ((), ())), preferred_element_type=F32)


def _mix_prompt_kernel(sink_ref, qa_ref, ga_ref, qb_ref, kb_ref, vb_ref, gb_ref,
                       kvc_ref, kvp_ref, gnw_ref, y_ref, r_ref,
                       dmask_ref, qdec_ref, kdec_ref,
                       qs_ref, s_ref, p_ref, o_ref, scb_ref, inter_ref):
    c = pl.program_id(0)
    C = RET_CHUNK
    row = lax.broadcasted_iota(jnp.int32, (C, LANES), 0)
    lane = lax.broadcasted_iota(jnp.int32, (C, LANES), 1)

    @pl.when(c == 0)
    def _():
        r_ref[...] = jnp.zeros(r_ref.shape, F32)
        rowf = row.astype(F32)
        diff = rowf - lane.astype(F32)
        for h in range(N_HEADS_B):
            dmask_ref[h] = jnp.where(diff >= 0, jnp.exp(LOG_G[h] * jnp.maximum(diff, 0.0)), 0.0)
            qdec_ref[h] = jnp.exp(LOG_G[h] * (rowf + 1.0))
            kdec_ref[h] = jnp.exp(LOG_G[h] * (C - 1.0 - rowf))

    tri = lane <= row
    lo = lane < HD_A
    has_prev = c > 0
    neg_inf = jnp.float32(-jnp.inf)
    half_group = GROUP_A // 2
    kcol = lambda g: slice(g * LANES, (g + 1) * LANES)
    vcol = lambda g: slice((N_KV_A + g) * LANES, (N_KV_A + g + 1) * LANES)
    pair_col = lambda g, pi: slice((g * half_group + pi) * LANES, (g * half_group + pi + 1) * LANES)
    head_rows = lambda hh: slice(hh * C, (hh + 1) * C)

    for g in range(N_KV_A):
        for pi in range(half_group):
            q_pair = qa_ref[:, pair_col(g, pi)].astype(F32)
            qs_ref[g, head_rows(2 * pi), :] = jnp.where(lo, q_pair, 0.0).astype(BF16)
            qs_ref[g, head_rows(2 * pi + 1), :] = jnp.where(lo, 0.0, q_pair).astype(BF16)
        qs = qs_ref[g]
        s_ref[g, 0] = _nt_dot(qs, kvc_ref[:, kcol(g)])
        s_ref[g, 1] = _nt_dot(qs, kvp_ref[:, kcol(g)])

    for h in range(N_HEADS_B):
        hc = slice(h * HD_B, (h + 1) * HD_B)
        q, k, v = qb_ref[:, hc], kb_ref[:, hc], vb_ref[:, hc]
        scb_ref[h] = (_nt_dot(q, k) * dmask_ref[h]).astype(BF16)
        r = r_ref[h]
        inter_ref[h] = jnp.dot(q, r.astype(BF16), preferred_element_type=F32) * qdec_ref[h]
        kd = (k.astype(F32) * kdec_ref[h]).astype(BF16)
        r_ref[h] = math.exp(LOG_G[h] * C) * r + _tn_dot(kd, v)

    for g in range(N_KV_A):
        for hh in range(GROUP_A):
            rows = head_rows(hh)
            s_prev = jnp.where(has_prev, s_ref[g, 1, rows, :], neg_inf)
            s = jnp.where(tri, s_ref[g, 0, rows, :], s_prev) * (HD_A ** -0.5)
            sink = sink_ref[g * GROUP_A + hh]
            m = jnp.maximum(jnp.max(s, axis=-1, keepdims=True), sink)
            e = jnp.exp(s - m)
            denom = jnp.sum(e, axis=-1, keepdims=True) + jnp.exp(sink - m)
            p = e / denom
            p_ref[g, 0, rows, :] = jnp.where(tri, p, 0.0).astype(BF16)
            p_ref[g, 1, rows, :] = jnp.where(tri, 0.0, p).astype(BF16)

    for g in range(N_KV_A):
        o_ref[g] = (jnp.dot(p_ref[g, 0], kvc_ref[:, vcol(g)], preferred_element_type=F32)
                    + jnp.dot(p_ref[g, 1], kvp_ref[:, vcol(g)], preferred_element_type=F32))

    for h in range(N_HEADS_B):
        hc = slice(h * HD_B, (h + 1) * HD_B)
        o = jnp.dot(scb_ref[h], vb_ref[:, hc], preferred_element_type=F32) + inter_ref[h]
        mu = jnp.mean(o, axis=-1, keepdims=True)
        d = o - mu
        var = jnp.mean(d * d, axis=-1, keepdims=True)
        yh = d * lax.rsqrt(var + EPS) * gnw_ref[:, hc] * _silu(gb_ref[:, hc].astype(F32))
        y_ref[:, W_A + h * HD_B: W_A + (h + 1) * HD_B] = yh.astype(y_ref.dtype)

    for g in range(N_KV_A):
        for pi in range(half_group):
            pc = pair_col(g, pi)
            o_pair = jnp.where(lo, o_ref[g, head_rows(2 * pi), :], o_ref[g, head_rows(2 * pi + 1), :])
            y_ref[:, pc] = (o_pair * _silu(ga_ref[:, pc].astype(F32))).astype(y_ref.dtype)


def _mix_prompt(p, sinks, gn_w):
    t = p.shape[0]
    C = RET_CHUNK
    sec_spec = lambda s: pl.BlockSpec((C, W_A), lambda c, s=s: (c, s))
    return pl.pallas_call(
        _mix_prompt_kernel,
        grid=(t // C,),
        in_specs=[
            pl.BlockSpec(memory_space=pltpu.SMEM),
            sec_spec(SEC_QA), sec_spec(SEC_GA), sec_spec(SEC_QB),
            sec_spec(SEC_KB), sec_spec(SEC_VB), sec_spec(SEC_GB),
            pl.BlockSpec((C, COL_TILE), lambda c: (c, KV_TILE)),
            pl.BlockSpec((C, COL_TILE), lambda c: (jnp.maximum(c - 1, 0), KV_TILE)),
            pl.BlockSpec((1, W_B), lambda c: (0, 0)),
        ],
        out_specs=[
            pl.BlockSpec((C, D_MODEL), lambda c: (c, 0)),
            pl.BlockSpec((N_HEADS_B, HD_B, HD_B), lambda c: (0, 0, 0)),
        ],
        out_shape=[
            jax.ShapeDtypeStruct((t, D_MODEL), BF16),
            jax.ShapeDtypeStruct((N_HEADS_B, HD_B, HD_B), F32),
        ],
        scratch_shapes=[pltpu.VMEM((N_HEADS_B, C, LANES), F32)] * 3 + [
            pltpu.VMEM((N_KV_A, GROUP_A * C, LANES), BF16),
            pltpu.VMEM((N_KV_A, 2, GROUP_A * C, LANES), F32),
            pltpu.VMEM((N_KV_A, 2, GROUP_A * C, LANES), BF16),
            pltpu.VMEM((N_KV_A, GROUP_A * C, LANES), F32),
            pltpu.VMEM((N_HEADS_B, C, HD_B), BF16),
            pltpu.VMEM((N_HEADS_B, C, HD_B), F32),
        ],
        compiler_params=pltpu.CompilerParams(
            dimension_semantics=("arbitrary",),
            vmem_limit_bytes=VMEM_LIMIT),
        name="mix_prompt",
    )(sinks, p, p, p, p, p, p, p, p, gn_w)


SAMPLE_BB = 8
LANE_BATCH = LANES // 4


def _mix_sample_kernel(sinkrow_ref, q3_ref, ga3_ref, qb_ref, kb_ref, vb_ref, gb_ref, kvn_ref,
                       kbt_ref, vbg_ref, ck_ref, cv_ref, st_ref, gnw_ref, *rest, t_new):
    ya3_ref, yb_ref, nk_ref, nv_ref, nst_ref, s_ref, p_ref, vall_ref, kt_ref, x_ref, sc_ref = rest[-11:]
    i = pl.program_id(0)
    T = t_new
    BB = SAMPLE_BB
    L = ck_ref.shape[1]
    n_keys = L + T
    QR = T * GROUP_A
    R = BB * T
    lo = lax.broadcasted_iota(jnp.int32, (T, LANES), 1) < HD_A
    head_sl = lambda g: slice(g * GROUP_A, (g + 1) * GROUP_A)
    lane_sl = lambda g: slice(g * HD_A, (g + 1) * HD_A)
    tok_rows = lambda bb: slice(bb * T, (bb + 1) * T)


    for bb in range(BB):
        kvn = kvn_ref[tok_rows(bb), :]
        k_new = jnp.where(lo, kvn[:, 0:LANES], kvn[:, LANES:2 * LANES])
        v_new = jnp.where(lo, kvn[:, 2 * LANES:3 * LANES], kvn[:, 3 * LANES:4 * LANES])
        k_all = jnp.concatenate([ck_ref[bb], k_new], axis=0)
        v_all = jnp.concatenate([cv_ref[bb], v_new], axis=0)
        nk_ref[bb] = k_all[T:, :]
        nv_ref[bb] = v_all[T:, :]
        vall_ref[bb] = v_all.astype(BF16)
        k_all_b = k_all.astype(BF16)
        for g in range(N_KV_A):
            q = q3_ref[tok_rows(bb), head_sl(g), :].reshape(T * GROUP_A, HD_A).astype(BF16)
            s_ref[g, bb * QR:(bb + 1) * QR, :] = _nt_dot(q, k_all_b[:, lane_sl(g)])

    lane_t = (lax.broadcasted_iota(jnp.int32, (1, LANES), 1) % T).astype(F32)
    lane_b = lax.broadcasted_iota(jnp.int32, (HD_B, LANES), 1) // T
    slot0 = (i % (LANE_BATCH // BB)) * BB
    for h in range(N_HEADS_B):
        hc = slice(h * HD_B, (h + 1) * HD_B)
        lg = LOG_G[h]
        kdec = jnp.exp(lg * (T - 1.0 - lane_t))
        kt_dec = kbt_ref[hc, :] * kdec
        for bb in range(BB):
            kt_ref[h, bb * HD_B:(bb + 1) * HD_B, :] = jnp.where(lane_b == slot0 + bb, kt_dec, 0.0).astype(BF16)
        u = jnp.dot(kt_ref[h], vbg_ref[:, hc].astype(BF16), preferred_element_type=F32)
        for bb in range(BB):
            r = st_ref[bb, h]
            pr = slice((bb // 2) * 2 * T, (bb // 2 + 1) * 2 * T)
            x_ref[bb * N_HEADS_B + h] = jnp.dot(qb_ref[pr, hc].astype(BF16), r.astype(BF16),
                                                 preferred_element_type=F32)
            nst_ref[bb, h] = math.exp(lg * T) * r + u[bb * HD_B:(bb + 1) * HD_B, :]

    ri = lax.broadcasted_iota(jnp.int32, (R, R), 0)
    ci = lax.broadcasted_iota(jnp.int32, (R, R), 1)
    same_b = (ri // T) == (ci // T)
    dt = (ri % T - ci % T).astype(F32)
    keep = same_b & (dt >= 0)
    for h in range(N_HEADS_B):
        hc = slice(h * HD_B, (h + 1) * HD_B)
        dmask = jnp.where(keep, jnp.exp(LOG_G[h] * jnp.maximum(dt, 0.0)), 0.0)
        sc = _nt_dot(qb_ref[:, hc].astype(BF16), kb_ref[:, hc].astype(BF16)) * dmask
        sc_ref[h] = sc.astype(BF16)

    rows_q = (lax.broadcasted_iota(jnp.int32, (BB * QR, n_keys), 0) % QR) // GROUP_A
    key = lax.broadcasted_iota(jnp.int32, (BB * QR, n_keys), 1)
    delta = jnp.where(key < L, rows_q + L - key, rows_q - (key - L))
    valid = (delta >= 0) & (delta < WINDOW)
    for g in range(N_KV_A):
        s = jnp.where(valid, s_ref[g] * (HD_A ** -0.5), -jnp.inf)
        sink = sinkrow_ref[g]
        m = jnp.maximum(jnp.max(s, axis=-1, keepdims=True), sink)
        e = jnp.exp(s - m)
        denom = jnp.sum(e, axis=-1, keepdims=True) + jnp.exp(sink - m)
        p_ref[g] = (e / denom).astype(BF16)

    tposf = (lax.broadcasted_iota(jnp.int32, (R, 1), 0) % T).astype(F32)
    upper = (lax.broadcasted_iota(jnp.int32, (2 * T, 1), 0) < T)
    for h in range(N_HEADS_B):
        hc = slice(h * HD_B, (h + 1) * HD_B)
        intra = jnp.dot(sc_ref[h], vb_ref[:, hc].astype(BF16), preferred_element_type=F32)
        inter = jnp.concatenate(
            [jnp.where(upper, x_ref[(2 * pair) * N_HEADS_B + h], x_ref[(2 * pair + 1) * N_HEADS_B + h])
             for pair in range(BB // 2)], axis=0)
        o = intra + inter * jnp.exp(LOG_G[h] * (tposf + 1.0))
        mu = jnp.mean(o, axis=-1, keepdims=True)
        d = o - mu
        var = jnp.mean(d * d, axis=-1, keepdims=True)
        yb_ref[:, hc] = d * lax.rsqrt(var + EPS) * gnw_ref[:, hc] * _silu(gb_ref[:, hc])

    for bb in range(BB):
        for g in range(N_KV_A):
            o = jnp.dot(p_ref[g, bb * QR:(bb + 1) * QR, :], vall_ref[bb, :, lane_sl(g)],
                        preferred_element_type=F32)
            gate = _silu(ga3_ref[tok_rows(bb), head_sl(g), :].reshape(QR, HD_A))
            ya3_ref[tok_rows(bb), head_sl(g), :] = (o * gate).reshape(T, GROUP_A, HD_A)


def _mix_sample(ps, sinks, gn_w, cache_k, cache_v, state, d, prev_out, t_new):
    m = ps.shape[0]
    nb = m // t_new
    L = cache_k.shape[2]
    BB = SAMPLE_BB
    n_alias = len(prev_out)
    n_in = 14
    rows = BB * t_new
    qa = ps[:, SEC_QA * W_A:(SEC_QA + 1) * W_A].reshape(m, N_HEADS_A, HD_A)
    ga = ps[:, SEC_GA * W_A:(SEC_GA + 1) * W_A].reshape(m, N_HEADS_A, HD_A)
    kbt = ps[:, SEC_KB * W_A:(SEC_KB + 1) * W_A].T
    sinkrow = jnp.tile(sinks.reshape(N_KV_A, 1, GROUP_A), (1, BB * t_new, 1)).reshape(N_KV_A, -1, 1)
    sec_spec = lambda s: pl.BlockSpec((rows, W_A), lambda i, s=s: (i, s))
    steps_per_lane_group = LANE_BATCH // BB
    return pl.pallas_call(
        functools.partial(_mix_sample_kernel, t_new=t_new),
        grid=(nb // BB,),
        in_specs=[
            pl.BlockSpec((N_KV_A, BB * t_new * GROUP_A, 1), lambda i: (0, 0, 0)),
            pl.BlockSpec((rows, N_HEADS_A, HD_A), lambda i: (i, 0, 0)),
            pl.BlockSpec((rows, N_HEADS_A, HD_A), lambda i: (i, 0, 0)),
            sec_spec(SEC_QB), sec_spec(SEC_KB), sec_spec(SEC_VB), sec_spec(SEC_GB),
            pl.BlockSpec((rows, COL_TILE), lambda i: (i, KV_TILE)),
            pl.BlockSpec((W_B, LANES), lambda i: (0, i // steps_per_lane_group)),
            pl.BlockSpec((LANES, W_B), lambda i: (i // steps_per_lane_group, SEC_VB)),
            pl.BlockSpec((None, BB, L, LANES), lambda i: (d, i, 0, 0)),
            pl.BlockSpec((None, BB, L, LANES), lambda i: (d, i, 0, 0)),
            pl.BlockSpec((None, BB, N_HEADS_B, HD_B, HD_B), lambda i: (d, i, 0, 0, 0)),
            pl.BlockSpec((1, W_B), lambda i: (0, 0)),
        ] + [pl.BlockSpec(memory_space=pl.ANY)] * n_alias,
        out_specs=[
            pl.BlockSpec((rows, N_HEADS_A, HD_A), lambda i: (i, 0, 0)),
            pl.BlockSpec((rows, W_B), lambda i: (i, 0)),
            pl.BlockSpec((None, BB, L, LANES), lambda i: (d, i, 0, 0)),
            pl.BlockSpec((None, BB, L, LANES), lambda i: (d, i, 0, 0)),
            pl.BlockSpec((None, BB, N_HEADS_B, HD_B, HD_B), lambda i: (d, i, 0, 0, 0)),
        ],
        out_shape=[
            jax.ShapeDtypeStruct((m, N_HEADS_A, HD_A), F32),
            jax.ShapeDtypeStruct((m, W_B), F32),
            jax.ShapeDtypeStruct(cache_k.shape, F32),
            jax.ShapeDtypeStruct(cache_v.shape, F32),
            jax.ShapeDtypeStruct(state.shape, F32),
        ],
        input_output_aliases={n_in + a: 2 + a for a in range(n_alias)},
        scratch_shapes=[
            pltpu.VMEM((N_KV_A, BB * t_new * GROUP_A, L + t_new), F32),
            pltpu.VMEM((N_KV_A, BB * t_new * GROUP_A, L + t_new), BF16),
            pltpu.VMEM((BB, L + t_new, LANES), BF16),
            pltpu.VMEM((N_HEADS_B, BB * HD_B, LANES), BF16),
            pltpu.VMEM((BB * N_HEADS_B, 2 * t_new, HD_B), F32),
            pltpu.VMEM((N_HEADS_B, rows, rows), BF16),
        ],
        compiler_params=pltpu.CompilerParams(
            dimension_semantics=("arbitrary",),
            vmem_limit_bytes=VMEM_LIMIT),
        name="mix_sample",
    )(sinkrow, qa, ga, ps, ps, ps, ps, ps, kbt, ps, cache_k, cache_v, state, gn_w, *prev_out)


def _prep_w_kv(w_in):
    kv_width = N_KV_A * HD_A
    ka = w_in[..., W_A:W_A + kv_width]
    va = w_in[..., W_A + kv_width:W_A + 2 * kv_width]
    dup = lambda a: [a[..., g * HD_A:(g + 1) * HD_A] for g in range(N_KV_A) for _ in range(2)]
    return jnp.concatenate(dup(ka) + dup(va), axis=-1).astype(BF16)


def _rope_tables(pos):
    def tab(hd):
        half = hd // 2
        inv = ROPE_THETA ** (-jnp.arange(half, dtype=F32) / half)
        ang = pos.astype(F32)[:, None] * inv[None, :]
        cos, sin = jnp.cos(ang), jnp.sin(ang)
        reps = LANES // hd
        return (jnp.tile(jnp.concatenate([cos, cos], axis=1), (1, reps)),
                jnp.tile(jnp.concatenate([-sin, sin], axis=1), (1, reps)))
    ca, sa = tab(HD_A)
    cb, sb = tab(HD_B)
    scale = HD_B ** -0.5
    by_type = {EPI_ROPE_A: (ca, sa), EPI_ROPE_B: (cb, sb), EPI_ROPE_B_SCALED: (cb * scale, sb * scale),
               EPI_PLAIN: (jnp.ones_like(ca), jnp.zeros_like(sa))}
    return (jnp.stack([by_type[e][0] for e in range(N_EPI)]),
            jnp.stack([by_type[e][1] for e in range(N_EPI)]))


def kernel(x_prompt, x_sample, cache_k_win, cache_v_win, state_ret, norm_w, w_in, attn_sinks,
           ret_norm_w, w_out, final_norm_w):
    bp, tp, _ = x_prompt.shape
    bs, ts, _ = x_sample.shape
    assert bp == 1 and tp % RET_CHUNK == 0 and min(WINDOW, tp) == WINDOW
    L = cache_k_win.shape[2]
    depth = w_in.shape[0]

    w_in_b = w_in.astype(BF16)
    w_kv = _prep_w_kv(w_in)
    w_out_b = w_out.astype(BF16)
    tab_p = _rope_tables(jnp.arange(tp, dtype=jnp.int32))
    tab_s = _rope_tables(jnp.tile(PAST_LEN + jnp.arange(ts, dtype=jnp.int32), bs))
    fw = final_norm_w.reshape(1, D_MODEL)

    xp = x_prompt.reshape(tp, D_MODEL)
    xs = x_sample.reshape(bs * ts, D_MODEL)
    ck = cache_k_win.reshape(depth, bs, L, N_KV_A * HD_A)
    cv = cache_v_win.reshape(depth, bs, L, N_KV_A * HD_A)

    kp_l, vp_l, rp_l = [], [], []
    sample_out = ()
    for d in range(depth):
        final = d == depth - 1
        nw = norm_w[d].reshape(1, D_MODEL)
        gnw = ret_norm_w[d].reshape(1, W_B)
        p, tail = _in_proj(xp, nw, w_in_b, w_kv, d, tab_p, BF16, tm=1024)
        y, r_fin = _mix_prompt(p, attn_sinks[d], gnw)
        xp = _out_proj(y, xp, w_out_b[d], fw, final, tm=512)
        kp_l.append(jnp.stack([tail[:, 0:HD_A], tail[:, LANES:LANES + HD_A]], axis=1))
        vp_l.append(jnp.stack([tail[:, 2 * LANES:2 * LANES + HD_A], tail[:, 3 * LANES:3 * LANES + HD_A]], axis=1))
        rp_l.append(r_fin)
        ps, _ = _in_proj(xs, nw, w_in_b, w_kv, d, tab_s, F32, tm=bs * ts)
        ya3, yb, *sample_out = _mix_sample(ps, attn_sinks[d], gnw, ck, cv, state_ret, d, sample_out, ts)
        ys = jnp.concatenate([ya3.reshape(bs * ts, W_A), yb], axis=1).astype(BF16)
        xs = _out_proj(ys, xs, w_out_b[d], fw, final, tm=bs * ts)

    keep = min(WINDOW, tp)
    nk, nv, nst = sample_out
    return (xp.reshape(bp, tp, D_MODEL), xs.reshape(bs, ts, D_MODEL),
            jnp.stack(kp_l).reshape(depth, bp, keep, N_KV_A, HD_A),
            jnp.stack(vp_l).reshape(depth, bp, keep, N_KV_A, HD_A),
            jnp.stack(rp_l).reshape(depth, bp, N_HEADS_B, HD_B, HD_B),
            nk.reshape(depth, bs, L, N_KV_A, HD_A), nv.reshape(depth, bs, L, N_KV_A, HD_A), nst)
```

```python
import functools
import math

import jax
import jax.numpy as jnp
from jax import lax
from jax.experimental import pallas as pl
from jax.experimental.pallas import tpu as pltpu

F32 = jnp.float32
BF16 = jnp.bfloat16

D_MODEL = 2048
DEPTH = 2
PAST_LEN = 8192
WINDOW = 128
HD_A = 64
N_HEADS_A = 16
N_KV_A = 2
GROUP_A = N_HEADS_A // N_KV_A
W_A = N_HEADS_A * HD_A
HD_B = 128
N_HEADS_B = 8
W_B = N_HEADS_B * HD_B
RET_CHUNK = 128
ROPE_THETA = 10000.0
EPS = 1e-6

LANES = 128
MXU_WIDTH = 256
SEC_QA, SEC_GA, SEC_QB, SEC_KB, SEC_VB, SEC_GB = 0, 1, 2, 3, 4, 5
IN_TILE = W_A
KV_TILE = 6
COL_TILE = 4 * N_KV_A * HD_A
KV_BLOCK = KV_TILE * IN_TILE // COL_TILE
D_PROJ_R = (KV_TILE + 1) * IN_TILE
VMEM_LIMIT = 56 * 1024 * 1024

LOG_G = [math.log1p(-(2.0 ** (-5.0 - h))) for h in range(N_HEADS_B)]


def _silu(g):
    return g * (1.0 / (1.0 + jnp.exp(-g)))


EPI_ROPE_A, EPI_ROPE_B, EPI_ROPE_B_SCALED, EPI_PLAIN = 0, 1, 2, 3
N_EPI = 4
_SEC_EPI = (EPI_ROPE_A, EPI_PLAIN, EPI_ROPE_B, EPI_ROPE_B_SCALED, EPI_PLAIN, EPI_PLAIN)


KV_ROPE_GROUPS = N_KV_A
N_W_BLOCKS = IN_TILE // MXU_WIDTH


def _tile_epi(tile, hi):
    t = jnp.int32(EPI_PLAIN if hi else EPI_ROPE_A)
    for s, e in enumerate(_SEC_EPI):
        t = jnp.where(tile == s, e, t)
    return t


def _in_proj_kernel(x_ref, nw_ref, *refs, row_chunk):
    w_refs = refs[:N_W_BLOCKS]
    wkv_ref, c0_ref, s0_ref, c1_ref, s1_ref, o_ref, tail_ref, h_ref, acc_ref = refs[N_W_BLOCKS:]
    i = pl.program_id(0)
    j = pl.program_id(1)
    tm = x_ref.shape[0]
    n_chunks = tm // row_chunk
    n_groups = IN_TILE // LANES
    slot = j % 2
    prev_slot = 1 - slot
    prev_tile = j - 1

    @pl.when(j == 0)
    def _():
        def body(r, carry):
            rows = pl.ds(pl.multiple_of(r * row_chunk, row_chunk), row_chunk)
            x = x_ref[rows, :]
            ms = jnp.mean(x * x, axis=-1, keepdims=True)
            h_ref[rows, :] = (x * lax.rsqrt(ms + EPS) * nw_ref[...]).astype(BF16)
            return carry
        lax.fori_loop(0, n_chunks, body, 0)

        @pl.when(i == 0)
        def _():
            acc_ref[1] = jnp.zeros(acc_ref.shape[1:], F32)

    def epilogue():
        lane = lax.broadcasted_iota(jnp.int32, (row_chunk, LANES), 1)
        first_half = (lane % HD_A) < HD_A // 2
        for hi in range(2):
            c_ref, s_ref = (c0_ref, s0_ref) if hi == 0 else (c1_ref, s1_ref)
            is_a = _tile_epi(prev_tile, hi) == EPI_ROPE_A
            shift_up = jnp.where(is_a, LANES - HD_A // 2, HD_B // 2)
            shift_dn = jnp.where(is_a, HD_A // 2, HD_B // 2)
            groups = range(KV_ROPE_GROUPS) if hi == 0 else range(KV_ROPE_GROUPS, n_groups)
            for r in range(n_chunks):
                rows = slice(r * row_chunk, (r + 1) * row_chunk)
                c, s = c_ref[rows, :], s_ref[rows, :]
                for g in groups:
                    cols = slice(g * LANES, (g + 1) * LANES)
                    x = acc_ref[prev_slot, rows, cols]
                    partner = jnp.where(first_half, pltpu.roll(x, shift_up, 1), pltpu.roll(x, shift_dn, 1))
                    out = x * c + partner * s
                    o_ref[rows, cols] = out.astype(o_ref.dtype)
                    if r == n_chunks - 1 and g < COL_TILE // LANES:
                        tail_ref[:, cols] = out[row_chunk - WINDOW:, :]

    @pl.when(j < KV_TILE)
    def _():
        epilogue()
        for e, w_ref in enumerate(w_refs):
            acc_ref[slot, :, e * MXU_WIDTH:(e + 1) * MXU_WIDTH] = jnp.dot(
                h_ref[...], w_ref[...], preferred_element_type=F32)

    @pl.when(j == KV_TILE)
    def _():
        epilogue()
        acc_ref[slot, :, :COL_TILE] = jnp.dot(h_ref[...], wkv_ref[...], preferred_element_type=F32)

    @pl.when(j == KV_TILE + 1)
    def _():
        epilogue()


def _in_proj(x, norm_w, w_all, wkv_all, d, tables, out_dtype, tm):
    m = x.shape[0]
    n_tiles = D_PROJ_R // IN_TILE
    n_src_blocks = w_all.shape[2] // MXU_WIDTH
    c_tab, s_tab = tables

    def src_block(j, e):
        blk = j * N_W_BLOCKS + jnp.minimum(j, 1) + e
        return jnp.minimum(blk, n_src_blocks - 1)

    def w_spec(e):
        return pl.BlockSpec((None, D_MODEL, MXU_WIDTH), lambda i, j: (d, 0, src_block(j, e)))

    def tab_spec(hi):
        return pl.BlockSpec((None, tm, LANES), lambda i, j: (_tile_epi(jnp.maximum(j - 1, 0), hi), i, 0))

    return pl.pallas_call(
        functools.partial(_in_proj_kernel, row_chunk=min(tm, 256)),
        grid=(m // tm, n_tiles + 1),
        in_specs=[
            pl.BlockSpec((tm, D_MODEL), lambda i, j: (i, 0)),
            pl.BlockSpec((1, D_MODEL), lambda i, j: (0, 0)),
        ] + [w_spec(e) for e in range(N_W_BLOCKS)] + [
            pl.BlockSpec((None, D_MODEL, COL_TILE), lambda i, j: (d, 0, 0)),
            tab_spec(0), tab_spec(0), tab_spec(1), tab_spec(1),
        ],
        out_specs=[
            pl.BlockSpec((tm, IN_TILE), lambda i, j: (i, jnp.maximum(j - 1, 0))),
            pl.BlockSpec((WINDOW, COL_TILE), lambda i, j: (0, 0)),
        ],
        out_shape=[
            jax.ShapeDtypeStruct((m, D_PROJ_R), out_dtype),
            jax.ShapeDtypeStruct((WINDOW, COL_TILE), F32),
        ],
        scratch_shapes=[
            pltpu.VMEM((tm, D_MODEL), BF16),
            pltpu.VMEM((2, tm, IN_TILE), F32),
        ],
        compiler_params=pltpu.CompilerParams(
            dimension_semantics=("arbitrary", "arbitrary"),
            vmem_limit_bytes=VMEM_LIMIT),
        name="in_proj",
    )(x, norm_w, *([w_all] * N_W_BLOCKS), wkv_all, c_tab, s_tab, c_tab, s_tab)


def _out_proj_kernel(y_ref, x_ref, w_ref, fw_ref, o_ref, *, final):
    acc = x_ref[...] + jnp.dot(y_ref[...], w_ref[...], preferred_element_type=F32)
    if final:
        ms = jnp.mean(acc * acc, axis=-1, keepdims=True)
        acc = acc * lax.rsqrt(ms + EPS) * fw_ref[...]
    o_ref[...] = acc


def _out_proj(y, x, w, final_w, final, tm):
    m = x.shape[0]
    return pl.pallas_call(
        functools.partial(_out_proj_kernel, final=final),
        grid=(m // tm,),
        in_specs=[
            pl.BlockSpec((tm, D_MODEL), lambda i: (i, 0)),
            pl.BlockSpec((tm, D_MODEL), lambda i: (i, 0)),
            pl.BlockSpec((D_MODEL, D_MODEL), lambda i: (0, 0)),
            pl.BlockSpec((1, D_MODEL), lambda i: (0, 0)),
        ],
        out_specs=pl.BlockSpec((tm, D_MODEL), lambda i: (i, 0)),
        out_shape=jax.ShapeDtypeStruct((m, D_MODEL), F32),
        compiler_params=pltpu.CompilerParams(
            dimension_semantics=("arbitrary",),
            vmem_limit_bytes=VMEM_LIMIT),
        name="out_proj",
    )(y, x, w, final_w)


def _nt_dot(a, b):
    return lax.dot_general(a, b, (((1,), (1,)), ((), ())), preferred_element_type=F32)


def _tn_dot(a, b):
    return lax.dot_general(a, b, (((0,), (0,)), ((), ())), preferred_element_type=F32)


MIX_CHUNKS = 4


def _mix_prompt_kernel(sink_ref, qa_ref, ga_ref, qb_ref, kb_ref, vb_ref, gb_ref,
                       kv_ref, gnw_ref, y_ref, r_ref,
                       dmask_ref, qdec_ref, kdec_ref, kvp_ref,
                       qs_ref, s_ref, p_ref, o_ref, scb_ref, inter_ref):
    step = pl.program_id(0)
    C = RET_CHUNK
    row = lax.broadcasted_iota(jnp.int32, (C, LANES), 0)
    lane = lax.broadcasted_iota(jnp.int32, (C, LANES), 1)

    @pl.when(step == 0)
    def _():
        r_ref[...] = jnp.zeros(r_ref.shape, F32)
        kvp_ref[...] = jnp.zeros(kvp_ref.shape, kvp_ref.dtype)
        rowf = row.astype(F32)
        diff = rowf - lane.astype(F32)
        for h in range(N_HEADS_B):
            dmask_ref[h] = jnp.where(diff >= 0, jnp.exp(LOG_G[h] * jnp.maximum(diff, 0.0)), 0.0)
            qdec_ref[h] = jnp.exp(LOG_G[h] * (rowf + 1.0))
            kdec_ref[h] = jnp.exp(LOG_G[h] * (C - 1.0 - rowf))

    tri = lane <= row
    lo = lane < HD_A
    neg_inf = jnp.float32(-jnp.inf)
    half_group = GROUP_A // 2
    kcol = lambda g: slice(g * LANES, (g + 1) * LANES)
    vcol = lambda g: slice((N_KV_A + g) * LANES, (N_KV_A + g + 1) * LANES)
    pair_col = lambda g, pi: slice((g * half_group + pi) * LANES, (g * half_group + pi + 1) * LANES)
    head_rows = lambda hh: slice(hh * C, (hh + 1) * C)

    def chunk(cc, carry):
        rows = pl.ds(pl.multiple_of(cc * C, C), C)
        has_prev = step * MIX_CHUNKS + cc > 0

        for g in range(N_KV_A):
            for pi in range(half_group):
                q_pair = qa_ref[rows, pair_col(g, pi)].astype(F32)
                qs_ref[g, head_rows(2 * pi), :] = jnp.where(lo, q_pair, 0.0).astype(BF16)
                qs_ref[g, head_rows(2 * pi + 1), :] = jnp.where(lo, 0.0, q_pair).astype(BF16)
            qs = qs_ref[g]
            s_ref[g, 0] = _nt_dot(qs, kv_ref[rows, kcol(g)])
            s_ref[g, 1] = _nt_dot(qs, kvp_ref[:, kcol(g)])

        for h in range(N_HEADS_B):
            hc = slice(h * HD_B, (h + 1) * HD_B)
            q, k, v = qb_ref[rows, hc], kb_ref[rows, hc], vb_ref[rows, hc]
            scb_ref[h] = (_nt_dot(q, k) * dmask_ref[h]).astype(BF16)
            r = r_ref[h]
            inter_ref[h] = jnp.dot(q, r.astype(BF16), preferred_element_type=F32) * qdec_ref[h]
            kd = (k.astype(F32) * kdec_ref[h]).astype(BF16)
            r_ref[h] = math.exp(LOG_G[h] * C) * r + _tn_dot(kd, v)

        for g in range(N_KV_A):
            for hh in range(GROUP_A):
                hr = head_rows(hh)
                s_prev = jnp.where(has_prev, s_ref[g, 1, hr, :], neg_inf)
                s = jnp.where(tri, s_ref[g, 0, hr, :], s_prev) * (HD_A ** -0.5)
                sink = sink_ref[g * GROUP_A + hh]
                m = jnp.maximum(jnp.max(s, axis=-1, keepdims=True), sink)
                e = jnp.exp(s - m)
                denom = jnp.sum(e, axis=-1, keepdims=True) + jnp.exp(sink - m)
                p = e / denom
                p_ref[g, 0, hr, :] = jnp.where(tri, p, 0.0).astype(BF16)
                p_ref[g, 1, hr, :] = jnp.where(tri, 0.0, p).astype(BF16)

        for g in range(N_KV_A):
            o_ref[g] = (jnp.dot(p_ref[g, 0], kv_ref[rows, vcol(g)], preferred_element_type=F32)
                        + jnp.dot(p_ref[g, 1], kvp_ref[:, vcol(g)], preferred_element_type=F32))

        for h in range(N_HEADS_B):
            hc = slice(h * HD_B, (h + 1) * HD_B)
            o = jnp.dot(scb_ref[h], vb_ref[rows, hc], preferred_element_type=F32) + inter_ref[h]
            mu = jnp.mean(o, axis=-1, keepdims=True)
            d = o - mu
            var = jnp.mean(d * d, axis=-1, keepdims=True)
            yh = d * lax.rsqrt(var + EPS) * gnw_ref[:, hc] * _silu(gb_ref[rows, hc].astype(F32))
            y_ref[rows, W_A + h * HD_B: W_A + (h + 1) * HD_B] = yh.astype(y_ref.dtype)

        for g in range(N_KV_A):
            for pi in range(half_group):
                pc = pair_col(g, pi)
                o_pair = jnp.where(lo, o_ref[g, head_rows(2 * pi), :], o_ref[g, head_rows(2 * pi + 1), :])
                y_ref[rows, pc] = (o_pair * _silu(ga_ref[rows, pc].astype(F32))).astype(y_ref.dtype)

        kvp_ref[...] = kv_ref[rows, :]
        return carry

    lax.fori_loop(0, MIX_CHUNKS, chunk, 0)


def _mix_prompt(p, sinks, gn_w):
    t = p.shape[0]
    C = RET_CHUNK
    rows = MIX_CHUNKS * C
    sec_spec = lambda s: pl.BlockSpec((rows, W_A), lambda c, s=s: (c, s))
    return pl.pallas_call(
        _mix_prompt_kernel,
        grid=(t // rows,),
        in_specs=[
            pl.BlockSpec(memory_space=pltpu.SMEM),
            sec_spec(SEC_QA), sec_spec(SEC_GA), sec_spec(SEC_QB),
            sec_spec(SEC_KB), sec_spec(SEC_VB), sec_spec(SEC_GB),
            pl.BlockSpec((rows, COL_TILE), lambda c: (c, KV_BLOCK)),
            pl.BlockSpec((1, W_B), lambda c: (0, 0)),
        ],
        out_specs=[
            pl.BlockSpec((rows, D_MODEL), lambda c: (c, 0)),
            pl.BlockSpec((N_HEADS_B, HD_B, HD_B), lambda c: (0, 0, 0)),
        ],
        out_shape=[
            jax.ShapeDtypeStruct((t, D_MODEL), BF16),
            jax.ShapeDtypeStruct((N_HEADS_B, HD_B, HD_B), F32),
        ],
        scratch_shapes=[pltpu.VMEM((N_HEADS_B, C, LANES), F32)] * 3 + [
            pltpu.VMEM((C, COL_TILE), BF16),
            pltpu.VMEM((N_KV_A, GROUP_A * C, LANES), BF16),
            pltpu.VMEM((N_KV_A, 2, GROUP_A * C, LANES), F32),
            pltpu.VMEM((N_KV_A, 2, GROUP_A * C, LANES), BF16),
            pltpu.VMEM((N_KV_A, GROUP_A * C, LANES), F32),
            pltpu.VMEM((N_HEADS_B, C, HD_B), BF16),
            pltpu.VMEM((N_HEADS_B, C, HD_B), F32),
        ],
        compiler_params=pltpu.CompilerParams(
            dimension_semantics=("arbitrary",),
            vmem_limit_bytes=VMEM_LIMIT),
        name="mix_prompt",
    )(sinks, p, p, p, p, p, p, p, gn_w)


SAMPLE_BB = 8
LANE_BATCH = LANES // 4


def _mix_sample_kernel(sinkrow_ref, q3_ref, ga3_ref, qb_ref, kb_ref, vb_ref, gb_ref, kvn_ref,
                       kbt_ref, vbg_ref, ck_ref, cv_ref, st_ref, gnw_ref, *rest, t_new):
    ya3_ref, yb_ref, nk_ref, nv_ref, nst_ref, s_ref, p_ref, vall_ref, kt_ref, x_ref, sc_ref = rest[-11:]
    i = pl.program_id(0)
    T = t_new
    BB = SAMPLE_BB
    L = ck_ref.shape[1]
    n_keys = L + T
    QR = T * GROUP_A
    R = BB * T
    lo = lax.broadcasted_iota(jnp.int32, (T, LANES), 1) < HD_A
    head_sl = lambda g: slice(g * GROUP_A, (g + 1) * GROUP_A)
    lane_sl = lambda g: slice(g * HD_A, (g + 1) * HD_A)
    tok_rows = lambda bb: slice(bb * T, (bb + 1) * T)


    for bb in range(BB):
        kvn = kvn_ref[tok_rows(bb), :]
        k_new = jnp.where(lo, kvn[:, 0:LANES], kvn[:, LANES:2 * LANES])
        v_new = jnp.where(lo, kvn[:, 2 * LANES:3 * LANES], kvn[:, 3 * LANES:4 * LANES])
        k_all = jnp.concatenate([ck_ref[bb], k_new], axis=0)
        v_all = jnp.concatenate([cv_ref[bb], v_new], axis=0)
        nk_ref[bb] = k_all[T:, :]
        nv_ref[bb] = v_all[T:, :]
        vall_ref[bb] = v_all.astype(BF16)
        k_all_b = k_all.astype(BF16)
        for g in range(N_KV_A):
            q = q3_ref[tok_rows(bb), head_sl(g), :].reshape(T * GROUP_A, HD_A).astype(BF16)
            s_ref[g, bb * QR:(bb + 1) * QR, :] = _nt_dot(q, k_all_b[:, lane_sl(g)])

    lane_t = (lax.broadcasted_iota(jnp.int32, (1, LANES), 1) % T).astype(F32)
    lane_b = lax.broadcasted_iota(jnp.int32, (HD_B, LANES), 1) // T
    slot0 = (i % (LANE_BATCH // BB)) * BB
    for h in range(N_HEADS_B):
        hc = slice(h * HD_B, (h + 1) * HD_B)
        lg = LOG_G[h]
        kdec = jnp.exp(lg * (T - 1.0 - lane_t))
        kt_dec = kbt_ref[hc, :] * kdec
        for bb in range(BB):
            kt_ref[h, bb * HD_B:(bb + 1) * HD_B, :] = jnp.where(lane_b == slot0 + bb, kt_dec, 0.0).astype(BF16)
        u = jnp.dot(kt_ref[h], vbg_ref[:, hc].astype(BF16), preferred_element_type=F32)
        for bb in range(BB):
            r = st_ref[bb, h]
            pr = slice((bb // 2) * 2 * T, (bb // 2 + 1) * 2 * T)
            x_ref[bb * N_HEADS_B + h] = jnp.dot(qb_ref[pr, hc].astype(BF16), r.astype(BF16),
                                                 preferred_element_type=F32)
            nst_ref[bb, h] = math.exp(lg * T) * r + u[bb * HD_B:(bb + 1) * HD_B, :]

    ri = lax.broadcasted_iota(jnp.int32, (R, R), 0)
    ci = lax.broadcasted_iota(jnp.int32, (R, R), 1)
    same_b = (ri // T) == (ci // T)
    dt = (ri % T - ci % T).astype(F32)
    keep = same_b & (dt >= 0)
    for h in range(N_HEADS_B):
        hc = slice(h * HD_B, (h + 1) * HD_B)
        dmask = jnp.where(keep, jnp.exp(LOG_G[h] * jnp.maximum(dt, 0.0)), 0.0)
        sc = _nt_dot(qb_ref[:, hc].astype(BF16), kb_ref[:, hc].astype(BF16)) * dmask
        sc_ref[h] = sc.astype(BF16)

    rows_q = (lax.broadcasted_iota(jnp.int32, (BB * QR, n_keys), 0) % QR) // GROUP_A
    key = lax.broadcasted_iota(jnp.int32, (BB * QR, n_keys), 1)
    delta = jnp.where(key < L, rows_q + L - key, rows_q - (key - L))
    valid = (delta >= 0) & (delta < WINDOW)
    for g in range(N_KV_A):
        s = jnp.where(valid, s_ref[g] * (HD_A ** -0.5), -jnp.inf)
        sink = sinkrow_ref[g]
        m = jnp.maximum(jnp.max(s, axis=-1, keepdims=True), sink)
        e = jnp.exp(s - m)
        denom = jnp.sum(e, axis=-1, keepdims=True) + jnp.exp(sink - m)
        p_ref[g] = (e / denom).astype(BF16)

    tposf = (lax.broadcasted_iota(jnp.int32, (R, 1), 0) % T).astype(F32)
    upper = (lax.broadcasted_iota(jnp.int32, (2 * T, 1), 0) < T)
    for h in range(N_HEADS_B):
        hc = slice(h * HD_B, (h + 1) * HD_B)
        intra = jnp.dot(sc_ref[h], vb_ref[:, hc].astype(BF16), preferred_element_type=F32)
        inter = jnp.concatenate(
            [jnp.where(upper, x_ref[(2 * pair) * N_HEADS_B + h], x_ref[(2 * pair + 1) * N_HEADS_B + h])
             for pair in range(BB // 2)], axis=0)
        o = intra + inter * jnp.exp(LOG_G[h] * (tposf + 1.0))
        mu = jnp.mean(o, axis=-1, keepdims=True)
        d = o - mu
        var = jnp.mean(d * d, axis=-1, keepdims=True)
        yb_ref[:, hc] = d * lax.rsqrt(var + EPS) * gnw_ref[:, hc] * _silu(gb_ref[:, hc])

    for bb in range(BB):
        for g in range(N_KV_A):
            o = jnp.dot(p_ref[g, bb * QR:(bb + 1) * QR, :], vall_ref[bb, :, lane_sl(g)],
                        preferred_element_type=F32)
            gate = _silu(ga3_ref[tok_rows(bb), head_sl(g), :].reshape(QR, HD_A))
            ya3_ref[tok_rows(bb), head_sl(g), :] = (o * gate).reshape(T, GROUP_A, HD_A)


def _mix_sample(ps, sinks, gn_w, cache_k, cache_v, state, d, prev_out, t_new):
    m = ps.shape[0]
    nb = m // t_new
    L = cache_k.shape[2]
    BB = SAMPLE_BB
    n_alias = len(prev_out)
    n_in = 14
    rows = BB * t_new
    qa = ps[:, SEC_QA * W_A:(SEC_QA + 1) * W_A].reshape(m, N_HEADS_A, HD_A)
    ga = ps[:, SEC_GA * W_A:(SEC_GA + 1) * W_A].reshape(m, N_HEADS_A, HD_A)
    kbt = ps[:, SEC_KB * W_A:(SEC_KB + 1) * W_A].T
    sinkrow = jnp.tile(sinks.reshape(N_KV_A, 1, GROUP_A), (1, BB * t_new, 1)).reshape(N_KV_A, -1, 1)
    sec_spec = lambda s: pl.BlockSpec((rows, W_A), lambda i, s=s: (i, s))
    steps_per_lane_group = LANE_BATCH // BB
    return pl.pallas_call(
        functools.partial(_mix_sample_kernel, t_new=t_new),
        grid=(nb // BB,),
        in_specs=[
            pl.BlockSpec((N_KV_A, BB * t_new * GROUP_A, 1), lambda i: (0, 0, 0)),
            pl.BlockSpec((rows, N_HEADS_A, HD_A), lambda i: (i, 0, 0)),
            pl.BlockSpec((rows, N_HEADS_A, HD_A), lambda i: (i, 0, 0)),
            sec_spec(SEC_QB), sec_spec(SEC_KB), sec_spec(SEC_VB), sec_spec(SEC_GB),
            pl.BlockSpec((rows, COL_TILE), lambda i: (i, KV_BLOCK)),
            pl.BlockSpec((W_B, LANES), lambda i: (0, i // steps_per_lane_group)),
            pl.BlockSpec((LANES, W_B), lambda i: (i // steps_per_lane_group, SEC_VB)),
            pl.BlockSpec((None, BB, L, LANES), lambda i: (d, i, 0, 0)),
            pl.BlockSpec((None, BB, L, LANES), lambda i: (d, i, 0, 0)),
            pl.BlockSpec((None, BB, N_HEADS_B, HD_B, HD_B), lambda i: (d, i, 0, 0, 0)),
            pl.BlockSpec((1, W_B), lambda i: (0, 0)),
        ] + [pl.BlockSpec(memory_space=pl.ANY)] * n_alias,
        out_specs=[
            pl.BlockSpec((rows, N_HEADS_A, HD_A), lambda i: (i, 0, 0)),
            pl.BlockSpec((rows, W_B), lambda i: (i, 0)),
            pl.BlockSpec((None, BB, L, LANES), lambda i: (d, i, 0, 0)),
            pl.BlockSpec((None, BB, L, LANES), lambda i: (d, i, 0, 0)),
            pl.BlockSpec((None, BB, N_HEADS_B, HD_B, HD_B), lambda i: (d, i, 0, 0, 0)),
        ],
        out_shape=[
            jax.ShapeDtypeStruct((m, N_HEADS_A, HD_A), F32),
            jax.ShapeDtypeStruct((m, W_B), F32),
            jax.ShapeDtypeStruct(cache_k.shape, F32),
            jax.ShapeDtypeStruct(cache_v.shape, F32),
            jax.ShapeDtypeStruct(state.shape, F32),
        ],
        input_output_aliases={n_in + a: 2 + a for a in range(n_alias)},
        scratch_shapes=[
            pltpu.VMEM((N_KV_A, BB * t_new * GROUP_A, L + t_new), F32),
            pltpu.VMEM((N_KV_A, BB * t_new * GROUP_A, L + t_new), BF16),
            pltpu.VMEM((BB, L + t_new, LANES), BF16),
            pltpu.VMEM((N_HEADS_B, BB * HD_B, LANES), BF16),
            pltpu.VMEM((BB * N_HEADS_B, 2 * t_new, HD_B), F32),
            pltpu.VMEM((N_HEADS_B, rows, rows), BF16),
        ],
        compiler_params=pltpu.CompilerParams(
            dimension_semantics=("arbitrary",),
            vmem_limit_bytes=VMEM_LIMIT),
        name="mix_sample",
    )(sinkrow, qa, ga, ps, ps, ps, ps, ps, kbt, ps, cache_k, cache_v, state, gn_w, *prev_out)


def _prep_w_kv(w_in):
    kv_width = N_KV_A * HD_A
    ka = w_in[..., W_A:W_A + kv_width]
    va = w_in[..., W_A + kv_width:W_A + 2 * kv_width]
    dup = lambda a: [a[..., g * HD_A:(g + 1) * HD_A] for g in range(N_KV_A) for _ in range(2)]
    return jnp.concatenate(dup(ka) + dup(va), axis=-1).astype(BF16)


def _rope_tables(pos):
    def tab(hd):
        half = hd // 2
        inv = ROPE_THETA ** (-jnp.arange(half, dtype=F32) / half)
        ang = pos.astype(F32)[:, None] * inv[None, :]
        cos, sin = jnp.cos(ang), jnp.sin(ang)
        reps = LANES // hd
        return (jnp.tile(jnp.concatenate([cos, cos], axis=1), (1, reps)),
                jnp.tile(jnp.concatenate([-sin, sin], axis=1), (1, reps)))
    ca, sa = tab(HD_A)
    cb, sb = tab(HD_B)
    scale = HD_B ** -0.5
    by_type = {EPI_ROPE_A: (ca, sa), EPI_ROPE_B: (cb, sb), EPI_ROPE_B_SCALED: (cb * scale, sb * scale),
               EPI_PLAIN: (jnp.ones_like(ca), jnp.zeros_like(sa))}
    return (jnp.stack([by_type[e][0] for e in range(N_EPI)]),
            jnp.stack([by_type[e][1] for e in range(N_EPI)]))


def kernel(x_prompt, x_sample, cache_k_win, cache_v_win, state_ret, norm_w, w_in, attn_sinks,
           ret_norm_w, w_out, final_norm_w):
    bp, tp, _ = x_prompt.shape
    bs, ts, _ = x_sample.shape
    assert bp == 1 and tp % RET_CHUNK == 0 and min(WINDOW, tp) == WINDOW
    L = cache_k_win.shape[2]
    depth = w_in.shape[0]

    w_in_b = w_in.astype(BF16)
    w_kv = _prep_w_kv(w_in)
    w_out_b = w_out.astype(BF16)
    tab_p = _rope_tables(jnp.arange(tp, dtype=jnp.int32))
    tab_s = _rope_tables(jnp.tile(PAST_LEN + jnp.arange(ts, dtype=jnp.int32), bs))
    fw = final_norm_w.reshape(1, D_MODEL)

    xp = x_prompt.reshape(tp, D_MODEL)
    xs = x_sample.reshape(bs * ts, D_MODEL)
    ck = cache_k_win.reshape(depth, bs, L, N_KV_A * HD_A)
    cv = cache_v_win.reshape(depth, bs, L, N_KV_A * HD_A)

    kp_l, vp_l, rp_l = [], [], []
    sample_out = ()
    for d in range(depth):
        final = d == depth - 1
        nw = norm_w[d].reshape(1, D_MODEL)
        gnw = ret_norm_w[d].reshape(1, W_B)
        p, tail = _in_proj(xp, nw, w_in_b, w_kv, d, tab_p, BF16, tm=1024)
        y, r_fin = _mix_prompt(p, attn_sinks[d], gnw)
        xp = _out_proj(y, xp, w_out_b[d], fw, final, tm=512)
        kp_l.append(jnp.stack([tail[:, 0:HD_A], tail[:, LANES:LANES + HD_A]], axis=1))
        vp_l.append(jnp.stack([tail[:, 2 * LANES:2 * LANES + HD_A], tail[:, 3 * LANES:3 * LANES + HD_A]], axis=1))
        rp_l.append(r_fin)
        ps, _ = _in_proj(xs, nw, w_in_b, w_kv, d, tab_s, F32, tm=bs * ts)
        ya3, yb, *sample_out = _mix_sample(ps, attn_sinks[d], gnw, ck, cv, state_ret, d, sample_out, ts)
        ys = jnp.concatenate([ya3.reshape(bs * ts, W_A), yb], axis=1).astype(BF16)
        xs = _out_proj(ys, xs, w_out_b[d], fw, final, tm=bs * ts)

    keep = min(WINDOW, tp)
    nk, nv, nst = sample_out
    return (xp.reshape(bp, tp, D_MODEL), xs.reshape(bs, ts, D_MODEL),
            jnp.stack(kp_l).reshape(depth, bp, keep, N_KV_A, HD_A),
            jnp.stack(vp_l).reshape(depth, bp, keep, N_KV_A, HD_A),
            jnp.stack(rp_l).reshape(depth, bp, N_HEADS_B, HD_B, HD_B),
            nk.reshape(depth, bs, L, N_KV_A, HD_A), nv.reshape(depth, bs, L, N_KV_A, HD_A), nst)
```

```python
import functools
import math

import jax
import jax.numpy as jnp
from jax import lax
from jax.experimental import pallas as pl
from jax.experimental.pallas import tpu as pltpu

F32 = jnp.float32
BF16 = jnp.bfloat16

D_MODEL = 2048
DEPTH = 2
PAST_LEN = 8192
WINDOW = 128
HD_A = 64
N_HEADS_A = 16
N_KV_A = 2
GROUP_A = N_HEADS_A // N_KV_A
W_A = N_HEADS_A * HD_A
HD_B = 128
N_HEADS_B = 8
W_B = N_HEADS_B * HD_B
RET_CHUNK = 128
ROPE_THETA = 10000.0
EPS = 1e-6

LANES = 128
MXU_WIDTH = 256
SEC_QA, SEC_GA, SEC_QB, SEC_KB, SEC_VB, SEC_GB = 0, 1, 2, 3, 4, 5
IN_TILE = 1024
TILES_PER_SEC = W_A // IN_TILE
KV_TILE = 6 * TILES_PER_SEC
COL_TILE = 4 * N_KV_A * HD_A
KV_BLOCK = KV_TILE * IN_TILE // COL_TILE
D_PROJ_R = (KV_TILE + 1) * IN_TILE
VMEM_LIMIT = 56 * 1024 * 1024

LOG_G = [math.log1p(-(2.0 ** (-5.0 - h))) for h in range(N_HEADS_B)]


def _silu(g):
    return g * (1.0 / (1.0 + jnp.exp(-g)))


EPI_ROPE_A, EPI_ROPE_B, EPI_ROPE_B_SCALED, EPI_PLAIN = 0, 1, 2, 3
N_EPI = 4
_SEC_EPI = (EPI_ROPE_A, EPI_PLAIN, EPI_ROPE_B, EPI_ROPE_B_SCALED, EPI_PLAIN, EPI_PLAIN)


KV_ROPE_GROUPS = N_KV_A
N_W_BLOCKS = IN_TILE // MXU_WIDTH


def _tile_epi(tile, hi):
    sec = tile // TILES_PER_SEC
    t = jnp.int32(EPI_PLAIN if hi else EPI_ROPE_A)
    for s, e in enumerate(_SEC_EPI):
        t = jnp.where(sec == s, e, t)
    return t


def _in_proj_kernel(x_ref, nw_ref, *refs, row_chunk):
    w_refs = refs[:N_W_BLOCKS]
    wkv_ref, ca_ref, sa_ref, cb_ref, sb_ref, o_ref, tail_ref, h_ref = refs[N_W_BLOCKS:]
    j = pl.program_id(1)
    tm = x_ref.shape[0]
    n_chunks = tm // row_chunk

    @pl.when(j == 0)
    def _():
        def body(r, carry):
            rows = pl.ds(pl.multiple_of(r * row_chunk, row_chunk), row_chunk)
            x = x_ref[rows, :]
            ms = jnp.mean(x * x, axis=-1, keepdims=True)
            h_ref[rows, :] = (x * lax.rsqrt(ms + EPS) * nw_ref[...]).astype(BF16)
            return carry
        lax.fori_loop(0, n_chunks, body, 0)

    lane = lax.broadcasted_iota(jnp.int32, (tm, LANES), 1)
    first_half = (lane % HD_A) < HD_A // 2

    def rotate(x, c, s, shift_up, shift_dn):
        partner = jnp.where(first_half, pltpu.roll(x, shift_up, 1), pltpu.roll(x, shift_dn, 1))
        return x * c + partner * s

    epi = _tile_epi(j, 0)
    is_plain = epi == EPI_PLAIN

    @pl.when((j < KV_TILE) & is_plain)
    def _():
        for e, w_ref in enumerate(w_refs):
            cols = slice(e * MXU_WIDTH, (e + 1) * MXU_WIDTH)
            o_ref[:, cols] = jnp.dot(h_ref[...], w_ref[...], preferred_element_type=F32).astype(o_ref.dtype)

    @pl.when((j < KV_TILE) & jnp.logical_not(is_plain))
    def _():
        is_a = epi == EPI_ROPE_A
        scale = jnp.where(epi == EPI_ROPE_B_SCALED, HD_B ** -0.5, 1.0).astype(F32)
        c = jnp.where(is_a, ca_ref[...], cb_ref[...]) * scale
        s = jnp.where(is_a, sa_ref[...], sb_ref[...]) * scale
        shift_up = jnp.where(is_a, LANES - HD_A // 2, HD_B // 2)
        shift_dn = jnp.where(is_a, HD_A // 2, HD_B // 2)
        for e, w_ref in enumerate(w_refs):
            acc = jnp.dot(h_ref[...], w_ref[...], preferred_element_type=F32)
            for gg in range(MXU_WIDTH // LANES):
                cols = slice(e * MXU_WIDTH + gg * LANES, e * MXU_WIDTH + (gg + 1) * LANES)
                out = rotate(acc[:, gg * LANES:(gg + 1) * LANES], c, s, shift_up, shift_dn)
                o_ref[:, cols] = out.astype(o_ref.dtype)

    @pl.when(j == KV_TILE)
    def _():
        acc = jnp.dot(h_ref[...], wkv_ref[...], preferred_element_type=F32)
        for g in range(COL_TILE // LANES):
            cols = slice(g * LANES, (g + 1) * LANES)
            out = acc[:, cols]
            if g < KV_ROPE_GROUPS:
                out = rotate(out, ca_ref[...], sa_ref[...], LANES - HD_A // 2, HD_A // 2)
            o_ref[:, cols] = out.astype(o_ref.dtype)
            tail_ref[:, cols] = out[tm - WINDOW:, :]
        o_ref[:, COL_TILE:] = jnp.zeros((tm, IN_TILE - COL_TILE), o_ref.dtype)


def _in_proj(x, norm_w, w_all, wkv_all, d, tables, out_dtype, tm):
    m = x.shape[0]
    n_tiles = D_PROJ_R // IN_TILE
    n_src_blocks = w_all.shape[2] // MXU_WIDTH

    def src_block(j, e):
        sec = j // TILES_PER_SEC
        blk = sec * (W_A // MXU_WIDTH) + jnp.minimum(sec, 1) + (j % TILES_PER_SEC) * N_W_BLOCKS + e
        return jnp.minimum(blk, n_src_blocks - 1)

    def w_spec(e):
        return pl.BlockSpec((None, D_MODEL, MXU_WIDTH), lambda i, j: (d, 0, src_block(j, e)))

    tab_spec = pl.BlockSpec((tm, LANES), lambda i, j: (i, 0))

    return pl.pallas_call(
        functools.partial(_in_proj_kernel, row_chunk=min(tm, 256)),
        grid=(m // tm, n_tiles),
        in_specs=[
            pl.BlockSpec((tm, D_MODEL), lambda i, j: (i, 0)),
            pl.BlockSpec((1, D_MODEL), lambda i, j: (0, 0)),
        ] + [w_spec(e) for e in range(N_W_BLOCKS)] + [
            pl.BlockSpec((None, D_MODEL, COL_TILE), lambda i, j: (d, 0, 0)),
            tab_spec, tab_spec, tab_spec, tab_spec,
        ],
        out_specs=[
            pl.BlockSpec((tm, IN_TILE), lambda i, j: (i, j)),
            pl.BlockSpec((WINDOW, COL_TILE), lambda i, j: (0, 0)),
        ],
        out_shape=[
            jax.ShapeDtypeStruct((m, D_PROJ_R), out_dtype),
            jax.ShapeDtypeStruct((WINDOW, COL_TILE), F32),
        ],
        scratch_shapes=[pltpu.VMEM((tm, D_MODEL), BF16)],
        compiler_params=pltpu.CompilerParams(
            dimension_semantics=("arbitrary", "arbitrary"),
            vmem_limit_bytes=VMEM_LIMIT),
        name="in_proj",
    )(x, norm_w, *([w_all] * N_W_BLOCKS), wkv_all, *tables)


def _out_proj_kernel(y_ref, x_ref, w_ref, fw_ref, o_ref, *, final):
    acc = x_ref[...] + jnp.dot(y_ref[...], w_ref[...], preferred_element_type=F32)
    if final:
        ms = jnp.mean(acc * acc, axis=-1, keepdims=True)
        acc = acc * lax.rsqrt(ms + EPS) * fw_ref[...]
    o_ref[...] = acc


def _out_proj(y, x, w_all, d, final_w, final, tm):
    m = x.shape[0]
    return pl.pallas_call(
        functools.partial(_out_proj_kernel, final=final),
        grid=(m // tm,),
        in_specs=[
            pl.BlockSpec((tm, D_MODEL), lambda i: (i, 0)),
            pl.BlockSpec((tm, D_MODEL), lambda i: (i, 0)),
            pl.BlockSpec((None, D_MODEL, D_MODEL), lambda i: (d, 0, 0)),
            pl.BlockSpec((1, D_MODEL), lambda i: (0, 0)),
        ],
        out_specs=pl.BlockSpec((tm, D_MODEL), lambda i: (i, 0)),
        out_shape=jax.ShapeDtypeStruct((m, D_MODEL), F32),
        compiler_params=pltpu.CompilerParams(
            dimension_semantics=("arbitrary",),
            vmem_limit_bytes=VMEM_LIMIT),
        name="out_proj",
    )(y, x, w_all, final_w)


def _nt_dot(a, b):
    return lax.dot_general(a, b, (((1,), (1,)), ((), ())), preferred_element_type=F32)


def _tn_dot(a, b):
    return lax.dot_general(a, b, (((0,), (0,)), ((), ())), preferred_element_type=F32)


MIX_CHUNKS = 4


def _mix_prompt_kernel(sink_ref, qa_ref, ga_ref, qb_ref, kb_ref, vb_ref, gb_ref,
                       kv_ref, gnw_ref, y_ref, r_ref,
                       dmask_ref, qdec_ref, kdec_ref, kvp_ref,
                       qs_ref, s_ref, p_ref, o_ref, scb_ref, inter_ref):
    step = pl.program_id(0)
    C = RET_CHUNK
    row = lax.broadcasted_iota(jnp.int32, (C, LANES), 0)
    lane = lax.broadcasted_iota(jnp.int32, (C, LANES), 1)

    @pl.when(step == 0)
    def _():
        r_ref[...] = jnp.zeros(r_ref.shape, F32)
        kvp_ref[...] = jnp.zeros(kvp_ref.shape, kvp_ref.dtype)
        rowf = row.astype(F32)
        diff = rowf - lane.astype(F32)
        for h in range(N_HEADS_B):
            dmask_ref[h] = jnp.where(diff >= 0, jnp.exp(LOG_G[h] * jnp.maximum(diff, 0.0)), 0.0)
            qdec_ref[h] = jnp.exp(LOG_G[h] * (rowf + 1.0))
            kdec_ref[h] = jnp.exp(LOG_G[h] * (C - 1.0 - rowf))

    tri = lane <= row
    lo = lane < HD_A
    neg_inf = jnp.float32(-jnp.inf)
    half_group = GROUP_A // 2
    kcol = lambda g: slice(g * LANES, (g + 1) * LANES)
    vcol = lambda g: slice((N_KV_A + g) * LANES, (N_KV_A + g + 1) * LANES)
    pair_col = lambda g, pi: slice((g * half_group + pi) * LANES, (g * half_group + pi + 1) * LANES)
    head_rows = lambda hh: slice(hh * C, (hh + 1) * C)

    def chunk(cc, carry):
        rows = pl.ds(pl.multiple_of(cc * C, C), C)
        has_prev = step * MIX_CHUNKS + cc > 0

        for g in range(N_KV_A):
            for pi in range(half_group):
                q_pair = qa_ref[rows, pair_col(g, pi)].astype(F32)
                qs_ref[g, head_rows(2 * pi), :] = jnp.where(lo, q_pair, 0.0).astype(BF16)
                qs_ref[g, head_rows(2 * pi + 1), :] = jnp.where(lo, 0.0, q_pair).astype(BF16)
            qs = qs_ref[g]
            s_ref[g, 0] = _nt_dot(qs, kv_ref[rows, kcol(g)])
            s_ref[g, 1] = _nt_dot(qs, kvp_ref[:, kcol(g)])

        for h in range(N_HEADS_B):
            hc = slice(h * HD_B, (h + 1) * HD_B)
            q, k, v = qb_ref[rows, hc], kb_ref[rows, hc], vb_ref[rows, hc]
            scb_ref[h] = (_nt_dot(q, k) * dmask_ref[h]).astype(BF16)
            r = r_ref[h]
            inter_ref[h] = jnp.dot(q, r.astype(BF16), preferred_element_type=F32) * qdec_ref[h]
            kd = (k.astype(F32) * kdec_ref[h]).astype(BF16)
            r_ref[h] = math.exp(LOG_G[h] * C) * r + _tn_dot(kd, v)

        for g in range(N_KV_A):
            for hh in range(GROUP_A):
                hr = head_rows(hh)
                s_prev = jnp.where(has_prev, s_ref[g, 1, hr, :], neg_inf)
                s = jnp.where(tri, s_ref[g, 0, hr, :], s_prev) * (HD_A ** -0.5)
                sink = sink_ref[g * GROUP_A + hh]
                m = jnp.maximum(jnp.max(s, axis=-1, keepdims=True), sink)
                e = jnp.exp(s - m)
                denom = jnp.sum(e, axis=-1, keepdims=True) + jnp.exp(sink - m)
                p = e / denom
                p_ref[g, 0, hr, :] = jnp.where(tri, p, 0.0).astype(BF16)
                p_ref[g, 1, hr, :] = jnp.where(tri, 0.0, p).astype(BF16)

        for g in range(N_KV_A):
            o_ref[g] = (jnp.dot(p_ref[g, 0], kv_ref[rows, vcol(g)], preferred_element_type=F32)
                        + jnp.dot(p_ref[g, 1], kvp_ref[:, vcol(g)], preferred_element_type=F32))

        for h in range(N_HEADS_B):
            hc = slice(h * HD_B, (h + 1) * HD_B)
            o = jnp.dot(scb_ref[h], vb_ref[rows, hc], preferred_element_type=F32) + inter_ref[h]
            mu = jnp.mean(o, axis=-1, keepdims=True)
            d = o - mu
            var = jnp.mean(d * d, axis=-1, keepdims=True)
            yh = d * lax.rsqrt(var + EPS) * gnw_ref[:, hc] * _silu(gb_ref[rows, hc].astype(F32))
            y_ref[rows, W_A + h * HD_B: W_A + (h + 1) * HD_B] = yh.astype(y_ref.dtype)

        for g in range(N_KV_A):
            for pi in range(half_group):
                pc = pair_col(g, pi)
                o_pair = jnp.where(lo, o_ref[g, head_rows(2 * pi), :], o_ref[g, head_rows(2 * pi + 1), :])
                y_ref[rows, pc] = (o_pair * _silu(ga_ref[rows, pc].astype(F32))).astype(y_ref.dtype)

        kvp_ref[...] = kv_ref[rows, :]
        return carry

    lax.fori_loop(0, MIX_CHUNKS, chunk, 0)


def _mix_prompt(p, sinks, gn_w):
    t = p.shape[0]
    C = RET_CHUNK
    rows = MIX_CHUNKS * C
    sec_spec = lambda s: pl.BlockSpec((rows, W_A), lambda c, s=s: (c, s))
    return pl.pallas_call(
        _mix_prompt_kernel,
        grid=(t // rows,),
        in_specs=[
            pl.BlockSpec(memory_space=pltpu.SMEM),
            sec_spec(SEC_QA), sec_spec(SEC_GA), sec_spec(SEC_QB),
            sec_spec(SEC_KB), sec_spec(SEC_VB), sec_spec(SEC_GB),
            pl.BlockSpec((rows, COL_TILE), lambda c: (c, KV_BLOCK)),
            pl.BlockSpec((1, W_B), lambda c: (0, 0)),
        ],
        out_specs=[
            pl.BlockSpec((rows, D_MODEL), lambda c: (c, 0)),
            pl.BlockSpec((N_HEADS_B, HD_B, HD_B), lambda c: (0, 0, 0)),
        ],
        out_shape=[
            jax.ShapeDtypeStruct((t, D_MODEL), BF16),
            jax.ShapeDtypeStruct((N_HEADS_B, HD_B, HD_B), F32),
        ],
        scratch_shapes=[pltpu.VMEM((N_HEADS_B, C, LANES), F32)] * 3 + [
            pltpu.VMEM((C, COL_TILE), BF16),
            pltpu.VMEM((N_KV_A, GROUP_A * C, LANES), BF16),
            pltpu.VMEM((N_KV_A, 2, GROUP_A * C, LANES), F32),
            pltpu.VMEM((N_KV_A, 2, GROUP_A * C, LANES), BF16),
            pltpu.VMEM((N_KV_A, GROUP_A * C, LANES), F32),
            pltpu.VMEM((N_HEADS_B, C, HD_B), BF16),
            pltpu.VMEM((N_HEADS_B, C, HD_B), F32),
        ],
        compiler_params=pltpu.CompilerParams(
            dimension_semantics=("arbitrary",),
            vmem_limit_bytes=VMEM_LIMIT),
        name="mix_prompt",
    )(sinks, p, p, p, p, p, p, p, gn_w)


SAMPLE_BB = 8
LANE_BATCH = LANES // 4


def _mix_sample_kernel(sinkrow_ref, q3_ref, ga3_ref, qb_ref, kb_ref, vb_ref, gb_ref, kvn_ref,
                       kbt_ref, vbg_ref, ck_ref, cv_ref, st_ref, gnw_ref, *rest, t_new):
    ya3_ref, yb_ref, nk_ref, nv_ref, nst_ref, s_ref, p_ref, vall_ref, kt_ref, x_ref, sc_ref = rest[-11:]
    i = pl.program_id(0)
    T = t_new
    BB = SAMPLE_BB
    L = ck_ref.shape[1]
    n_keys = L + T
    QR = T * GROUP_A
    R = BB * T
    lo = lax.broadcasted_iota(jnp.int32, (T, LANES), 1) < HD_A
    head_sl = lambda g: slice(g * GROUP_A, (g + 1) * GROUP_A)
    lane_sl = lambda g: slice(g * HD_A, (g + 1) * HD_A)
    tok_rows = lambda bb: slice(bb * T, (bb + 1) * T)


    for bb in range(BB):
        kvn = kvn_ref[tok_rows(bb), :]
        k_new = jnp.where(lo, kvn[:, 0:LANES], kvn[:, LANES:2 * LANES])
        v_new = jnp.where(lo, kvn[:, 2 * LANES:3 * LANES], kvn[:, 3 * LANES:4 * LANES])
        k_all = jnp.concatenate([ck_ref[bb], k_new], axis=0)
        v_all = jnp.concatenate([cv_ref[bb], v_new], axis=0)
        nk_ref[bb] = k_all[T:, :]
        nv_ref[bb] = v_all[T:, :]
        vall_ref[bb] = v_all.astype(BF16)
        k_all_b = k_all.astype(BF16)
        for g in range(N_KV_A):
            q = q3_ref[tok_rows(bb), head_sl(g), :].reshape(T * GROUP_A, HD_A).astype(BF16)
            s_ref[g, bb * QR:(bb + 1) * QR, :] = _nt_dot(q, k_all_b[:, lane_sl(g)])

    lane_t = (lax.broadcasted_iota(jnp.int32, (1, LANES), 1) % T).astype(F32)
    lane_b = lax.broadcasted_iota(jnp.int32, (HD_B, LANES), 1) // T
    slot0 = (i % (LANE_BATCH // BB)) * BB
    for h in range(N_HEADS_B):
        hc = slice(h * HD_B, (h + 1) * HD_B)
        lg = LOG_G[h]
        kdec = jnp.exp(lg * (T - 1.0 - lane_t))
        kt_dec = kbt_ref[hc, :] * kdec
        for bb in range(BB):
            kt_ref[h, bb * HD_B:(bb + 1) * HD_B, :] = jnp.where(lane_b == slot0 + bb, kt_dec, 0.0).astype(BF16)
        u = jnp.dot(kt_ref[h], vbg_ref[:, hc].astype(BF16), preferred_element_type=F32)
        for bb in range(BB):
            r = st_ref[bb, h]
            pr = slice((bb // 2) * 2 * T, (bb // 2 + 1) * 2 * T)
            x_ref[bb * N_HEADS_B + h] = jnp.dot(qb_ref[pr, hc].astype(BF16), r.astype(BF16),
                                                 preferred_element_type=F32)
            nst_ref[bb, h] = math.exp(lg * T) * r + u[bb * HD_B:(bb + 1) * HD_B, :]

    ri = lax.broadcasted_iota(jnp.int32, (R, R), 0)
    ci = lax.broadcasted_iota(jnp.int32, (R, R), 1)
    same_b = (ri // T) == (ci // T)
    dt = (ri % T - ci % T).astype(F32)
    keep = same_b & (dt >= 0)
    for h in range(N_HEADS_B):
        hc = slice(h * HD_B, (h + 1) * HD_B)
        dmask = jnp.where(keep, jnp.exp(LOG_G[h] * jnp.maximum(dt, 0.0)), 0.0)
        sc = _nt_dot(qb_ref[:, hc].astype(BF16), kb_ref[:, hc].astype(BF16)) * dmask
        sc_ref[h] = sc.astype(BF16)

    rows_q = (lax.broadcasted_iota(jnp.int32, (BB * QR, n_keys), 0) % QR) // GROUP_A
    key = lax.broadcasted_iota(jnp.int32, (BB * QR, n_keys), 1)
    delta = jnp.where(key < L, rows_q + L - key, rows_q - (key - L))
    valid = (delta >= 0) & (delta < WINDOW)
    for g in range(N_KV_A):
        s = jnp.where(valid, s_ref[g] * (HD_A ** -0.5), -jnp.inf)
        sink = sinkrow_ref[g]
        m = jnp.maximum(jnp.max(s, axis=-1, keepdims=True), sink)
        e = jnp.exp(s - m)
        denom = jnp.sum(e, axis=-1, keepdims=True) + jnp.exp(sink - m)
        p_ref[g] = (e / denom).astype(BF16)

    tposf = (lax.broadcasted_iota(jnp.int32, (R, 1), 0) % T).astype(F32)
    upper = (lax.broadcasted_iota(jnp.int32, (2 * T, 1), 0) < T)
    for h in range(N_HEADS_B):
        hc = slice(h * HD_B, (h + 1) * HD_B)
        intra = jnp.dot(sc_ref[h], vb_ref[:, hc].astype(BF16), preferred_element_type=F32)
        inter = jnp.concatenate(
            [jnp.where(upper, x_ref[(2 * pair) * N_HEADS_B + h], x_ref[(2 * pair + 1) * N_HEADS_B + h])
             for pair in range(BB // 2)], axis=0)
        o = intra + inter * jnp.exp(LOG_G[h] * (tposf + 1.0))
        mu = jnp.mean(o, axis=-1, keepdims=True)
        d = o - mu
        var = jnp.mean(d * d, axis=-1, keepdims=True)
        yb_ref[:, hc] = d * lax.rsqrt(var + EPS) * gnw_ref[:, hc] * _silu(gb_ref[:, hc])

    for bb in range(BB):
        for g in range(N_KV_A):
            o = jnp.dot(p_ref[g, bb * QR:(bb + 1) * QR, :], vall_ref[bb, :, lane_sl(g)],
                        preferred_element_type=F32)
            gate = _silu(ga3_ref[tok_rows(bb), head_sl(g), :].reshape(QR, HD_A))
            ya3_ref[tok_rows(bb), head_sl(g), :] = (o * gate).reshape(T, GROUP_A, HD_A)


def _mix_sample(ps, sinks, gn_w, cache_k, cache_v, state, d, prev_out, t_new):
    m = ps.shape[0]
    nb = m // t_new
    L = cache_k.shape[2]
    BB = SAMPLE_BB
    n_alias = len(prev_out)
    n_in = 14
    rows = BB * t_new
    qa = ps[:, SEC_QA * W_A:(SEC_QA + 1) * W_A].reshape(m, N_HEADS_A, HD_A)
    ga = ps[:, SEC_GA * W_A:(SEC_GA + 1) * W_A].reshape(m, N_HEADS_A, HD_A)
    kbt = ps[:, SEC_KB * W_A:(SEC_KB + 1) * W_A].T
    sinkrow = jnp.tile(sinks.reshape(N_KV_A, 1, GROUP_A), (1, BB * t_new, 1)).reshape(N_KV_A, -1, 1)
    sec_spec = lambda s: pl.BlockSpec((rows, W_A), lambda i, s=s: (i, s))
    steps_per_lane_group = LANE_BATCH // BB
    return pl.pallas_call(
        functools.partial(_mix_sample_kernel, t_new=t_new),
        grid=(nb // BB,),
        in_specs=[
            pl.BlockSpec((N_KV_A, BB * t_new * GROUP_A, 1), lambda i: (0, 0, 0)),
            pl.BlockSpec((rows, N_HEADS_A, HD_A), lambda i: (i, 0, 0)),
            pl.BlockSpec((rows, N_HEADS_A, HD_A), lambda i: (i, 0, 0)),
            sec_spec(SEC_QB), sec_spec(SEC_KB), sec_spec(SEC_VB), sec_spec(SEC_GB),
            pl.BlockSpec((rows, COL_TILE), lambda i: (i, KV_BLOCK)),
            pl.BlockSpec((W_B, LANES), lambda i: (0, i // steps_per_lane_group)),
            pl.BlockSpec((LANES, W_B), lambda i: (i // steps_per_lane_group, SEC_VB)),
            pl.BlockSpec((None, BB, L, LANES), lambda i: (d, i, 0, 0)),
            pl.BlockSpec((None, BB, L, LANES), lambda i: (d, i, 0, 0)),
            pl.BlockSpec((None, BB, N_HEADS_B, HD_B, HD_B), lambda i: (d, i, 0, 0, 0)),
            pl.BlockSpec((1, W_B), lambda i: (0, 0)),
        ] + [pl.BlockSpec(memory_space=pl.ANY)] * n_alias,
        out_specs=[
            pl.BlockSpec((rows, N_HEADS_A, HD_A), lambda i: (i, 0, 0)),
            pl.BlockSpec((rows, W_B), lambda i: (i, 0)),
            pl.BlockSpec((None, BB, L, LANES), lambda i: (d, i, 0, 0)),
            pl.BlockSpec((None, BB, L, LANES), lambda i: (d, i, 0, 0)),
            pl.BlockSpec((None, BB, N_HEADS_B, HD_B, HD_B), lambda i: (d, i, 0, 0, 0)),
        ],
        out_shape=[
            jax.ShapeDtypeStruct((m, N_HEADS_A, HD_A), F32),
            jax.ShapeDtypeStruct((m, W_B), F32),
            jax.ShapeDtypeStruct(cache_k.shape, F32),
            jax.ShapeDtypeStruct(cache_v.shape, F32),
            jax.ShapeDtypeStruct(state.shape, F32),
        ],
        input_output_aliases={n_in + a: 2 + a for a in range(n_alias)},
        scratch_shapes=[
            pltpu.VMEM((N_KV_A, BB * t_new * GROUP_A, L + t_new), F32),
            pltpu.VMEM((N_KV_A, BB * t_new * GROUP_A, L + t_new), BF16),
            pltpu.VMEM((BB, L + t_new, LANES), BF16),
            pltpu.VMEM((N_HEADS_B, BB * HD_B, LANES), BF16),
            pltpu.VMEM((BB * N_HEADS_B, 2 * t_new, HD_B), F32),
            pltpu.VMEM((N_HEADS_B, rows, rows), BF16),
        ],
        compiler_params=pltpu.CompilerParams(
            dimension_semantics=("arbitrary",),
            vmem_limit_bytes=VMEM_LIMIT),
        name="mix_sample",
    )(sinkrow, qa, ga, ps, ps, ps, ps, ps, kbt, ps, cache_k, cache_v, state, gn_w, *prev_out)


def _prep_w_kv(w_in):
    kv_width = N_KV_A * HD_A
    ka = w_in[..., W_A:W_A + kv_width]
    va = w_in[..., W_A + kv_width:W_A + 2 * kv_width]
    dup = lambda a: [a[..., g * HD_A:(g + 1) * HD_A] for g in range(N_KV_A) for _ in range(2)]
    return jnp.concatenate(dup(ka) + dup(va), axis=-1).astype(BF16)


def _rope_tables(pos):
    def tab(hd):
        half = hd // 2
        inv = ROPE_THETA ** (-jnp.arange(half, dtype=F32) / half)
        ang = pos.astype(F32)[:, None] * inv[None, :]
        cos, sin = jnp.cos(ang), jnp.sin(ang)
        reps = LANES // hd
        return (jnp.tile(jnp.concatenate([cos, cos], axis=1), (1, reps)),
                jnp.tile(jnp.concatenate([-sin, sin], axis=1), (1, reps)))
    return tab(HD_A) + tab(HD_B)


def kernel(x_prompt, x_sample, cache_k_win, cache_v_win, state_ret, norm_w, w_in, attn_sinks,
           ret_norm_w, w_out, final_norm_w):
    bp, tp, _ = x_prompt.shape
    bs, ts, _ = x_sample.shape
    assert bp == 1 and tp % RET_CHUNK == 0 and min(WINDOW, tp) == WINDOW
    L = cache_k_win.shape[2]
    depth = w_in.shape[0]

    w_in_b = w_in.astype(BF16)
    w_kv = _prep_w_kv(w_in)
    w_out_b = w_out.astype(BF16)
    tab_p = _rope_tables(jnp.arange(tp, dtype=jnp.int32))
    tab_s = _rope_tables(jnp.tile(PAST_LEN + jnp.arange(ts, dtype=jnp.int32), bs))
    fw = final_norm_w.reshape(1, D_MODEL)

    xp = x_prompt.reshape(tp, D_MODEL)
    xs = x_sample.reshape(bs * ts, D_MODEL)
    ck = cache_k_win.reshape(depth, bs, L, N_KV_A * HD_A)
    cv = cache_v_win.reshape(depth, bs, L, N_KV_A * HD_A)

    kp_l, vp_l, rp_l = [], [], []
    sample_out = ()
    for d in range(depth):
        final = d == depth - 1
        nw = norm_w[d].reshape(1, D_MODEL)
        gnw = ret_norm_w[d].reshape(1, W_B)
        p, tail = _in_proj(xp, nw, w_in_b, w_kv, d, tab_p, BF16, tm=1024)
        y, r_fin = _mix_prompt(p, attn_sinks[d], gnw)
        xp = _out_proj(y, xp, w_out_b, d, fw, final, tm=512)
        kp_l.append(jnp.stack([tail[:, 0:HD_A], tail[:, LANES:LANES + HD_A]], axis=1))
        vp_l.append(jnp.stack([tail[:, 2 * LANES:2 * LANES + HD_A], tail[:, 3 * LANES:3 * LANES + HD_A]], axis=1))
        rp_l.append(r_fin)
        ps, _ = _in_proj(xs, nw, w_in_b, w_kv, d, tab_s, F32, tm=bs * ts)
        ya3, yb, *sample_out = _mix_sample(ps, attn_sinks[d], gnw, ck, cv, state_ret, d, sample_out, ts)
        ys = jnp.concatenate([ya3.reshape(bs * ts, W_A), yb], axis=1).astype(BF16)
        xs = _out_proj(ys, xs, w_out_b, d, fw, final, tm=bs * ts)

    keep = min(WINDOW, tp)
    nk, nv, nst = sample_out
    return (xp.reshape(bp, tp, D_MODEL), xs.reshape(bs, ts, D_MODEL),
            jnp.stack(kp_l).reshape(depth, bp, keep, N_KV_A, HD_A),
            jnp.stack(vp_l).reshape(depth, bp, keep, N_KV_A, HD_A),
            jnp.stack(rp_l).reshape(depth, bp, N_HEADS_B, HD_B, HD_B),
            nk.reshape(depth, bs, L, N_KV_A, HD_A), nv.reshape(depth, bs, L, N_KV_A, HD_A), nst)
```

```python
import functools
import math

import jax
import jax.numpy as jnp
from jax import lax
from jax.experimental import pallas as pl
from jax.experimental.pallas import tpu as pltpu

F32 = jnp.float32
BF16 = jnp.bfloat16

D_MODEL = 2048
DEPTH = 2
PAST_LEN = 8192
WINDOW = 128
HD_A = 64
N_HEADS_A = 16
N_KV_A = 2
GROUP_A = N_HEADS_A // N_KV_A
W_A = N_HEADS_A * HD_A
HD_B = 128
N_HEADS_B = 8
W_B = N_HEADS_B * HD_B
RET_CHUNK = 128
ROPE_THETA = 10000.0
EPS = 1e-6

LANES = 128
MXU_WIDTH = 256
SEC_QA, SEC_GA, SEC_QB, SEC_KB, SEC_VB, SEC_GB = 0, 1, 2, 3, 4, 5
IN_TILE = 1024
TILES_PER_SEC = W_A // IN_TILE
KV_TILE = 6 * TILES_PER_SEC
COL_TILE = 4 * N_KV_A * HD_A
KV_BLOCK = KV_TILE * IN_TILE // COL_TILE
D_PROJ_R = (KV_TILE + 1) * IN_TILE
VMEM_LIMIT = 56 * 1024 * 1024

LOG_G = [math.log1p(-(2.0 ** (-5.0 - h))) for h in range(N_HEADS_B)]


def _silu(g):
    return g * (1.0 / (1.0 + jnp.exp(-g)))


EPI_ROPE_A, EPI_ROPE_B, EPI_ROPE_B_SCALED, EPI_PLAIN = 0, 1, 2, 3
N_EPI = 4
_SEC_EPI = (EPI_ROPE_A, EPI_PLAIN, EPI_ROPE_B, EPI_ROPE_B_SCALED, EPI_PLAIN, EPI_PLAIN)


KV_ROPE_GROUPS = N_KV_A
N_W_BLOCKS = IN_TILE // MXU_WIDTH


def _tile_epi(tile, hi):
    sec = tile // TILES_PER_SEC
    t = jnp.int32(EPI_PLAIN if hi else EPI_ROPE_A)
    for s, e in enumerate(_SEC_EPI):
        t = jnp.where(sec == s, e, t)
    return t


def _in_proj_kernel(x_ref, nw_ref, *refs, row_chunk):
    w_refs = refs[:N_W_BLOCKS]
    wkv_ref, ca_ref, sa_ref, cb_ref, sb_ref, o_ref, tail_ref, h_ref = refs[N_W_BLOCKS:]
    j = pl.program_id(1)
    tm = x_ref.shape[0]
    n_chunks = tm // row_chunk

    @pl.when(j == 0)
    def _():
        def body(r, carry):
            rows = pl.ds(pl.multiple_of(r * row_chunk, row_chunk), row_chunk)
            x = x_ref[rows, :]
            ms = jnp.mean(x * x, axis=-1, keepdims=True)
            h_ref[rows, :] = (x * lax.rsqrt(ms + EPS) * nw_ref[...]).astype(BF16)
            return carry
        lax.fori_loop(0, n_chunks, body, 0)

    lane = lax.broadcasted_iota(jnp.int32, (tm, LANES), 1)
    first_half = (lane % HD_A) < HD_A // 2

    def rotate(x, c, s, shift_up, shift_dn):
        partner = jnp.where(first_half, pltpu.roll(x, shift_up, 1), pltpu.roll(x, shift_dn, 1))
        return x * c + partner * s

    epi = _tile_epi(j, 0)
    is_plain = epi == EPI_PLAIN

    @pl.when((j < KV_TILE) & is_plain)
    def _():
        for e, w_ref in enumerate(w_refs):
            cols = slice(e * MXU_WIDTH, (e + 1) * MXU_WIDTH)
            o_ref[:, cols] = jnp.dot(h_ref[...], w_ref[...], preferred_element_type=F32).astype(o_ref.dtype)

    @pl.when((j < KV_TILE) & jnp.logical_not(is_plain))
    def _():
        is_a = epi == EPI_ROPE_A
        scale = jnp.where(epi == EPI_ROPE_B_SCALED, HD_B ** -0.5, 1.0).astype(F32)
        c = jnp.where(is_a, ca_ref[...], cb_ref[...]) * scale
        s = jnp.where(is_a, sa_ref[...], sb_ref[...]) * scale
        shift_up = jnp.where(is_a, LANES - HD_A // 2, HD_B // 2)
        shift_dn = jnp.where(is_a, HD_A // 2, HD_B // 2)
        for e, w_ref in enumerate(w_refs):
            acc = jnp.dot(h_ref[...], w_ref[...], preferred_element_type=F32)
            for gg in range(MXU_WIDTH // LANES):
                cols = slice(e * MXU_WIDTH + gg * LANES, e * MXU_WIDTH + (gg + 1) * LANES)
                out = rotate(acc[:, gg * LANES:(gg + 1) * LANES], c, s, shift_up, shift_dn)
                o_ref[:, cols] = out.astype(o_ref.dtype)

    @pl.when(j == KV_TILE)
    def _():
        acc = jnp.dot(h_ref[...], wkv_ref[...], preferred_element_type=F32)
        for g in range(COL_TILE // LANES):
            cols = slice(g * LANES, (g + 1) * LANES)
            out = acc[:, cols]
            if g < KV_ROPE_GROUPS:
                out = rotate(out, ca_ref[...], sa_ref[...], LANES - HD_A // 2, HD_A // 2)
            o_ref[:, cols] = out.astype(o_ref.dtype)
            tail_ref[:, cols] = out[tm - WINDOW:, :]
        o_ref[:, COL_TILE:] = jnp.zeros((tm, IN_TILE - COL_TILE), o_ref.dtype)


def _in_proj(x, norm_w, w_all, wkv_all, d, tables, out_dtype, tm):
    m = x.shape[0]
    n_tiles = D_PROJ_R // IN_TILE
    n_src_blocks = w_all.shape[2] // MXU_WIDTH

    def src_block(j, e):
        sec = j // TILES_PER_SEC
        blk = sec * (W_A // MXU_WIDTH) + jnp.minimum(sec, 1) + (j % TILES_PER_SEC) * N_W_BLOCKS + e
        return jnp.minimum(blk, n_src_blocks - 1)

    def w_spec(e):
        return pl.BlockSpec((None, D_MODEL, MXU_WIDTH), lambda i, j: (d, 0, src_block(j, e)))

    tab_spec = pl.BlockSpec((tm, LANES), lambda i, j: (i, 0))

    return pl.pallas_call(
        functools.partial(_in_proj_kernel, row_chunk=min(tm, 256)),
        grid=(m // tm, n_tiles),
        in_specs=[
            pl.BlockSpec((tm, D_MODEL), lambda i, j: (i, 0)),
            pl.BlockSpec((1, D_MODEL), lambda i, j: (0, 0)),
        ] + [w_spec(e) for e in range(N_W_BLOCKS)] + [
            pl.BlockSpec((None, D_MODEL, COL_TILE), lambda i, j: (d, 0, 0)),
            tab_spec, tab_spec, tab_spec, tab_spec,
        ],
        out_specs=[
            pl.BlockSpec((tm, IN_TILE), lambda i, j: (i, j)),
            pl.BlockSpec((WINDOW, COL_TILE), lambda i, j: (0, 0)),
        ],
        out_shape=[
            jax.ShapeDtypeStruct((m, D_PROJ_R), out_dtype),
            jax.ShapeDtypeStruct((WINDOW, COL_TILE), F32),
        ],
        scratch_shapes=[pltpu.VMEM((tm, D_MODEL), BF16)],
        compiler_params=pltpu.CompilerParams(
            dimension_semantics=("arbitrary", "arbitrary"),
            vmem_limit_bytes=VMEM_LIMIT),
        name="in_proj",
    )(x, norm_w, *([w_all] * N_W_BLOCKS), wkv_all, *tables)


def _out_proj_kernel(y_ref, x_ref, w_ref, fw_ref, o_ref, *, final):
    acc = x_ref[...] + jnp.dot(y_ref[...], w_ref[...], preferred_element_type=F32)
    if final:
        ms = jnp.mean(acc * acc, axis=-1, keepdims=True)
        acc = acc * lax.rsqrt(ms + EPS) * fw_ref[...]
    o_ref[...] = acc


def _out_proj(y, x, w_all, d, final_w, final, tm):
    m = x.shape[0]
    return pl.pallas_call(
        functools.partial(_out_proj_kernel, final=final),
        grid=(m // tm,),
        in_specs=[
            pl.BlockSpec((tm, D_MODEL), lambda i: (i, 0)),
            pl.BlockSpec((tm, D_MODEL), lambda i: (i, 0)),
            pl.BlockSpec((None, D_MODEL, D_MODEL), lambda i: (d, 0, 0)),
            pl.BlockSpec((1, D_MODEL), lambda i: (0, 0)),
        ],
        out_specs=pl.BlockSpec((tm, D_MODEL), lambda i: (i, 0)),
        out_shape=jax.ShapeDtypeStruct((m, D_MODEL), F32),
        compiler_params=pltpu.CompilerParams(
            dimension_semantics=("arbitrary",),
            vmem_limit_bytes=VMEM_LIMIT),
        name="out_proj",
    )(y, x, w_all, final_w)


def _nt_dot(a, b):
    return lax.dot_general(a, b, (((1,), (1,)), ((), ())), preferred_element_type=F32)


def _tn_dot(a, b):
    return lax.dot_general(a, b, (((0,), (0,)), ((), ())), preferred_element_type=F32)


MIX_CHUNKS = 4
OUT_PIECES = D_MODEL // MXU_WIDTH
PIECES_PER_CHUNK = OUT_PIECES // MIX_CHUNKS


def _mix_prompt_kernel(sink_ref, qa_ref, ga_ref, qb_ref, kb_ref, vb_ref, gb_ref,
                       kv_ref, gnw_ref, x_ref, wo_ref, fw_ref, xo_ref, r_ref,
                       dmask_ref, qdec_ref, kdec_ref, kvp_ref,
                       qs_ref, s_ref, p_ref, o_ref, scb_ref, inter_ref, y_ref, *, final):
    step = pl.program_id(0)
    last_step = pl.num_programs(0) - 1
    slot = step % 2
    prev_slot = 1 - slot
    C = RET_CHUNK
    row = lax.broadcasted_iota(jnp.int32, (C, LANES), 0)
    lane = lax.broadcasted_iota(jnp.int32, (C, LANES), 1)

    @pl.when(step == 0)
    def _():
        r_ref[...] = jnp.zeros(r_ref.shape, F32)
        kvp_ref[...] = jnp.zeros(kvp_ref.shape, kvp_ref.dtype)
        y_ref[1] = jnp.zeros(y_ref.shape[1:], y_ref.dtype)
        rowf = row.astype(F32)
        diff = rowf - lane.astype(F32)
        for h in range(N_HEADS_B):
            dmask_ref[h] = jnp.where(diff >= 0, jnp.exp(LOG_G[h] * jnp.maximum(diff, 0.0)), 0.0)
            qdec_ref[h] = jnp.exp(LOG_G[h] * (rowf + 1.0))
            kdec_ref[h] = jnp.exp(LOG_G[h] * (C - 1.0 - rowf))

    tri = lane <= row
    lo = lane < HD_A
    neg_inf = jnp.float32(-jnp.inf)
    half_group = GROUP_A // 2
    kcol = lambda g: slice(g * LANES, (g + 1) * LANES)
    vcol = lambda g: slice((N_KV_A + g) * LANES, (N_KV_A + g + 1) * LANES)
    pair_col = lambda g, pi: slice((g * half_group + pi) * LANES, (g * half_group + pi + 1) * LANES)
    head_rows = lambda hh: slice(hh * C, (hh + 1) * C)

    def out_piece(e):
        cols = slice(e * MXU_WIDTH, (e + 1) * MXU_WIDTH)
        xo_ref[:, cols] = x_ref[:, cols] + jnp.dot(y_ref[prev_slot], wo_ref[:, cols],
                                                   preferred_element_type=F32)

    def finish_out():
        if final:
            acc = xo_ref[...]
            ms = jnp.mean(acc * acc, axis=-1, keepdims=True)
            xo_ref[...] = acc * lax.rsqrt(ms + EPS) * fw_ref[...]

    def chunk(cc):
        rows = slice(cc * C, (cc + 1) * C)
        has_prev = step * MIX_CHUNKS + cc > 0
        if cc == 0:
            kvp_ref[C:, :] = kv_ref[rows, :]
            kv2 = lambda cols: kvp_ref[:, cols]
        else:
            kv2 = lambda cols: kv_ref[(cc - 1) * C:(cc + 1) * C, cols]

        for g in range(N_KV_A):
            for pi in range(half_group):
                q_pair = qa_ref[rows, pair_col(g, pi)].astype(F32)
                qs_ref[g, head_rows(2 * pi), :] = jnp.where(lo, q_pair, 0.0).astype(BF16)
                qs_ref[g, head_rows(2 * pi + 1), :] = jnp.where(lo, 0.0, q_pair).astype(BF16)
            s_ref[g] = _nt_dot(qs_ref[g], kv2(kcol(g)))

        out_piece(cc * PIECES_PER_CHUNK)

        for h in range(N_HEADS_B):
            hc = slice(h * HD_B, (h + 1) * HD_B)
            q, k, v = qb_ref[rows, hc], kb_ref[rows, hc], vb_ref[rows, hc]
            scb_ref[h] = (_nt_dot(q, k) * dmask_ref[h]).astype(BF16)
            r = r_ref[h]
            inter_ref[h] = jnp.dot(q, r.astype(BF16), preferred_element_type=F32) * qdec_ref[h]
            kd = (k.astype(F32) * kdec_ref[h]).astype(BF16)
            r_ref[h] = math.exp(LOG_G[h] * C) * r + _tn_dot(kd, v)

        for g in range(N_KV_A):
            for hh in range(GROUP_A):
                hr = head_rows(hh)
                s_prev = jnp.where(has_prev, s_ref[g, hr, :LANES], neg_inf)
                s = jnp.where(tri, s_ref[g, hr, LANES:], s_prev) * (HD_A ** -0.5)
                sink = sink_ref[g * GROUP_A + hh]
                m = jnp.maximum(jnp.max(s, axis=-1, keepdims=True), sink)
                e = jnp.exp(s - m)
                denom = jnp.sum(e, axis=-1, keepdims=True) + jnp.exp(sink - m)
                p = e / denom
                p_ref[g, hr, :LANES] = jnp.where(tri, 0.0, p).astype(BF16)
                p_ref[g, hr, LANES:] = jnp.where(tri, p, 0.0).astype(BF16)

        for g in range(N_KV_A):
            o_ref[g] = jnp.dot(p_ref[g], kv2(vcol(g)), preferred_element_type=F32)

        for e in range(cc * PIECES_PER_CHUNK + 1, (cc + 1) * PIECES_PER_CHUNK):
            out_piece(e)

        for h in range(N_HEADS_B):
            hc = slice(h * HD_B, (h + 1) * HD_B)
            o = jnp.dot(scb_ref[h], vb_ref[rows, hc], preferred_element_type=F32) + inter_ref[h]
            mu = jnp.mean(o, axis=-1, keepdims=True)
            d = o - mu
            var = jnp.mean(d * d, axis=-1, keepdims=True)
            yh = d * lax.rsqrt(var + EPS) * gnw_ref[:, hc] * _silu(gb_ref[rows, hc].astype(F32))
            y_ref[slot, rows, W_A + h * HD_B: W_A + (h + 1) * HD_B] = yh.astype(y_ref.dtype)

        for g in range(N_KV_A):
            for pi in range(half_group):
                pc = pair_col(g, pi)
                o_pair = jnp.where(lo, o_ref[g, head_rows(2 * pi), :], o_ref[g, head_rows(2 * pi + 1), :])
                y_ref[slot, rows, pc] = (o_pair * _silu(ga_ref[rows, pc].astype(F32))).astype(y_ref.dtype)

        if cc == MIX_CHUNKS - 1:
            kvp_ref[:C, :] = kv_ref[rows, :]

    @pl.when(step < last_step)
    def _():
        for cc in range(MIX_CHUNKS):
            chunk(cc)
        finish_out()

    @pl.when(step == last_step)
    def _():
        for e in range(OUT_PIECES):
            out_piece(e)
        finish_out()


def _mix_prompt(p, x, sinks, gn_w, w_out_all, d, final_w, final):
    t = p.shape[0]
    C = RET_CHUNK
    rows = MIX_CHUNKS * C
    n_blocks = t // rows
    cur = lambda c: jnp.minimum(c, n_blocks - 1)
    prev = lambda c: jnp.maximum(c - 1, 0)
    sec_spec = lambda s: pl.BlockSpec((rows, W_A), lambda c, s=s: (cur(c), s))
    return pl.pallas_call(
        functools.partial(_mix_prompt_kernel, final=final),
        grid=(n_blocks + 1,),
        in_specs=[
            pl.BlockSpec(memory_space=pltpu.SMEM),
            sec_spec(SEC_QA), sec_spec(SEC_GA), sec_spec(SEC_QB),
            sec_spec(SEC_KB), sec_spec(SEC_VB), sec_spec(SEC_GB),
            pl.BlockSpec((rows, COL_TILE), lambda c: (cur(c), KV_BLOCK)),
            pl.BlockSpec((1, W_B), lambda c: (0, 0)),
            pl.BlockSpec((rows, D_MODEL), lambda c: (prev(c), 0)),
            pl.BlockSpec((None, D_MODEL, D_MODEL), lambda c: (d, 0, 0), pipeline_mode=pl.Buffered(1)),
            pl.BlockSpec((1, D_MODEL), lambda c: (0, 0)),
        ],
        out_specs=[
            pl.BlockSpec((rows, D_MODEL), lambda c: (prev(c), 0)),
            pl.BlockSpec((N_HEADS_B, HD_B, HD_B), lambda c: (0, 0, 0)),
        ],
        out_shape=[
            jax.ShapeDtypeStruct((t, D_MODEL), F32),
            jax.ShapeDtypeStruct((N_HEADS_B, HD_B, HD_B), F32),
        ],
        scratch_shapes=[pltpu.VMEM((N_HEADS_B, C, LANES), F32)] * 3 + [
            pltpu.VMEM((2 * C, COL_TILE), BF16),
            pltpu.VMEM((N_KV_A, GROUP_A * C, LANES), BF16),
            pltpu.VMEM((N_KV_A, GROUP_A * C, 2 * LANES), F32),
            pltpu.VMEM((N_KV_A, GROUP_A * C, 2 * LANES), BF16),
            pltpu.VMEM((N_KV_A, GROUP_A * C, LANES), F32),
            pltpu.VMEM((N_HEADS_B, C, HD_B), BF16),
            pltpu.VMEM((N_HEADS_B, C, HD_B), F32),
            pltpu.VMEM((2, rows, D_MODEL), BF16),
        ],
        compiler_params=pltpu.CompilerParams(
            dimension_semantics=("arbitrary",),
            vmem_limit_bytes=VMEM_LIMIT),
        name="mix_prompt",
    )(sinks, p, p, p, p, p, p, p, gn_w, x, w_out_all, final_w)


SAMPLE_BB = 8
LANE_BATCH = LANES // 4


def _mix_sample_kernel(sinkrow_ref, q3_ref, ga3_ref, qb_ref, kb_ref, vb_ref, gb_ref, kvn_ref,
                       kbt_ref, vbg_ref, ck_ref, cv_ref, st_ref, gnw_ref, *rest, t_new):
    ya3_ref, yb_ref, nk_ref, nv_ref, nst_ref, s_ref, p_ref, vall_ref, kt_ref, x_ref, sc_ref = rest[-11:]
    i = pl.program_id(0)
    T = t_new
    BB = SAMPLE_BB
    L = ck_ref.shape[1]
    n_keys = L + T
    QR = T * GROUP_A
    R = BB * T
    lo = lax.broadcasted_iota(jnp.int32, (T, LANES), 1) < HD_A
    head_sl = lambda g: slice(g * GROUP_A, (g + 1) * GROUP_A)
    lane_sl = lambda g: slice(g * HD_A, (g + 1) * HD_A)
    tok_rows = lambda bb: slice(bb * T, (bb + 1) * T)


    for bb in range(BB):
        kvn = kvn_ref[tok_rows(bb), :]
        k_new = jnp.where(lo, kvn[:, 0:LANES], kvn[:, LANES:2 * LANES])
        v_new = jnp.where(lo, kvn[:, 2 * LANES:3 * LANES], kvn[:, 3 * LANES:4 * LANES])
        k_all = jnp.concatenate([ck_ref[bb], k_new], axis=0)
        v_all = jnp.concatenate([cv_ref[bb], v_new], axis=0)
        nk_ref[bb] = k_all[T:, :]
        nv_ref[bb] = v_all[T:, :]
        vall_ref[bb] = v_all.astype(BF16)
        k_all_b = k_all.astype(BF16)
        for g in range(N_KV_A):
            q = q3_ref[tok_rows(bb), head_sl(g), :].reshape(T * GROUP_A, HD_A).astype(BF16)
            s_ref[g, bb * QR:(bb + 1) * QR, :] = _nt_dot(q, k_all_b[:, lane_sl(g)])

    lane_t = (lax.broadcasted_iota(jnp.int32, (1, LANES), 1) % T).astype(F32)
    lane_b = lax.broadcasted_iota(jnp.int32, (HD_B, LANES), 1) // T
    slot0 = (i % (LANE_BATCH // BB)) * BB
    for h in range(N_HEADS_B):
        hc = slice(h * HD_B, (h + 1) * HD_B)
        lg = LOG_G[h]
        kdec = jnp.exp(lg * (T - 1.0 - lane_t))
        kt_dec = kbt_ref[hc, :] * kdec
        for bb in range(BB):
            kt_ref[h, bb * HD_B:(bb + 1) * HD_B, :] = jnp.where(lane_b == slot0 + bb, kt_dec, 0.0).astype(BF16)
        u = jnp.dot(kt_ref[h], vbg_ref[:, hc].astype(BF16), preferred_element_type=F32)
        for bb in range(BB):
            r = st_ref[bb, h]
            pr = slice((bb // 2) * 2 * T, (bb // 2 + 1) * 2 * T)
            x_ref[bb * N_HEADS_B + h] = jnp.dot(qb_ref[pr, hc].astype(BF16), r.astype(BF16),
                                                 preferred_element_type=F32)
            nst_ref[bb, h] = math.exp(lg * T) * r + u[bb * HD_B:(bb + 1) * HD_B, :]

    ri = lax.broadcasted_iota(jnp.int32, (R, R), 0)
    ci = lax.broadcasted_iota(jnp.int32, (R, R), 1)
    same_b = (ri // T) == (ci // T)
    dt = (ri % T - ci % T).astype(F32)
    keep = same_b & (dt >= 0)
    for h in range(N_HEADS_B):
        hc = slice(h * HD_B, (h + 1) * HD_B)
        dmask = jnp.where(keep, jnp.exp(LOG_G[h] * jnp.maximum(dt, 0.0)), 0.0)
        sc = _nt_dot(qb_ref[:, hc].astype(BF16), kb_ref[:, hc].astype(BF16)) * dmask
        sc_ref[h] = sc.astype(BF16)

    rows_q = (lax.broadcasted_iota(jnp.int32, (BB * QR, n_keys), 0) % QR) // GROUP_A
    key = lax.broadcasted_iota(jnp.int32, (BB * QR, n_keys), 1)
    delta = jnp.where(key < L, rows_q + L - key, rows_q - (key - L))
    valid = (delta >= 0) & (delta < WINDOW)
    for g in range(N_KV_A):
        s = jnp.where(valid, s_ref[g] * (HD_A ** -0.5), -jnp.inf)
        sink = sinkrow_ref[g]
        m = jnp.maximum(jnp.max(s, axis=-1, keepdims=True), sink)
        e = jnp.exp(s - m)
        denom = jnp.sum(e, axis=-1, keepdims=True) + jnp.exp(sink - m)
        p_ref[g] = (e / denom).astype(BF16)

    tposf = (lax.broadcasted_iota(jnp.int32, (R, 1), 0) % T).astype(F32)
    upper = (lax.broadcasted_iota(jnp.int32, (2 * T, 1), 0) < T)
    for h in range(N_HEADS_B):
        hc = slice(h * HD_B, (h + 1) * HD_B)
        intra = jnp.dot(sc_ref[h], vb_ref[:, hc].astype(BF16), preferred_element_type=F32)
        inter = jnp.concatenate(
            [jnp.where(upper, x_ref[(2 * pair) * N_HEADS_B + h], x_ref[(2 * pair + 1) * N_HEADS_B + h])
             for pair in range(BB // 2)], axis=0)
        o = intra + inter * jnp.exp(LOG_G[h] * (tposf + 1.0))
        mu = jnp.mean(o, axis=-1, keepdims=True)
        d = o - mu
        var = jnp.mean(d * d, axis=-1, keepdims=True)
        yb_ref[:, hc] = d * lax.rsqrt(var + EPS) * gnw_ref[:, hc] * _silu(gb_ref[:, hc])

    for bb in range(BB):
        for g in range(N_KV_A):
            o = jnp.dot(p_ref[g, bb * QR:(bb + 1) * QR, :], vall_ref[bb, :, lane_sl(g)],
                        preferred_element_type=F32)
            gate = _silu(ga3_ref[tok_rows(bb), head_sl(g), :].reshape(QR, HD_A))
            ya3_ref[tok_rows(bb), head_sl(g), :] = (o * gate).reshape(T, GROUP_A, HD_A)


def _mix_sample(ps, sinks, gn_w, cache_k, cache_v, state, d, prev_out, t_new):
    m = ps.shape[0]
    nb = m // t_new
    L = cache_k.shape[2]
    BB = SAMPLE_BB
    n_alias = len(prev_out)
    n_in = 14
    rows = BB * t_new
    qa = ps[:, SEC_QA * W_A:(SEC_QA + 1) * W_A].reshape(m, N_HEADS_A, HD_A)
    ga = ps[:, SEC_GA * W_A:(SEC_GA + 1) * W_A].reshape(m, N_HEADS_A, HD_A)
    kbt = ps[:, SEC_KB * W_A:(SEC_KB + 1) * W_A].T
    sinkrow = jnp.tile(sinks.reshape(N_KV_A, 1, GROUP_A), (1, BB * t_new, 1)).reshape(N_KV_A, -1, 1)
    sec_spec = lambda s: pl.BlockSpec((rows, W_A), lambda i, s=s: (i, s))
    steps_per_lane_group = LANE_BATCH // BB
    return pl.pallas_call(
        functools.partial(_mix_sample_kernel, t_new=t_new),
        grid=(nb // BB,),
        in_specs=[
            pl.BlockSpec((N_KV_A, BB * t_new * GROUP_A, 1), lambda i: (0, 0, 0)),
            pl.BlockSpec((rows, N_HEADS_A, HD_A), lambda i: (i, 0, 0)),
            pl.BlockSpec((rows, N_HEADS_A, HD_A), lambda i: (i, 0, 0)),
            sec_spec(SEC_QB), sec_spec(SEC_KB), sec_spec(SEC_VB), sec_spec(SEC_GB),
            pl.BlockSpec((rows, COL_TILE), lambda i: (i, KV_BLOCK)),
            pl.BlockSpec((W_B, LANES), lambda i: (0, i // steps_per_lane_group)),
            pl.BlockSpec((LANES, W_B), lambda i: (i // steps_per_lane_group, SEC_VB)),
            pl.BlockSpec((None, BB, L, LANES), lambda i: (d, i, 0, 0)),
            pl.BlockSpec((None, BB, L, LANES), lambda i: (d, i, 0, 0)),
            pl.BlockSpec((None, BB, N_HEADS_B, HD_B, HD_B), lambda i: (d, i, 0, 0, 0)),
            pl.BlockSpec((1, W_B), lambda i: (0, 0)),
        ] + [pl.BlockSpec(memory_space=pl.ANY)] * n_alias,
        out_specs=[
            pl.BlockSpec((rows, N_HEADS_A, HD_A), lambda i: (i, 0, 0)),
            pl.BlockSpec((rows, W_B), lambda i: (i, 0)),
            pl.BlockSpec((None, BB, L, LANES), lambda i: (d, i, 0, 0)),
            pl.BlockSpec((None, BB, L, LANES), lambda i: (d, i, 0, 0)),
            pl.BlockSpec((None, BB, N_HEADS_B, HD_B, HD_B), lambda i: (d, i, 0, 0, 0)),
        ],
        out_shape=[
            jax.ShapeDtypeStruct((m, N_HEADS_A, HD_A), F32),
            jax.ShapeDtypeStruct((m, W_B), F32),
            jax.ShapeDtypeStruct(cache_k.shape, F32),
            jax.ShapeDtypeStruct(cache_v.shape, F32),
            jax.ShapeDtypeStruct(state.shape, F32),
        ],
        input_output_aliases={n_in + a: 2 + a for a in range(n_alias)},
        scratch_shapes=[
            pltpu.VMEM((N_KV_A, BB * t_new * GROUP_A, L + t_new), F32),
            pltpu.VMEM((N_KV_A, BB * t_new * GROUP_A, L + t_new), BF16),
            pltpu.VMEM((BB, L + t_new, LANES), BF16),
            pltpu.VMEM((N_HEADS_B, BB * HD_B, LANES), BF16),
            pltpu.VMEM((BB * N_HEADS_B, 2 * t_new, HD_B), F32),
            pltpu.VMEM((N_HEADS_B, rows, rows), BF16),
        ],
        compiler_params=pltpu.CompilerParams(
            dimension_semantics=("arbitrary",),
            vmem_limit_bytes=VMEM_LIMIT),
        name="mix_sample",
    )(sinkrow, qa, ga, ps, ps, ps, ps, ps, kbt, ps, cache_k, cache_v, state, gn_w, *prev_out)


def _prep_w_kv(w_in):
    kv_width = N_KV_A * HD_A
    ka = w_in[..., W_A:W_A + kv_width]
    va = w_in[..., W_A + kv_width:W_A + 2 * kv_width]
    dup = lambda a: [a[..., g * HD_A:(g + 1) * HD_A] for g in range(N_KV_A) for _ in range(2)]
    return jnp.concatenate(dup(ka) + dup(va), axis=-1).astype(BF16)


def _rope_tables(pos):
    def tab(hd):
        half = hd // 2
        inv = ROPE_THETA ** (-jnp.arange(half, dtype=F32) / half)
        ang = pos.astype(F32)[:, None] * inv[None, :]
        cos, sin = jnp.cos(ang), jnp.sin(ang)
        reps = LANES // hd
        return (jnp.tile(jnp.concatenate([cos, cos], axis=1), (1, reps)),
                jnp.tile(jnp.concatenate([-sin, sin], axis=1), (1, reps)))
    return tab(HD_A) + tab(HD_B)


def kernel(x_prompt, x_sample, cache_k_win, cache_v_win, state_ret, norm_w, w_in, attn_sinks,
           ret_norm_w, w_out, final_norm_w):
    bp, tp, _ = x_prompt.shape
    bs, ts, _ = x_sample.shape
    assert bp == 1 and tp % RET_CHUNK == 0 and min(WINDOW, tp) == WINDOW
    L = cache_k_win.shape[2]
    depth = w_in.shape[0]

    w_in_b = w_in.astype(BF16)
    w_kv = _prep_w_kv(w_in)
    w_out_b = w_out.astype(BF16)
    tab_p = _rope_tables(jnp.arange(tp, dtype=jnp.int32))
    tab_s = _rope_tables(jnp.tile(PAST_LEN + jnp.arange(ts, dtype=jnp.int32), bs))
    fw = final_norm_w.reshape(1, D_MODEL)

    xp = x_prompt.reshape(tp, D_MODEL)
    xs = x_sample.reshape(bs * ts, D_MODEL)
    ck = cache_k_win.reshape(depth, bs, L, N_KV_A * HD_A)
    cv = cache_v_win.reshape(depth, bs, L, N_KV_A * HD_A)

    kp_l, vp_l, rp_l = [], [], []
    sample_out = ()
    for d in range(depth):
        final = d == depth - 1
        nw = norm_w[d].reshape(1, D_MODEL)
        gnw = ret_norm_w[d].reshape(1, W_B)
        p, tail = _in_proj(xp, nw, w_in_b, w_kv, d, tab_p, BF16, tm=1024)
        xp, r_fin = _mix_prompt(p, xp, attn_sinks[d], gnw, w_out_b, d, fw, final)
        kp_l.append(jnp.stack([tail[:, 0:HD_A], tail[:, LANES:LANES + HD_A]], axis=1))
        vp_l.append(jnp.stack([tail[:, 2 * LANES:2 * LANES + HD_A], tail[:, 3 * LANES:3 * LANES + HD_A]], axis=1))
        rp_l.append(r_fin)
        ps, _ = _in_proj(xs, nw, w_in_b, w_kv, d, tab_s, F32, tm=bs * ts)
        ya3, yb, *sample_out = _mix_sample(ps, attn_sinks[d], gnw, ck, cv, state_ret, d, sample_out, ts)
        ys = jnp.concatenate([ya3.reshape(bs * ts, W_A), yb], axis=1).astype(BF16)
        xs = _out_proj(ys, xs, w_out_b, d, fw, final, tm=bs * ts)

    keep = min(WINDOW, tp)
    nk, nv, nst = sample_out
    return (xp.reshape(bp, tp, D_MODEL), xs.reshape(bs, ts, D_MODEL),
            jnp.stack(kp_l).reshape(depth, bp, keep, N_KV_A, HD_A),
            jnp.stack(vp_l).reshape(depth, bp, keep, N_KV_A, HD_A),
            jnp.stack(rp_l).reshape(depth, bp, N_HEADS_B, HD_B, HD_B),
            nk.reshape(depth, bs, L, N_KV_A, HD_A), nv.reshape(depth, bs, L, N_KV_A, HD_A), nst)
```

```python
import functools
import math

import jax
import jax.numpy as jnp
from jax import lax
from jax.experimental import pallas as pl
from jax.experimental.pallas import tpu as pltpu

F32 = jnp.float32
BF16 = jnp.bfloat16

D_MODEL = 2048
DEPTH = 2
PAST_LEN = 8192
WINDOW = 128
HD_A = 64
N_HEADS_A = 16
N_KV_A = 2
GROUP_A = N_HEADS_A // N_KV_A
W_A = N_HEADS_A * HD_A
HD_B = 128
N_HEADS_B = 8
W_B = N_HEADS_B * HD_B
RET_CHUNK = 128
ROPE_THETA = 10000.0
EPS = 1e-6

LANES = 128
MXU_WIDTH = 256
SEC_QA, SEC_GA, SEC_QB, SEC_KB, SEC_VB, SEC_GB = 0, 1, 2, 3, 4, 5
IN_TILE = 1024
TILES_PER_SEC = W_A // IN_TILE
KV_TILE = 6 * TILES_PER_SEC
COL_TILE = 4 * N_KV_A * HD_A
KV_BLOCK = KV_TILE * IN_TILE // COL_TILE
D_PROJ_R = (KV_TILE + 1) * IN_TILE
VMEM_LIMIT = 56 * 1024 * 1024

LOG_G = [math.log1p(-(2.0 ** (-5.0 - h))) for h in range(N_HEADS_B)]


def _silu(g):
    return g * (1.0 / (1.0 + jnp.exp(-g)))


EPI_ROPE_A, EPI_ROPE_B, EPI_ROPE_B_SCALED, EPI_PLAIN = 0, 1, 2, 3
N_EPI = 4
_SEC_EPI = (EPI_ROPE_A, EPI_PLAIN, EPI_ROPE_B, EPI_ROPE_B_SCALED, EPI_PLAIN, EPI_PLAIN)


KV_ROPE_GROUPS = N_KV_A
N_W_BLOCKS = IN_TILE // MXU_WIDTH


def _tile_epi(tile, hi):
    sec = tile // TILES_PER_SEC
    t = jnp.int32(EPI_PLAIN if hi else EPI_ROPE_A)
    for s, e in enumerate(_SEC_EPI):
        t = jnp.where(sec == s, e, t)
    return t


def _in_proj_kernel(x_ref, nw_ref, *refs, row_chunk, emit_w):
    n_w = N_W_BLOCKS if emit_w else 1
    w_refs = refs[:n_w]
    wkv_ref, ca_ref, sa_ref, cb_ref, sb_ref, o_ref, tail_ref = refs[n_w:n_w + 7]
    w_tile_ref = refs[n_w + 7] if emit_w else w_refs[0]
    h_ref = refs[-1]
    j = pl.program_id(1)

    def weights(e):
        cols = slice(e * MXU_WIDTH, (e + 1) * MXU_WIDTH)
        if emit_w:
            w_tile_ref[:, cols] = w_refs[e][...].astype(BF16)
        return w_tile_ref[:, cols]

    tm = x_ref.shape[0]
    n_chunks = tm // row_chunk

    @pl.when(j == 0)
    def _():
        def body(r, carry):
            rows = pl.ds(pl.multiple_of(r * row_chunk, row_chunk), row_chunk)
            x = x_ref[rows, :]
            ms = jnp.mean(x * x, axis=-1, keepdims=True)
            h_ref[rows, :] = (x * lax.rsqrt(ms + EPS) * nw_ref[...]).astype(BF16)
            return carry
        lax.fori_loop(0, n_chunks, body, 0)

    lane = lax.broadcasted_iota(jnp.int32, (tm, LANES), 1)
    first_half = (lane % HD_A) < HD_A // 2

    def rotate(x, c, s, shift_up, shift_dn):
        partner = jnp.where(first_half, pltpu.roll(x, shift_up, 1), pltpu.roll(x, shift_dn, 1))
        return x * c + partner * s

    epi = _tile_epi(j, 0)
    is_plain = epi == EPI_PLAIN

    @pl.when((j < KV_TILE) & is_plain)
    def _():
        for e in range(N_W_BLOCKS):
            cols = slice(e * MXU_WIDTH, (e + 1) * MXU_WIDTH)
            o_ref[:, cols] = jnp.dot(h_ref[...], weights(e), preferred_element_type=F32).astype(o_ref.dtype)

    @pl.when((j < KV_TILE) & jnp.logical_not(is_plain))
    def _():
        is_a = epi == EPI_ROPE_A
        scale = jnp.where(epi == EPI_ROPE_B_SCALED, HD_B ** -0.5, 1.0).astype(F32)
        c = jnp.where(is_a, ca_ref[...], cb_ref[...]) * scale
        s = jnp.where(is_a, sa_ref[...], sb_ref[...]) * scale
        shift_up = jnp.where(is_a, LANES - HD_A // 2, HD_B // 2)
        shift_dn = jnp.where(is_a, HD_A // 2, HD_B // 2)
        for e in range(N_W_BLOCKS):
            acc = jnp.dot(h_ref[...], weights(e), preferred_element_type=F32)
            for gg in range(MXU_WIDTH // LANES):
                cols = slice(e * MXU_WIDTH + gg * LANES, e * MXU_WIDTH + (gg + 1) * LANES)
                out = rotate(acc[:, gg * LANES:(gg + 1) * LANES], c, s, shift_up, shift_dn)
                o_ref[:, cols] = out.astype(o_ref.dtype)

    @pl.when(j == KV_TILE)
    def _():
        acc = jnp.dot(h_ref[...], wkv_ref[...], preferred_element_type=F32)
        for g in range(COL_TILE // LANES):
            cols = slice(g * LANES, (g + 1) * LANES)
            out = acc[:, cols]
            if g < KV_ROPE_GROUPS:
                out = rotate(out, ca_ref[...], sa_ref[...], LANES - HD_A // 2, HD_A // 2)
            o_ref[:, cols] = out.astype(o_ref.dtype)
            tail_ref[:, cols] = out[tm - WINDOW:, :]
        o_ref[:, COL_TILE:] = jnp.zeros((tm, IN_TILE - COL_TILE), o_ref.dtype)
        if emit_w:
            w_tile_ref[...] = jnp.zeros(w_tile_ref.shape, BF16)


def _in_proj(x, norm_w, w_src, wkv_all, d, tables, out_dtype, tm, emit_w):
    m = x.shape[0]
    n_tiles = D_PROJ_R // IN_TILE
    assert m == tm or not emit_w
    n_src_blocks = (6 * W_A + 2 * N_KV_A * HD_A) // MXU_WIDTH

    def src_block(j, e):
        sec = j // TILES_PER_SEC
        blk = sec * (W_A // MXU_WIDTH) + jnp.minimum(sec, 1) + (j % TILES_PER_SEC) * N_W_BLOCKS + e
        return jnp.minimum(blk, n_src_blocks - 1)

    if emit_w:
        w_specs = [pl.BlockSpec((None, D_MODEL, MXU_WIDTH), lambda i, j, e=e: (d, 0, src_block(j, e)))
                   for e in range(N_W_BLOCKS)]
    else:
        w_specs = [pl.BlockSpec((D_MODEL, IN_TILE), lambda i, j: (0, jnp.minimum(j, KV_TILE - 1)))]
    tab_spec = pl.BlockSpec((tm, LANES), lambda i, j: (i, 0))

    return pl.pallas_call(
        functools.partial(_in_proj_kernel, row_chunk=min(tm, 256), emit_w=emit_w),
        grid=(m // tm, n_tiles),
        in_specs=[
            pl.BlockSpec((tm, D_MODEL), lambda i, j: (i, 0)),
            pl.BlockSpec((1, D_MODEL), lambda i, j: (0, 0)),
        ] + w_specs + [
            pl.BlockSpec((None, D_MODEL, COL_TILE), lambda i, j: (d, 0, 0)),
            tab_spec, tab_spec, tab_spec, tab_spec,
        ],
        out_specs=[
            pl.BlockSpec((tm, IN_TILE), lambda i, j: (i, j)),
            pl.BlockSpec((WINDOW, COL_TILE), lambda i, j: (0, 0)),
        ] + ([pl.BlockSpec((D_MODEL, IN_TILE), lambda i, j: (0, j))] if emit_w else []),
        out_shape=[
            jax.ShapeDtypeStruct((m, D_PROJ_R), out_dtype),
            jax.ShapeDtypeStruct((WINDOW, COL_TILE), F32),
        ] + ([jax.ShapeDtypeStruct((D_MODEL, D_PROJ_R), BF16)] if emit_w else []),
        scratch_shapes=[pltpu.VMEM((tm, D_MODEL), BF16)],
        compiler_params=pltpu.CompilerParams(
            dimension_semantics=("arbitrary", "arbitrary"),
            vmem_limit_bytes=VMEM_LIMIT),
        name="in_proj",
    )(x, norm_w, *([w_src] * len(w_specs)), wkv_all, *tables)


def _out_proj_kernel(y_ref, x_ref, w_ref, fw_ref, o_ref, wout_ref, *, final):
    wout_ref[...] = w_ref[...].astype(BF16)
    acc = x_ref[...] + jnp.dot(y_ref[...], wout_ref[...], preferred_element_type=F32)
    if final:
        ms = jnp.mean(acc * acc, axis=-1, keepdims=True)
        acc = acc * lax.rsqrt(ms + EPS) * fw_ref[...]
    o_ref[...] = acc


def _out_proj(y, x, w_all, d, final_w, final):
    m = x.shape[0]
    return pl.pallas_call(
        functools.partial(_out_proj_kernel, final=final),
        grid=(1,),
        in_specs=[
            pl.BlockSpec((m, D_MODEL), lambda i: (0, 0)),
            pl.BlockSpec((m, D_MODEL), lambda i: (0, 0)),
            pl.BlockSpec((None, D_MODEL, D_MODEL), lambda i: (d, 0, 0), pipeline_mode=pl.Buffered(1)),
            pl.BlockSpec((1, D_MODEL), lambda i: (0, 0)),
        ],
        out_specs=[
            pl.BlockSpec((m, D_MODEL), lambda i: (0, 0)),
            pl.BlockSpec((D_MODEL, D_MODEL), lambda i: (0, 0)),
        ],
        out_shape=[
            jax.ShapeDtypeStruct((m, D_MODEL), F32),
            jax.ShapeDtypeStruct((D_MODEL, D_MODEL), BF16),
        ],
        compiler_params=pltpu.CompilerParams(
            dimension_semantics=("arbitrary",),
            vmem_limit_bytes=VMEM_LIMIT),
        name="out_proj",
    )(y, x, w_all, final_w)


def _nt_dot(a, b):
    return lax.dot_general(a, b, (((1,), (1,)), ((), ())), preferred_element_type=F32)


def _tn_dot(a, b):
    return lax.dot_general(a, b, (((0,), (0,)), ((), ())), preferred_element_type=F32)


MIX_CHUNKS = 4
OUT_PIECES = D_MODEL // MXU_WIDTH
PIECES_PER_CHUNK = OUT_PIECES // MIX_CHUNKS


def _mix_prompt_kernel(sink_ref, qa_ref, ga_ref, qb_ref, kb_ref, vb_ref, gb_ref,
                       kv_ref, gnw_ref, x_ref, wo_ref, fw_ref, xo_ref, r_ref,
                       dmask_ref, qdec_ref, kdec_ref, kvp_ref,
                       qs_ref, s_ref, p_ref, o_ref, scb_ref, inter_ref, y_ref, *, final):
    step = pl.program_id(0)
    last_step = pl.num_programs(0) - 1
    slot = step % 2
    prev_slot = 1 - slot
    C = RET_CHUNK
    row = lax.broadcasted_iota(jnp.int32, (C, LANES), 0)
    lane = lax.broadcasted_iota(jnp.int32, (C, LANES), 1)

    @pl.when(step == 0)
    def _():
        r_ref[...] = jnp.zeros(r_ref.shape, F32)
        kvp_ref[...] = jnp.zeros(kvp_ref.shape, kvp_ref.dtype)
        y_ref[1] = jnp.zeros(y_ref.shape[1:], y_ref.dtype)
        rowf = row.astype(F32)
        diff = rowf - lane.astype(F32)
        for h in range(N_HEADS_B):
            dmask_ref[h] = jnp.where(diff >= 0, jnp.exp(LOG_G[h] * jnp.maximum(diff, 0.0)), 0.0)
            qdec_ref[h] = jnp.exp(LOG_G[h] * (rowf + 1.0))
            kdec_ref[h] = jnp.exp(LOG_G[h] * (C - 1.0 - rowf))

    tri = lane <= row
    lo = lane < HD_A
    neg_inf = jnp.float32(-jnp.inf)
    half_group = GROUP_A // 2
    kcol = lambda g: slice(g * LANES, (g + 1) * LANES)
    vcol = lambda g: slice((N_KV_A + g) * LANES, (N_KV_A + g + 1) * LANES)
    pair_col = lambda g, pi: slice((g * half_group + pi) * LANES, (g * half_group + pi + 1) * LANES)
    head_rows = lambda hh: slice(hh * C, (hh + 1) * C)

    def out_piece(e):
        cols = slice(e * MXU_WIDTH, (e + 1) * MXU_WIDTH)
        xo_ref[:, cols] = x_ref[:, cols] + jnp.dot(y_ref[prev_slot], wo_ref[:, cols],
                                                   preferred_element_type=F32)

    def finish_out():
        if final:
            acc = xo_ref[...]
            ms = jnp.mean(acc * acc, axis=-1, keepdims=True)
            xo_ref[...] = acc * lax.rsqrt(ms + EPS) * fw_ref[...]

    def chunk(cc):
        rows = slice(cc * C, (cc + 1) * C)
        has_prev = step * MIX_CHUNKS + cc > 0
        if cc == 0:
            kvp_ref[C:, :] = kv_ref[rows, :]
            kv2 = lambda cols: kvp_ref[:, cols]
        else:
            kv2 = lambda cols: kv_ref[(cc - 1) * C:(cc + 1) * C, cols]

        for g in range(N_KV_A):
            for pi in range(half_group):
                q_pair = qa_ref[rows, pair_col(g, pi)].astype(F32)
                qs_ref[g, head_rows(2 * pi), :] = jnp.where(lo, q_pair, 0.0).astype(BF16)
                qs_ref[g, head_rows(2 * pi + 1), :] = jnp.where(lo, 0.0, q_pair).astype(BF16)
            s_ref[g] = _nt_dot(qs_ref[g], kv2(kcol(g)))

        out_piece(cc * PIECES_PER_CHUNK)

        for h in range(N_HEADS_B):
            hc = slice(h * HD_B, (h + 1) * HD_B)
            q, k, v = qb_ref[rows, hc], kb_ref[rows, hc], vb_ref[rows, hc]
            scb_ref[h] = (_nt_dot(q, k) * dmask_ref[h]).astype(BF16)
            r = r_ref[h]
            inter_ref[h] = jnp.dot(q, r.astype(BF16), preferred_element_type=F32) * qdec_ref[h]
            kd = (k.astype(F32) * kdec_ref[h]).astype(BF16)
            r_ref[h] = math.exp(LOG_G[h] * C) * r + _tn_dot(kd, v)

        for g in range(N_KV_A):
            for hh in range(GROUP_A):
                hr = head_rows(hh)
                s_prev = jnp.where(has_prev, s_ref[g, hr, :LANES], neg_inf)
                s = jnp.where(tri, s_ref[g, hr, LANES:], s_prev) * (HD_A ** -0.5)
                sink = sink_ref[g * GROUP_A + hh]
                m = jnp.maximum(jnp.max(s, axis=-1, keepdims=True), sink)
                e = jnp.exp(s - m)
                denom = jnp.sum(e, axis=-1, keepdims=True) + jnp.exp(sink - m)
                p = e / denom
                p_ref[g, hr, :LANES] = jnp.where(tri, 0.0, p).astype(BF16)
                p_ref[g, hr, LANES:] = jnp.where(tri, p, 0.0).astype(BF16)

        for g in range(N_KV_A):
            o_ref[g] = jnp.dot(p_ref[g], kv2(vcol(g)), preferred_element_type=F32)

        for e in range(cc * PIECES_PER_CHUNK + 1, (cc + 1) * PIECES_PER_CHUNK):
            out_piece(e)

        for h in range(N_HEADS_B):
            hc = slice(h * HD_B, (h + 1) * HD_B)
            o = jnp.dot(scb_ref[h], vb_ref[rows, hc], preferred_element_type=F32) + inter_ref[h]
            mu = jnp.mean(o, axis=-1, keepdims=True)
            d = o - mu
            var = jnp.mean(d * d, axis=-1, keepdims=True)
            yh = d * lax.rsqrt(var + EPS) * gnw_ref[:, hc] * _silu(gb_ref[rows, hc].astype(F32))
            y_ref[slot, rows, W_A + h * HD_B: W_A + (h + 1) * HD_B] = yh.astype(y_ref.dtype)

        for g in range(N_KV_A):
            for pi in range(half_group):
                pc = pair_col(g, pi)
                o_pair = jnp.where(lo, o_ref[g, head_rows(2 * pi), :], o_ref[g, head_rows(2 * pi + 1), :])
                y_ref[slot, rows, pc] = (o_pair * _silu(ga_ref[rows, pc].astype(F32))).astype(y_ref.dtype)

        if cc == MIX_CHUNKS - 1:
            kvp_ref[:C, :] = kv_ref[rows, :]

    @pl.when(step < last_step)
    def _():
        for cc in range(MIX_CHUNKS):
            chunk(cc)
        finish_out()

    @pl.when(step == last_step)
    def _():
        for e in range(OUT_PIECES):
            out_piece(e)
        finish_out()


def _mix_prompt(p, x, sinks, gn_w, w_out_b, final_w, final):
    t = p.shape[0]
    C = RET_CHUNK
    rows = MIX_CHUNKS * C
    n_blocks = t // rows
    cur = lambda c: jnp.minimum(c, n_blocks - 1)
    prev = lambda c: jnp.maximum(c - 1, 0)
    sec_spec = lambda s: pl.BlockSpec((rows, W_A), lambda c, s=s: (cur(c), s))
    return pl.pallas_call(
        functools.partial(_mix_prompt_kernel, final=final),
        grid=(n_blocks + 1,),
        in_specs=[
            pl.BlockSpec(memory_space=pltpu.SMEM),
            sec_spec(SEC_QA), sec_spec(SEC_GA), sec_spec(SEC_QB),
            sec_spec(SEC_KB), sec_spec(SEC_VB), sec_spec(SEC_GB),
            pl.BlockSpec((rows, COL_TILE), lambda c: (cur(c), KV_BLOCK)),
            pl.BlockSpec((1, W_B), lambda c: (0, 0)),
            pl.BlockSpec((rows, D_MODEL), lambda c: (prev(c), 0)),
            pl.BlockSpec((D_MODEL, D_MODEL), lambda c: (0, 0), pipeline_mode=pl.Buffered(1)),
            pl.BlockSpec((1, D_MODEL), lambda c: (0, 0)),
        ],
        out_specs=[
            pl.BlockSpec((rows, D_MODEL), lambda c: (prev(c), 0)),
            pl.BlockSpec((N_HEADS_B, HD_B, HD_B), lambda c: (0, 0, 0)),
        ],
        out_shape=[
            jax.ShapeDtypeStruct((t, D_MODEL), F32),
            jax.ShapeDtypeStruct((N_HEADS_B, HD_B, HD_B), F32),
        ],
        scratch_shapes=[pltpu.VMEM((N_HEADS_B, C, LANES), F32)] * 3 + [
            pltpu.VMEM((2 * C, COL_TILE), BF16),
            pltpu.VMEM((N_KV_A, GROUP_A * C, LANES), BF16),
            pltpu.VMEM((N_KV_A, GROUP_A * C, 2 * LANES), F32),
            pltpu.VMEM((N_KV_A, GROUP_A * C, 2 * LANES), BF16),
            pltpu.VMEM((N_KV_A, GROUP_A * C, LANES), F32),
            pltpu.VMEM((N_HEADS_B, C, HD_B), BF16),
            pltpu.VMEM((N_HEADS_B, C, HD_B), F32),
            pltpu.VMEM((2, rows, D_MODEL), BF16),
        ],
        compiler_params=pltpu.CompilerParams(
            dimension_semantics=("arbitrary",),
            vmem_limit_bytes=VMEM_LIMIT),
        name="mix_prompt",
    )(sinks, p, p, p, p, p, p, p, gn_w, x, w_out_b, final_w)


SAMPLE_BB = 16
LANE_BATCH = LANES // 4


def _mix_sample_kernel(sinkrow_ref, q3_ref, ga3_ref, qb_ref, kb_ref, vb_ref, gb_ref, kvn_ref,
                       kbt_ref, vbg_ref, ck_ref, cv_ref, st_ref, gnw_ref, *rest, t_new):
    ya3_ref, yb_ref, nk_ref, nv_ref, nst_ref, s_ref, p_ref, vall_ref, kt_ref, x_ref, sc_ref = rest[-11:]
    i = pl.program_id(0)
    T = t_new
    BB = SAMPLE_BB
    L = ck_ref.shape[1]
    n_keys = L + T
    QR = T * GROUP_A
    R = BB * T
    lo = lax.broadcasted_iota(jnp.int32, (T, LANES), 1) < HD_A
    head_sl = lambda g: slice(g * GROUP_A, (g + 1) * GROUP_A)
    lane_sl = lambda g: slice(g * HD_A, (g + 1) * HD_A)
    tok_rows = lambda bb: slice(bb * T, (bb + 1) * T)


    for bb in range(BB):
        kvn = kvn_ref[tok_rows(bb), :]
        k_new = jnp.where(lo, kvn[:, 0:LANES], kvn[:, LANES:2 * LANES])
        v_new = jnp.where(lo, kvn[:, 2 * LANES:3 * LANES], kvn[:, 3 * LANES:4 * LANES])
        k_all = jnp.concatenate([ck_ref[bb], k_new], axis=0)
        v_all = jnp.concatenate([cv_ref[bb], v_new], axis=0)
        nk_ref[bb] = k_all[T:, :]
        nv_ref[bb] = v_all[T:, :]
        vall_ref[bb] = v_all.astype(BF16)
        k_all_b = k_all.astype(BF16)
        for g in range(N_KV_A):
            q = q3_ref[tok_rows(bb), head_sl(g), :].reshape(T * GROUP_A, HD_A).astype(BF16)
            s_ref[g, bb * QR:(bb + 1) * QR, :] = _nt_dot(q, k_all_b[:, lane_sl(g)])

    lane_t = (lax.broadcasted_iota(jnp.int32, (1, LANES), 1) % T).astype(F32)
    lane_b = lax.broadcasted_iota(jnp.int32, (HD_B, LANES), 1) // T
    slot0 = (i % (LANE_BATCH // BB)) * BB
    for h in range(N_HEADS_B):
        hc = slice(h * HD_B, (h + 1) * HD_B)
        lg = LOG_G[h]
        kdec = jnp.exp(lg * (T - 1.0 - lane_t))
        kt_dec = kbt_ref[hc, :] * kdec
        for bb in range(BB):
            kt_ref[h, bb * HD_B:(bb + 1) * HD_B, :] = jnp.where(lane_b == slot0 + bb, kt_dec, 0.0).astype(BF16)
        u = jnp.dot(kt_ref[h], vbg_ref[:, hc].astype(BF16), preferred_element_type=F32)
        for bb in range(BB):
            r = st_ref[bb, h]
            pr = slice((bb // 2) * 2 * T, (bb // 2 + 1) * 2 * T)
            x_ref[bb * N_HEADS_B + h] = jnp.dot(qb_ref[pr, hc].astype(BF16), r.astype(BF16),
                                                 preferred_element_type=F32)
            nst_ref[bb, h] = math.exp(lg * T) * r + u[bb * HD_B:(bb + 1) * HD_B, :]

    ri = lax.broadcasted_iota(jnp.int32, (R, R), 0)
    ci = lax.broadcasted_iota(jnp.int32, (R, R), 1)
    same_b = (ri // T) == (ci // T)
    dt = (ri % T - ci % T).astype(F32)
    keep = same_b & (dt >= 0)
    for h in range(N_HEADS_B):
        hc = slice(h * HD_B, (h + 1) * HD_B)
        dmask = jnp.where(keep, jnp.exp(LOG_G[h] * jnp.maximum(dt, 0.0)), 0.0)
        sc = _nt_dot(qb_ref[:, hc].astype(BF16), kb_ref[:, hc].astype(BF16)) * dmask
        sc_ref[h] = sc.astype(BF16)

    rows_q = (lax.broadcasted_iota(jnp.int32, (BB * QR, n_keys), 0) % QR) // GROUP_A
    key = lax.broadcasted_iota(jnp.int32, (BB * QR, n_keys), 1)
    delta = jnp.where(key < L, rows_q + L - key, rows_q - (key - L))
    valid = (delta >= 0) & (delta < WINDOW)
    for g in range(N_KV_A):
        s = jnp.where(valid, s_ref[g] * (HD_A ** -0.5), -jnp.inf)
        sink = sinkrow_ref[g]
        m = jnp.maximum(jnp.max(s, axis=-1, keepdims=True), sink)
        e = jnp.exp(s - m)
        denom = jnp.sum(e, axis=-1, keepdims=True) + jnp.exp(sink - m)
        p_ref[g] = (e / denom).astype(BF16)

    tposf = (lax.broadcasted_iota(jnp.int32, (R, 1), 0) % T).astype(F32)
    upper = (lax.broadcasted_iota(jnp.int32, (2 * T, 1), 0) < T)
    for h in range(N_HEADS_B):
        hc = slice(h * HD_B, (h + 1) * HD_B)
        intra = jnp.dot(sc_ref[h], vb_ref[:, hc].astype(BF16), preferred_element_type=F32)
        inter = jnp.concatenate(
            [jnp.where(upper, x_ref[(2 * pair) * N_HEADS_B + h], x_ref[(2 * pair + 1) * N_HEADS_B + h])
             for pair in range(BB // 2)], axis=0)
        o = intra + inter * jnp.exp(LOG_G[h] * (tposf + 1.0))
        mu = jnp.mean(o, axis=-1, keepdims=True)
        d = o - mu
        var = jnp.mean(d * d, axis=-1, keepdims=True)
        yb_ref[:, hc] = d * lax.rsqrt(var + EPS) * gnw_ref[:, hc] * _silu(gb_ref[:, hc])

    for bb in range(BB):
        for g in range(N_KV_A):
            o = jnp.dot(p_ref[g, bb * QR:(bb + 1) * QR, :], vall_ref[bb, :, lane_sl(g)],
                        preferred_element_type=F32)
            gate = _silu(ga3_ref[tok_rows(bb), head_sl(g), :].reshape(QR, HD_A))
            ya3_ref[tok_rows(bb), head_sl(g), :] = (o * gate).reshape(T, GROUP_A, HD_A)


def _mix_sample(ps, sinks, gn_w, cache_k, cache_v, state, d, prev_out, t_new):
    m = ps.shape[0]
    nb = m // t_new
    L = cache_k.shape[2]
    BB = SAMPLE_BB
    n_alias = len(prev_out)
    n_in = 14
    rows = BB * t_new
    qa = ps[:, SEC_QA * W_A:(SEC_QA + 1) * W_A].reshape(m, N_HEADS_A, HD_A)
    ga = ps[:, SEC_GA * W_A:(SEC_GA + 1) * W_A].reshape(m, N_HEADS_A, HD_A)
    kbt = ps[:, SEC_KB * W_A:(SEC_KB + 1) * W_A].T
    sinkrow = jnp.tile(sinks.reshape(N_KV_A, 1, GROUP_A), (1, BB * t_new, 1)).reshape(N_KV_A, -1, 1)
    sec_spec = lambda s: pl.BlockSpec((rows, W_A), lambda i, s=s: (i, s))
    steps_per_lane_group = LANE_BATCH // BB
    return pl.pallas_call(
        functools.partial(_mix_sample_kernel, t_new=t_new),
        grid=(nb // BB,),
        in_specs=[
            pl.BlockSpec((N_KV_A, BB * t_new * GROUP_A, 1), lambda i: (0, 0, 0)),
            pl.BlockSpec((rows, N_HEADS_A, HD_A), lambda i: (i, 0, 0)),
            pl.BlockSpec((rows, N_HEADS_A, HD_A), lambda i: (i, 0, 0)),
            sec_spec(SEC_QB), sec_spec(SEC_KB), sec_spec(SEC_VB), sec_spec(SEC_GB),
            pl.BlockSpec((rows, COL_TILE), lambda i: (i, KV_BLOCK)),
            pl.BlockSpec((W_B, LANES), lambda i: (0, i // steps_per_lane_group)),
            pl.BlockSpec((LANES, W_B), lambda i: (i // steps_per_lane_group, SEC_VB)),
            pl.BlockSpec((None, BB, L, LANES), lambda i: (d, i, 0, 0)),
            pl.BlockSpec((None, BB, L, LANES), lambda i: (d, i, 0, 0)),
            pl.BlockSpec((None, BB, N_HEADS_B, HD_B, HD_B), lambda i: (d, i, 0, 0, 0)),
            pl.BlockSpec((1, W_B), lambda i: (0, 0)),
        ] + [pl.BlockSpec(memory_space=pl.ANY)] * n_alias,
        out_specs=[
            pl.BlockSpec((rows, N_HEADS_A, HD_A), lambda i: (i, 0, 0)),
            pl.BlockSpec((rows, W_B), lambda i: (i, 0)),
            pl.BlockSpec((None, BB, L, LANES), lambda i: (d, i, 0, 0)),
            pl.BlockSpec((None, BB, L, LANES), lambda i: (d, i, 0, 0)),
            pl.BlockSpec((None, BB, N_HEADS_B, HD_B, HD_B), lambda i: (d, i, 0, 0, 0)),
        ],
        out_shape=[
            jax.ShapeDtypeStruct((m, N_HEADS_A, HD_A), F32),
            jax.ShapeDtypeStruct((m, W_B), F32),
            jax.ShapeDtypeStruct(cache_k.shape, F32),
            jax.ShapeDtypeStruct(cache_v.shape, F32),
            jax.ShapeDtypeStruct(state.shape, F32),
        ],
        input_output_aliases={n_in + a: 2 + a for a in range(n_alias)},
        scratch_shapes=[
            pltpu.VMEM((N_KV_A, BB * t_new * GROUP_A, L + t_new), F32),
            pltpu.VMEM((N_KV_A, BB * t_new * GROUP_A, L + t_new), BF16),
            pltpu.VMEM((BB, L + t_new, LANES), BF16),
            pltpu.VMEM((N_HEADS_B, BB * HD_B, LANES), BF16),
            pltpu.VMEM((BB * N_HEADS_B, 2 * t_new, HD_B), F32),
            pltpu.VMEM((N_HEADS_B, rows, rows), BF16),
        ],
        compiler_params=pltpu.CompilerParams(
            dimension_semantics=("arbitrary",),
            vmem_limit_bytes=VMEM_LIMIT),
        name="mix_sample",
    )(sinkrow, qa, ga, ps, ps, ps, ps, ps, kbt, ps, cache_k, cache_v, state, gn_w, *prev_out)


def _prep_w_kv(w_in):
    kv_width = N_KV_A * HD_A
    ka = w_in[..., W_A:W_A + kv_width]
    va = w_in[..., W_A + kv_width:W_A + 2 * kv_width]
    dup = lambda a: [a[..., g * HD_A:(g + 1) * HD_A] for g in range(N_KV_A) for _ in range(2)]
    return jnp.concatenate(dup(ka) + dup(va), axis=-1).astype(BF16)


def _rope_tables(pos):
    def tab(hd):
        half = hd // 2
        inv = ROPE_THETA ** (-jnp.arange(half, dtype=F32) / half)
        ang = pos.astype(F32)[:, None] * inv[None, :]
        cos, sin = jnp.cos(ang), jnp.sin(ang)
        reps = LANES // hd
        return (jnp.tile(jnp.concatenate([cos, cos], axis=1), (1, reps)),
                jnp.tile(jnp.concatenate([-sin, sin], axis=1), (1, reps)))
    return tab(HD_A) + tab(HD_B)


def kernel(x_prompt, x_sample, cache_k_win, cache_v_win, state_ret, norm_w, w_in, attn_sinks,
           ret_norm_w, w_out, final_norm_w):
    bp, tp, _ = x_prompt.shape
    bs, ts, _ = x_sample.shape
    assert bp == 1 and tp % RET_CHUNK == 0 and min(WINDOW, tp) == WINDOW
    L = cache_k_win.shape[2]
    depth = w_in.shape[0]

    w_kv = _prep_w_kv(w_in)
    tab_p = _rope_tables(jnp.arange(tp, dtype=jnp.int32))
    tab_s = _rope_tables(jnp.tile(PAST_LEN + jnp.arange(ts, dtype=jnp.int32), bs))
    fw = final_norm_w.reshape(1, D_MODEL)

    xp = x_prompt.reshape(tp, D_MODEL)
    xs = x_sample.reshape(bs * ts, D_MODEL)
    ck = cache_k_win.reshape(depth, bs, L, N_KV_A * HD_A)
    cv = cache_v_win.reshape(depth, bs, L, N_KV_A * HD_A)

    kp_l, vp_l, rp_l = [], [], []
    sample_out = ()
    for d in range(depth):
        final = d == depth - 1
        nw = norm_w[d].reshape(1, D_MODEL)
        gnw = ret_norm_w[d].reshape(1, W_B)
        ps, _, w_in_b = _in_proj(xs, nw, w_in, w_kv, d, tab_s, F32, tm=bs * ts, emit_w=True)
        ya3, yb, *sample_out = _mix_sample(ps, attn_sinks[d], gnw, ck, cv, state_ret, d, sample_out, ts)
        ys = jnp.concatenate([ya3.reshape(bs * ts, W_A), yb], axis=1).astype(BF16)
        xs, w_out_b = _out_proj(ys, xs, w_out, d, fw, final)
        p, tail = _in_proj(xp, nw, w_in_b, w_kv, d, tab_p, BF16, tm=1024, emit_w=False)
        xp, r_fin = _mix_prompt(p, xp, attn_sinks[d], gnw, w_out_b, fw, final)
        kp_l.append(jnp.stack([tail[:, 0:HD_A], tail[:, LANES:LANES + HD_A]], axis=1))
        vp_l.append(jnp.stack([tail[:, 2 * LANES:2 * LANES + HD_A], tail[:, 3 * LANES:3 * LANES + HD_A]], axis=1))
        rp_l.append(r_fin)

    keep = min(WINDOW, tp)
    nk, nv, nst = sample_out
    return (xp.reshape(bp, tp, D_MODEL), xs.reshape(bs, ts, D_MODEL),
            jnp.stack(kp_l).reshape(depth, bp, keep, N_KV_A, HD_A),
            jnp.stack(vp_l).reshape(depth, bp, keep, N_KV_A, HD_A),
            jnp.stack(rp_l).reshape(depth, bp, N_HEADS_B, HD_B, HD_B),
            nk.reshape(depth, bs, L, N_KV_A, HD_A), nv.reshape(depth, bs, L, N_KV_A, HD_A), nst)
```

```python
import functools
import math

import jax
import jax.numpy as jnp
from jax import lax
from jax.experimental import pallas as pl
from jax.experimental.pallas import tpu as pltpu

F32 = jnp.float32
BF16 = jnp.bfloat16

D_MODEL = 2048
DEPTH = 2
PAST_LEN = 8192
WINDOW = 128
HD_A = 64
N_HEADS_A = 16
N_KV_A = 2
GROUP_A = N_HEADS_A // N_KV_A
W_A = N_HEADS_A * HD_A
HD_B = 128
N_HEADS_B = 8
W_B = N_HEADS_B * HD_B
RET_CHUNK = 128
ROPE_THETA = 10000.0
EPS = 1e-6

LANES = 128
MXU_WIDTH = 256
SEC_QA, SEC_GA, SEC_QB, SEC_KB, SEC_VB, SEC_GB = 0, 1, 2, 3, 4, 5
IN_TILE = 1024
TILES_PER_SEC = W_A // IN_TILE
KV_TILE = 6 * TILES_PER_SEC
COL_TILE = 4 * N_KV_A * HD_A
KV_BLOCK = KV_TILE * IN_TILE // COL_TILE
D_PROJ_R = (KV_TILE + 1) * IN_TILE
VMEM_LIMIT = 56 * 1024 * 1024

LOG_G = [math.log1p(-(2.0 ** (-5.0 - h))) for h in range(N_HEADS_B)]


def _silu(g):
    return g * (1.0 / (1.0 + jnp.exp(-g)))


EPI_ROPE_A, EPI_ROPE_B, EPI_ROPE_B_SCALED, EPI_PLAIN = 0, 1, 2, 3
N_EPI = 4
_SEC_EPI = (EPI_ROPE_A, EPI_PLAIN, EPI_ROPE_B, EPI_ROPE_B_SCALED, EPI_PLAIN, EPI_PLAIN)


KV_ROPE_GROUPS = N_KV_A
N_W_BLOCKS = IN_TILE // MXU_WIDTH


def _tile_epi(tile, hi):
    sec = tile // TILES_PER_SEC
    t = jnp.int32(EPI_PLAIN if hi else EPI_ROPE_A)
    for s, e in enumerate(_SEC_EPI):
        t = jnp.where(sec == s, e, t)
    return t


def _in_proj_kernel(x_ref, nw_ref, *refs, row_chunk, emit_w):
    n_w = N_W_BLOCKS if emit_w else 1
    w_refs = refs[:n_w]
    wkv_in_ref, ca_ref, sa_ref, cb_ref, sb_ref, o_ref, tail_ref = refs[n_w:n_w + 7]
    w_tile_ref = refs[n_w + 7] if emit_w else w_refs[0]
    wkv_ref = refs[n_w + 8] if emit_w else wkv_in_ref
    h_ref = refs[-1]
    j = pl.program_id(1)

    def weights(e):
        cols = slice(e * MXU_WIDTH, (e + 1) * MXU_WIDTH)
        if emit_w:
            w_tile_ref[:, cols] = w_refs[e][...].astype(BF16)
        return w_tile_ref[:, cols]

    tm = x_ref.shape[0]
    n_chunks = tm // row_chunk

    @pl.when(j == 0)
    def _():
        def body(r, carry):
            rows = pl.ds(pl.multiple_of(r * row_chunk, row_chunk), row_chunk)
            x = x_ref[rows, :]
            ms = jnp.mean(x * x, axis=-1, keepdims=True)
            h_ref[rows, :] = (x * lax.rsqrt(ms + EPS) * nw_ref[...]).astype(BF16)
            return carry
        lax.fori_loop(0, n_chunks, body, 0)

    lane = lax.broadcasted_iota(jnp.int32, (tm, LANES), 1)
    first_half = (lane % HD_A) < HD_A // 2

    def rotate(x, c, s, shift_up, shift_dn):
        partner = jnp.where(first_half, pltpu.roll(x, shift_up, 1), pltpu.roll(x, shift_dn, 1))
        return x * c + partner * s

    epi = _tile_epi(j, 0)
    is_plain = epi == EPI_PLAIN

    @pl.when((j < KV_TILE) & is_plain)
    def _():
        for e in range(N_W_BLOCKS):
            cols = slice(e * MXU_WIDTH, (e + 1) * MXU_WIDTH)
            o_ref[:, cols] = jnp.dot(h_ref[...], weights(e), preferred_element_type=F32).astype(o_ref.dtype)

    @pl.when((j < KV_TILE) & jnp.logical_not(is_plain))
    def _():
        is_a = epi == EPI_ROPE_A
        scale = jnp.where(epi == EPI_ROPE_B_SCALED, HD_B ** -0.5, 1.0).astype(F32)
        c = jnp.where(is_a, ca_ref[...], cb_ref[...]) * scale
        s = jnp.where(is_a, sa_ref[...], sb_ref[...]) * scale
        shift_up = jnp.where(is_a, LANES - HD_A // 2, HD_B // 2)
        shift_dn = jnp.where(is_a, HD_A // 2, HD_B // 2)
        for e in range(N_W_BLOCKS):
            acc = jnp.dot(h_ref[...], weights(e), preferred_element_type=F32)
            for gg in range(MXU_WIDTH // LANES):
                cols = slice(e * MXU_WIDTH + gg * LANES, e * MXU_WIDTH + (gg + 1) * LANES)
                out = rotate(acc[:, gg * LANES:(gg + 1) * LANES], c, s, shift_up, shift_dn)
                o_ref[:, cols] = out.astype(o_ref.dtype)

    @pl.when(j == KV_TILE)
    def _():
        if emit_w:
            low = lax.broadcasted_iota(jnp.int32, (D_MODEL, LANES), 1) < HD_A
            for t in range(2):
                w = wkv_in_ref[:, t * LANES:(t + 1) * LANES]
                w_swapped = pltpu.roll(w, HD_A, 1)
                wkv_ref[:, (2 * t) * LANES:(2 * t + 1) * LANES] = jnp.where(low, w, w_swapped).astype(BF16)
                wkv_ref[:, (2 * t + 1) * LANES:(2 * t + 2) * LANES] = jnp.where(low, w_swapped, w).astype(BF16)
        acc = jnp.dot(h_ref[...], wkv_ref[...], preferred_element_type=F32)
        for g in range(COL_TILE // LANES):
            cols = slice(g * LANES, (g + 1) * LANES)
            out = acc[:, cols]
            if g < KV_ROPE_GROUPS:
                out = rotate(out, ca_ref[...], sa_ref[...], LANES - HD_A // 2, HD_A // 2)
            o_ref[:, cols] = out.astype(o_ref.dtype)
            tail_ref[:, cols] = out[tm - WINDOW:, :]
        o_ref[:, COL_TILE:] = jnp.zeros((tm, IN_TILE - COL_TILE), o_ref.dtype)
        if emit_w:
            w_tile_ref[...] = jnp.zeros(w_tile_ref.shape, BF16)


def _in_proj(x, norm_w, w_src, wkv_src, d, tables, out_dtype, tm, emit_w):
    m = x.shape[0]
    n_tiles = D_PROJ_R // IN_TILE
    assert m == tm or not emit_w
    n_src_blocks = (6 * W_A + 2 * N_KV_A * HD_A) // MXU_WIDTH

    def src_block(j, e):
        sec = j // TILES_PER_SEC
        blk = sec * (W_A // MXU_WIDTH) + jnp.minimum(sec, 1) + (j % TILES_PER_SEC) * N_W_BLOCKS + e
        return jnp.minimum(blk, n_src_blocks - 1)

    if emit_w:
        w_specs = [pl.BlockSpec((None, D_MODEL, MXU_WIDTH), lambda i, j, e=e: (d, 0, src_block(j, e)))
                   for e in range(N_W_BLOCKS)]
        wkv_spec = pl.BlockSpec((None, D_MODEL, MXU_WIDTH), lambda i, j: (d, 0, W_A // MXU_WIDTH))
    else:
        w_specs = [pl.BlockSpec((D_MODEL, IN_TILE), lambda i, j: (0, jnp.minimum(j, KV_TILE - 1)))]
        wkv_spec = pl.BlockSpec((D_MODEL, COL_TILE), lambda i, j: (0, 0))
    tab_spec = pl.BlockSpec((tm, LANES), lambda i, j: (i, 0))

    return pl.pallas_call(
        functools.partial(_in_proj_kernel, row_chunk=min(tm, 256), emit_w=emit_w),
        grid=(m // tm, n_tiles),
        in_specs=[
            pl.BlockSpec((tm, D_MODEL), lambda i, j: (i, 0)),
            pl.BlockSpec((1, D_MODEL), lambda i, j: (0, 0)),
        ] + w_specs + [wkv_spec, tab_spec, tab_spec, tab_spec, tab_spec],
        out_specs=[
            pl.BlockSpec((tm, IN_TILE), lambda i, j: (i, j)),
            pl.BlockSpec((WINDOW, COL_TILE), lambda i, j: (0, 0)),
        ] + ([pl.BlockSpec((D_MODEL, IN_TILE), lambda i, j: (0, j)),
              pl.BlockSpec((D_MODEL, COL_TILE), lambda i, j: (0, 0))] if emit_w else []),
        out_shape=[
            jax.ShapeDtypeStruct((m, D_PROJ_R), out_dtype),
            jax.ShapeDtypeStruct((WINDOW, COL_TILE), F32),
        ] + ([jax.ShapeDtypeStruct((D_MODEL, D_PROJ_R), BF16),
              jax.ShapeDtypeStruct((D_MODEL, COL_TILE), BF16)] if emit_w else []),
        scratch_shapes=[pltpu.VMEM((tm, D_MODEL), BF16)],
        compiler_params=pltpu.CompilerParams(
            dimension_semantics=("arbitrary", "arbitrary"),
            vmem_limit_bytes=VMEM_LIMIT),
        name="in_proj",
    )(x, norm_w, *([w_src] * len(w_specs)), wkv_src, *tables)


def _out_proj_kernel(y_ref, x_ref, w_ref, fw_ref, o_ref, wout_ref, *, final):
    wout_ref[...] = w_ref[...].astype(BF16)
    acc = x_ref[...] + jnp.dot(y_ref[...], wout_ref[...], preferred_element_type=F32)
    if final:
        ms = jnp.mean(acc * acc, axis=-1, keepdims=True)
        acc = acc * lax.rsqrt(ms + EPS) * fw_ref[...]
    o_ref[...] = acc


def _out_proj(y, x, w_all, d, final_w, final):
    m = x.shape[0]
    return pl.pallas_call(
        functools.partial(_out_proj_kernel, final=final),
        grid=(1,),
        in_specs=[
            pl.BlockSpec((m, D_MODEL), lambda i: (0, 0)),
            pl.BlockSpec((m, D_MODEL), lambda i: (0, 0)),
            pl.BlockSpec((None, D_MODEL, D_MODEL), lambda i: (d, 0, 0), pipeline_mode=pl.Buffered(1)),
            pl.BlockSpec((1, D_MODEL), lambda i: (0, 0)),
        ],
        out_specs=[
            pl.BlockSpec((m, D_MODEL), lambda i: (0, 0)),
            pl.BlockSpec((D_MODEL, D_MODEL), lambda i: (0, 0)),
        ],
        out_shape=[
            jax.ShapeDtypeStruct((m, D_MODEL), F32),
            jax.ShapeDtypeStruct((D_MODEL, D_MODEL), BF16),
        ],
        compiler_params=pltpu.CompilerParams(
            dimension_semantics=("arbitrary",),
            vmem_limit_bytes=VMEM_LIMIT),
        name="out_proj",
    )(y, x, w_all, final_w)


def _nt_dot(a, b):
    return lax.dot_general(a, b, (((1,), (1,)), ((), ())), preferred_element_type=F32)


def _tn_dot(a, b):
    return lax.dot_general(a, b, (((0,), (0,)), ((), ())), preferred_element_type=F32)


MIX_CHUNKS = 4
OUT_PIECES = D_MODEL // MXU_WIDTH
PIECES_PER_CHUNK = OUT_PIECES // MIX_CHUNKS


def _mix_prompt_kernel(sink_ref, qa_ref, ga_ref, qb_ref, kb_ref, vb_ref, gb_ref,
                       kv_ref, gnw_ref, x_ref, wo_ref, fw_ref, xo_ref, r_ref,
                       dmask_ref, qdec_ref, kdec_ref, kvp_ref,
                       qs_ref, s_ref, p_ref, o_ref, scb_ref, inter_ref, y_ref, *, final):
    step = pl.program_id(0)
    last_step = pl.num_programs(0) - 1
    slot = step % 2
    prev_slot = 1 - slot
    C = RET_CHUNK
    row = lax.broadcasted_iota(jnp.int32, (C, LANES), 0)
    lane = lax.broadcasted_iota(jnp.int32, (C, LANES), 1)

    @pl.when(step == 0)
    def _():
        r_ref[...] = jnp.zeros(r_ref.shape, F32)
        kvp_ref[...] = jnp.zeros(kvp_ref.shape, kvp_ref.dtype)
        y_ref[1] = jnp.zeros(y_ref.shape[1:], y_ref.dtype)
        rowf = row.astype(F32)
        diff = rowf - lane.astype(F32)
        for h in range(N_HEADS_B):
            dmask_ref[h] = jnp.where(diff >= 0, jnp.exp(LOG_G[h] * jnp.maximum(diff, 0.0)), 0.0)
            qdec_ref[h] = jnp.exp(LOG_G[h] * (rowf + 1.0))
            kdec_ref[h] = jnp.exp(LOG_G[h] * (C - 1.0 - rowf))

    tri = lane <= row
    lo = lane < HD_A
    neg_inf = jnp.float32(-jnp.inf)
    half_group = GROUP_A // 2
    kcol = lambda g: slice(g * LANES, (g + 1) * LANES)
    vcol = lambda g: slice((N_KV_A + g) * LANES, (N_KV_A + g + 1) * LANES)
    pair_col = lambda g, pi: slice((g * half_group + pi) * LANES, (g * half_group + pi + 1) * LANES)
    head_rows = lambda hh: slice(hh * C, (hh + 1) * C)

    def out_piece(e):
        cols = slice(e * MXU_WIDTH, (e + 1) * MXU_WIDTH)
        xo_ref[:, cols] = x_ref[:, cols] + jnp.dot(y_ref[prev_slot], wo_ref[:, cols],
                                                   preferred_element_type=F32)

    def finish_out():
        if final:
            acc = xo_ref[...]
            ms = jnp.mean(acc * acc, axis=-1, keepdims=True)
            xo_ref[...] = acc * lax.rsqrt(ms + EPS) * fw_ref[...]

    def chunk(cc):
        rows = slice(cc * C, (cc + 1) * C)
        has_prev = step * MIX_CHUNKS + cc > 0
        if cc == 0:
            kvp_ref[C:, :] = kv_ref[rows, :]
            kv2 = lambda cols: kvp_ref[:, cols]
        else:
            kv2 = lambda cols: kv_ref[(cc - 1) * C:(cc + 1) * C, cols]

        for g in range(N_KV_A):
            for pi in range(half_group):
                q_pair = qa_ref[rows, pair_col(g, pi)].astype(F32)
                qs_ref[g, head_rows(2 * pi), :] = jnp.where(lo, q_pair, 0.0).astype(BF16)
                qs_ref[g, head_rows(2 * pi + 1), :] = jnp.where(lo, 0.0, q_pair).astype(BF16)
            s_ref[g] = _nt_dot(qs_ref[g], kv2(kcol(g)))

        out_piece(cc * PIECES_PER_CHUNK)

        for h in range(N_HEADS_B):
            hc = slice(h * HD_B, (h + 1) * HD_B)
            q, k, v = qb_ref[rows, hc], kb_ref[rows, hc], vb_ref[rows, hc]
            scb_ref[h] = (_nt_dot(q, k) * dmask_ref[h]).astype(BF16)
            r = r_ref[h]
            inter_ref[h] = jnp.dot(q, r.astype(BF16), preferred_element_type=F32) * qdec_ref[h]
            kd = (k.astype(F32) * kdec_ref[h]).astype(BF16)
            r_ref[h] = math.exp(LOG_G[h] * C) * r + _tn_dot(kd, v)

        for g in range(N_KV_A):
            for hh in range(GROUP_A):
                hr = head_rows(hh)
                s_prev = jnp.where(has_prev, s_ref[g, hr, :LANES], neg_inf)
                s = jnp.where(tri, s_ref[g, hr, LANES:], s_prev) * (HD_A ** -0.5)
                sink = sink_ref[g * GROUP_A + hh]
                m = jnp.maximum(jnp.max(s, axis=-1, keepdims=True), sink)
                e = jnp.exp(s - m)
                denom = jnp.sum(e, axis=-1, keepdims=True) + jnp.exp(sink - m)
                p = e / denom
                p_ref[g, hr, :LANES] = jnp.where(tri, 0.0, p).astype(BF16)
                p_ref[g, hr, LANES:] = jnp.where(tri, p, 0.0).astype(BF16)

        for g in range(N_KV_A):
            o_ref[g] = jnp.dot(p_ref[g], kv2(vcol(g)), preferred_element_type=F32)

        for e in range(cc * PIECES_PER_CHUNK + 1, (cc + 1) * PIECES_PER_CHUNK):
            out_piece(e)

        for h in range(N_HEADS_B):
            hc = slice(h * HD_B, (h + 1) * HD_B)
            o = jnp.dot(scb_ref[h], vb_ref[rows, hc], preferred_element_type=F32) + inter_ref[h]
            mu = jnp.mean(o, axis=-1, keepdims=True)
            d = o - mu
            var = jnp.mean(d * d, axis=-1, keepdims=True)
            yh = d * lax.rsqrt(var + EPS) * gnw_ref[:, hc] * _silu(gb_ref[rows, hc].astype(F32))
            y_ref[slot, rows, W_A + h * HD_B: W_A + (h + 1) * HD_B] = yh.astype(y_ref.dtype)

        for g in range(N_KV_A):
            for pi in range(half_group):
                pc = pair_col(g, pi)
                o_pair = jnp.where(lo, o_ref[g, head_rows(2 * pi), :], o_ref[g, head_rows(2 * pi + 1), :])
                y_ref[slot, rows, pc] = (o_pair * _silu(ga_ref[rows, pc].astype(F32))).astype(y_ref.dtype)

        if cc == MIX_CHUNKS - 1:
            kvp_ref[:C, :] = kv_ref[rows, :]

    @pl.when(step < last_step)
    def _():
        for cc in range(MIX_CHUNKS):
            chunk(cc)
        finish_out()

    @pl.when(step == last_step)
    def _():
        for e in range(OUT_PIECES):
            out_piece(e)
        finish_out()


def _mix_prompt(p, x, sinks, gn_w, w_out_b, final_w, final):
    t = p.shape[0]
    C = RET_CHUNK
    rows = MIX_CHUNKS * C
    n_blocks = t // rows
    cur = lambda c: jnp.minimum(c, n_blocks - 1)
    prev = lambda c: jnp.maximum(c - 1, 0)
    sec_spec = lambda s: pl.BlockSpec((rows, W_A), lambda c, s=s: (cur(c), s))
    return pl.pallas_call(
        functools.partial(_mix_prompt_kernel, final=final),
        grid=(n_blocks + 1,),
        in_specs=[
            pl.BlockSpec(memory_space=pltpu.SMEM),
            sec_spec(SEC_QA), sec_spec(SEC_GA), sec_spec(SEC_QB),
            sec_spec(SEC_KB), sec_spec(SEC_VB), sec_spec(SEC_GB),
            pl.BlockSpec((rows, COL_TILE), lambda c: (cur(c), KV_BLOCK)),
            pl.BlockSpec((1, W_B), lambda c: (0, 0)),
            pl.BlockSpec((rows, D_MODEL), lambda c: (prev(c), 0)),
            pl.BlockSpec((D_MODEL, D_MODEL), lambda c: (0, 0), pipeline_mode=pl.Buffered(1)),
            pl.BlockSpec((1, D_MODEL), lambda c: (0, 0)),
        ],
        out_specs=[
            pl.BlockSpec((rows, D_MODEL), lambda c: (prev(c), 0)),
            pl.BlockSpec((N_HEADS_B, HD_B, HD_B), lambda c: (0, 0, 0)),
        ],
        out_shape=[
            jax.ShapeDtypeStruct((t, D_MODEL), F32),
            jax.ShapeDtypeStruct((N_HEADS_B, HD_B, HD_B), F32),
        ],
        scratch_shapes=[pltpu.VMEM((N_HEADS_B, C, LANES), F32)] * 3 + [
            pltpu.VMEM((2 * C, COL_TILE), BF16),
            pltpu.VMEM((N_KV_A, GROUP_A * C, LANES), BF16),
            pltpu.VMEM((N_KV_A, GROUP_A * C, 2 * LANES), F32),
            pltpu.VMEM((N_KV_A, GROUP_A * C, 2 * LANES), BF16),
            pltpu.VMEM((N_KV_A, GROUP_A * C, LANES), F32),
            pltpu.VMEM((N_HEADS_B, C, HD_B), BF16),
            pltpu.VMEM((N_HEADS_B, C, HD_B), F32),
            pltpu.VMEM((2, rows, D_MODEL), BF16),
        ],
        compiler_params=pltpu.CompilerParams(
            dimension_semantics=("arbitrary",),
            vmem_limit_bytes=VMEM_LIMIT),
        name="mix_prompt",
    )(sinks, p, p, p, p, p, p, p, gn_w, x, w_out_b, final_w)


SAMPLE_BB = 16
LANE_BATCH = LANES // 4


def _mix_sample_kernel(sinkrow_ref, q3_ref, ga3_ref, qb_ref, kb_ref, vb_ref, gb_ref, kvn_ref,
                       kbt_ref, vbg_ref, ck_ref, cv_ref, st_ref, gnw_ref, *rest, t_new):
    ya3_ref, yb_ref, nk_ref, nv_ref, nst_ref, s_ref, p_ref, vall_ref, kt_ref, x_ref, sc_ref = rest[-11:]
    i = pl.program_id(0)
    T = t_new
    BB = SAMPLE_BB
    L = ck_ref.shape[1]
    n_keys = L + T
    QR = T * GROUP_A
    R = BB * T
    lo = lax.broadcasted_iota(jnp.int32, (T, LANES), 1) < HD_A
    head_sl = lambda g: slice(g * GROUP_A, (g + 1) * GROUP_A)
    lane_sl = lambda g: slice(g * HD_A, (g + 1) * HD_A)
    tok_rows = lambda bb: slice(bb * T, (bb + 1) * T)


    for bb in range(BB):
        kvn = kvn_ref[tok_rows(bb), :]
        k_new = jnp.where(lo, kvn[:, 0:LANES], kvn[:, LANES:2 * LANES])
        v_new = jnp.where(lo, kvn[:, 2 * LANES:3 * LANES], kvn[:, 3 * LANES:4 * LANES])
        k_all = jnp.concatenate([ck_ref[bb], k_new], axis=0)
        v_all = jnp.concatenate([cv_ref[bb], v_new], axis=0)
        nk_ref[bb] = k_all[T:, :]
        nv_ref[bb] = v_all[T:, :]
        vall_ref[bb] = v_all.astype(BF16)
        k_all_b = k_all.astype(BF16)
        for g in range(N_KV_A):
            q = q3_ref[tok_rows(bb), head_sl(g), :].reshape(T * GROUP_A, HD_A).astype(BF16)
            s_ref[g, bb * QR:(bb + 1) * QR, :] = _nt_dot(q, k_all_b[:, lane_sl(g)])

    lane_t = (lax.broadcasted_iota(jnp.int32, (1, LANES), 1) % T).astype(F32)
    lane_b = lax.broadcasted_iota(jnp.int32, (HD_B, LANES), 1) // T
    slot0 = (i % (LANE_BATCH // BB)) * BB
    for h in range(N_HEADS_B):
        hc = slice(h * HD_B, (h + 1) * HD_B)
        lg = LOG_G[h]
        kdec = jnp.exp(lg * (T - 1.0 - lane_t))
        kt_dec = kbt_ref[hc, :] * kdec
        for bb in range(BB):
            kt_ref[h, bb * HD_B:(bb + 1) * HD_B, :] = jnp.where(lane_b == slot0 + bb, kt_dec, 0.0).astype(BF16)
        u = jnp.dot(kt_ref[h], vbg_ref[:, hc].astype(BF16), preferred_element_type=F32)
        for bb in range(BB):
            r = st_ref[bb, h]
            pr = slice((bb // 2) * 2 * T, (bb // 2 + 1) * 2 * T)
            x_ref[bb * N_HEADS_B + h] = jnp.dot(qb_ref[pr, hc].astype(BF16), r.astype(BF16),
                                                 preferred_element_type=F32)
            nst_ref[bb, h] = math.exp(lg * T) * r + u[bb * HD_B:(bb + 1) * HD_B, :]

    ri = lax.broadcasted_iota(jnp.int32, (R, R), 0)
    ci = lax.broadcasted_iota(jnp.int32, (R, R), 1)
    same_b = (ri // T) == (ci // T)
    dt = (ri % T - ci % T).astype(F32)
    keep = same_b & (dt >= 0)
    for h in range(N_HEADS_B):
        hc = slice(h * HD_B, (h + 1) * HD_B)
        dmask = jnp.where(keep, jnp.exp(LOG_G[h] * jnp.maximum(dt, 0.0)), 0.0)
        sc = _nt_dot(qb_ref[:, hc].astype(BF16), kb_ref[:, hc].astype(BF16)) * dmask
        sc_ref[h] = sc.astype(BF16)

    rows_q = (lax.broadcasted_iota(jnp.int32, (BB * QR, n_keys), 0) % QR) // GROUP_A
    key = lax.broadcasted_iota(jnp.int32, (BB * QR, n_keys), 1)
    delta = jnp.where(key < L, rows_q + L - key, rows_q - (key - L))
    valid = (delta >= 0) & (delta < WINDOW)
    for g in range(N_KV_A):
        s = jnp.where(valid, s_ref[g] * (HD_A ** -0.5), -jnp.inf)
        sink = sinkrow_ref[g]
        m = jnp.maximum(jnp.max(s, axis=-1, keepdims=True), sink)
        e = jnp.exp(s - m)
        denom = jnp.sum(e, axis=-1, keepdims=True) + jnp.exp(sink - m)
        p_ref[g] = (e / denom).astype(BF16)

    tposf = (lax.broadcasted_iota(jnp.int32, (R, 1), 0) % T).astype(F32)
    upper = (lax.broadcasted_iota(jnp.int32, (2 * T, 1), 0) < T)
    for h in range(N_HEADS_B):
        hc = slice(h * HD_B, (h + 1) * HD_B)
        intra = jnp.dot(sc_ref[h], vb_ref[:, hc].astype(BF16), preferred_element_type=F32)
        inter = jnp.concatenate(
            [jnp.where(upper, x_ref[(2 * pair) * N_HEADS_B + h], x_ref[(2 * pair + 1) * N_HEADS_B + h])
             for pair in range(BB // 2)], axis=0)
        o = intra + inter * jnp.exp(LOG_G[h] * (tposf + 1.0))
        mu = jnp.mean(o, axis=-1, keepdims=True)
        d = o - mu
        var = jnp.mean(d * d, axis=-1, keepdims=True)
        yb_ref[:, hc] = d * lax.rsqrt(var + EPS) * gnw_ref[:, hc] * _silu(gb_ref[:, hc])

    for bb in range(BB):
        for g in range(N_KV_A):
            o = jnp.dot(p_ref[g, bb * QR:(bb + 1) * QR, :], vall_ref[bb, :, lane_sl(g)],
                        preferred_element_type=F32)
            gate = _silu(ga3_ref[tok_rows(bb), head_sl(g), :].reshape(QR, HD_A))
            ya3_ref[tok_rows(bb), head_sl(g), :] = (o * gate).reshape(T, GROUP_A, HD_A)


def _mix_sample(ps, sinks, gn_w, cache_k, cache_v, state, d, prev_out, t_new):
    m = ps.shape[0]
    nb = m // t_new
    L = cache_k.shape[2]
    BB = SAMPLE_BB
    n_alias = len(prev_out)
    n_in = 14
    rows = BB * t_new
    qa = ps[:, SEC_QA * W_A:(SEC_QA + 1) * W_A].reshape(m, N_HEADS_A, HD_A)
    ga = ps[:, SEC_GA * W_A:(SEC_GA + 1) * W_A].reshape(m, N_HEADS_A, HD_A)
    kbt = ps[:, SEC_KB * W_A:(SEC_KB + 1) * W_A].T
    sinkrow = jnp.tile(sinks.reshape(N_KV_A, 1, GROUP_A), (1, BB * t_new, 1)).reshape(N_KV_A, -1, 1)
    sec_spec = lambda s: pl.BlockSpec((rows, W_A), lambda i, s=s: (i, s))
    steps_per_lane_group = LANE_BATCH // BB
    return pl.pallas_call(
        functools.partial(_mix_sample_kernel, t_new=t_new),
        grid=(nb // BB,),
        in_specs=[
            pl.BlockSpec((N_KV_A, BB * t_new * GROUP_A, 1), lambda i: (0, 0, 0)),
            pl.BlockSpec((rows, N_HEADS_A, HD_A), lambda i: (i, 0, 0)),
            pl.BlockSpec((rows, N_HEADS_A, HD_A), lambda i: (i, 0, 0)),
            sec_spec(SEC_QB), sec_spec(SEC_KB), sec_spec(SEC_VB), sec_spec(SEC_GB),
            pl.BlockSpec((rows, COL_TILE), lambda i: (i, KV_BLOCK)),
            pl.BlockSpec((W_B, LANES), lambda i: (0, i // steps_per_lane_group)),
            pl.BlockSpec((LANES, W_B), lambda i: (i // steps_per_lane_group, SEC_VB)),
            pl.BlockSpec((None, BB, L, LANES), lambda i: (d, i, 0, 0)),
            pl.BlockSpec((None, BB, L, LANES), lambda i: (d, i, 0, 0)),
            pl.BlockSpec((None, BB, N_HEADS_B, HD_B, HD_B), lambda i: (d, i, 0, 0, 0)),
            pl.BlockSpec((1, W_B), lambda i: (0, 0)),
        ] + [pl.BlockSpec(memory_space=pl.ANY)] * n_alias,
        out_specs=[
            pl.BlockSpec((rows, N_HEADS_A, HD_A), lambda i: (i, 0, 0)),
            pl.BlockSpec((rows, W_B), lambda i: (i, 0)),
            pl.BlockSpec((None, BB, L, LANES), lambda i: (d, i, 0, 0)),
            pl.BlockSpec((None, BB, L, LANES), lambda i: (d, i, 0, 0)),
            pl.BlockSpec((None, BB, N_HEADS_B, HD_B, HD_B), lambda i: (d, i, 0, 0, 0)),
        ],
        out_shape=[
            jax.ShapeDtypeStruct((m, N_HEADS_A, HD_A), F32),
            jax.ShapeDtypeStruct((m, W_B), F32),
            jax.ShapeDtypeStruct(cache_k.shape, F32),
            jax.ShapeDtypeStruct(cache_v.shape, F32),
            jax.ShapeDtypeStruct(state.shape, F32),
        ],
        input_output_aliases={n_in + a: 2 + a for a in range(n_alias)},
        scratch_shapes=[
            pltpu.VMEM((N_KV_A, BB * t_new * GROUP_A, L + t_new), F32),
            pltpu.VMEM((N_KV_A, BB * t_new * GROUP_A, L + t_new), BF16),
            pltpu.VMEM((BB, L + t_new, LANES), BF16),
            pltpu.VMEM((N_HEADS_B, BB * HD_B, LANES), BF16),
            pltpu.VMEM((BB * N_HEADS_B, 2 * t_new, HD_B), F32),
            pltpu.VMEM((N_HEADS_B, rows, rows), BF16),
        ],
        compiler_params=pltpu.CompilerParams(
            dimension_semantics=("arbitrary",),
            vmem_limit_bytes=VMEM_LIMIT),
        name="mix_sample",
    )(sinkrow, qa, ga, ps, ps, ps, ps, ps, kbt, ps, cache_k, cache_v, state, gn_w, *prev_out)


def _rope_tables(pos):
    def tab(hd):
        half = hd // 2
        inv = ROPE_THETA ** (-jnp.arange(half, dtype=F32) / half)
        ang = pos.astype(F32)[:, None] * inv[None, :]
        cos, sin = jnp.cos(ang), jnp.sin(ang)
        reps = LANES // hd
        return (jnp.tile(jnp.concatenate([cos, cos], axis=1), (1, reps)),
                jnp.tile(jnp.concatenate([-sin, sin], axis=1), (1, reps)))
    return tab(HD_A) + tab(HD_B)


def kernel(x_prompt, x_sample, cache_k_win, cache_v_win, state_ret, norm_w, w_in, attn_sinks,
           ret_norm_w, w_out, final_norm_w):
    bp, tp, _ = x_prompt.shape
    bs, ts, _ = x_sample.shape
    assert bp == 1 and tp % RET_CHUNK == 0 and min(WINDOW, tp) == WINDOW
    L = cache_k_win.shape[2]
    depth = w_in.shape[0]

    tab_p = _rope_tables(jnp.arange(tp, dtype=jnp.int32))
    tab_s = _rope_tables(jnp.tile(PAST_LEN + jnp.arange(ts, dtype=jnp.int32), bs))
    fw = final_norm_w.reshape(1, D_MODEL)

    xp = x_prompt.reshape(tp, D_MODEL)
    xs = x_sample.reshape(bs * ts, D_MODEL)
    ck = cache_k_win.reshape(depth, bs, L, N_KV_A * HD_A)
    cv = cache_v_win.reshape(depth, bs, L, N_KV_A * HD_A)

    kp_l, vp_l, rp_l = [], [], []
    sample_out = ()
    for d in range(depth):
        final = d == depth - 1
        nw = norm_w[d].reshape(1, D_MODEL)
        gnw = ret_norm_w[d].reshape(1, W_B)
        ps, _, w_in_b, w_kv_b = _in_proj(xs, nw, w_in, w_in, d, tab_s, F32, tm=bs * ts, emit_w=True)
        ya3, yb, *sample_out = _mix_sample(ps, attn_sinks[d], gnw, ck, cv, state_ret, d, sample_out, ts)
        ys = jnp.concatenate([ya3.reshape(bs * ts, W_A), yb], axis=1).astype(BF16)
        xs, w_out_b = _out_proj(ys, xs, w_out, d, fw, final)
        p, tail = _in_proj(xp, nw, w_in_b, w_kv_b, d, tab_p, BF16, tm=1024, emit_w=False)
        xp, r_fin = _mix_prompt(p, xp, attn_sinks[d], gnw, w_out_b, fw, final)
        kp_l.append(jnp.stack([tail[:, 0:HD_A], tail[:, LANES:LANES + HD_A]], axis=1))
        vp_l.append(jnp.stack([tail[:, 2 * LANES:2 * LANES + HD_A], tail[:, 3 * LANES:3 * LANES + HD_A]], axis=1))
        rp_l.append(r_fin)

    keep = min(WINDOW, tp)
    nk, nv, nst = sample_out
    return (xp.reshape(bp, tp, D_MODEL), xs.reshape(bs, ts, D_MODEL),
            jnp.stack(kp_l).reshape(depth, bp, keep, N_KV_A, HD_A),
            jnp.stack(vp_l).reshape(depth, bp, keep, N_KV_A, HD_A),
            jnp.stack(rp_l).reshape(depth, bp, N_HEADS_B, HD_B, HD_B),
            nk.reshape(depth, bs, L, N_KV_A, HD_A), nv.reshape(depth, bs, L, N_KV_A, HD_A), nst)
```

```python
import functools
import math

import jax
import jax.numpy as jnp
from jax import lax
from jax.experimental import pallas as pl
from jax.experimental.pallas import tpu as pltpu

F32 = jnp.float32
BF16 = jnp.bfloat16

D_MODEL = 2048
DEPTH = 2
PAST_LEN = 8192
WINDOW = 128
HD_A = 64
N_HEADS_A = 16
N_KV_A = 2
GROUP_A = N_HEADS_A // N_KV_A
W_A = N_HEADS_A * HD_A
HD_B = 128
N_HEADS_B = 8
W_B = N_HEADS_B * HD_B
RET_CHUNK = 128
ROPE_THETA = 10000.0
EPS = 1e-6

LANES = 128
MXU_WIDTH = 256
SEC_QA, SEC_GA, SEC_QB, SEC_KB, SEC_VB, SEC_GB = 0, 1, 2, 3, 4, 5
IN_TILE = 1024
TILES_PER_SEC = W_A // IN_TILE
KV_TILE = 6 * TILES_PER_SEC
COL_TILE = 2 * N_KV_A * HD_A
KV_BLOCK = KV_TILE * IN_TILE // COL_TILE
D_PROJ_R = (KV_TILE + 1) * IN_TILE
VMEM_LIMIT = 56 * 1024 * 1024

LOG_G = [math.log1p(-(2.0 ** (-5.0 - h))) for h in range(N_HEADS_B)]


def _silu(g):
    return g * (1.0 / (1.0 + jnp.exp(-g)))


EPI_ROPE_A, EPI_ROPE_B, EPI_ROPE_B_SCALED, EPI_PLAIN = 0, 1, 2, 3
N_EPI = 4
_SEC_EPI = (EPI_ROPE_A, EPI_PLAIN, EPI_ROPE_B, EPI_ROPE_B_SCALED, EPI_PLAIN, EPI_PLAIN)


KV_ROPE_GROUPS = N_KV_A * HD_A // LANES
N_W_BLOCKS = IN_TILE // MXU_WIDTH


def _step_tile(step):
    return jnp.where(step == 0, KV_TILE, step - 1)


def _tile_epi(tile, hi):
    sec = tile // TILES_PER_SEC
    t = jnp.int32(EPI_PLAIN if hi else EPI_ROPE_A)
    for s, e in enumerate(_SEC_EPI):
        t = jnp.where(sec == s, e, t)
    return t


def _in_proj_kernel(x_ref, nw_ref, *refs, row_chunk, emit_w):
    n_w = N_W_BLOCKS if emit_w else 1
    w_refs = refs[:n_w]
    wkv_in_ref, ca_ref, sa_ref, cb_ref, sb_ref, o_ref, tail_ref = refs[n_w:n_w + 7]
    w_tile_ref = refs[n_w + 7] if emit_w else w_refs[0]
    wkv_ref = refs[n_w + 8] if emit_w else wkv_in_ref
    h_ref = refs[-1]
    step = pl.program_id(1)
    j = _step_tile(step)

    def weights(e):
        cols = slice(e * MXU_WIDTH, (e + 1) * MXU_WIDTH)
        if emit_w:
            w_tile_ref[:, cols] = w_refs[e][...].astype(BF16)
        return w_tile_ref[:, cols]

    tm = x_ref.shape[0]
    n_chunks = tm // row_chunk

    @pl.when(step == 0)
    def _():
        def body(r, carry):
            rows = pl.ds(pl.multiple_of(r * row_chunk, row_chunk), row_chunk)
            x = x_ref[rows, :]
            ms = jnp.mean(x * x, axis=-1, keepdims=True)
            h_ref[rows, :] = (x * lax.rsqrt(ms + EPS) * nw_ref[...]).astype(BF16)
            return carry
        lax.fori_loop(0, n_chunks, body, 0)

    lane = lax.broadcasted_iota(jnp.int32, (tm, LANES), 1)
    first_half = (lane % HD_A) < HD_A // 2

    def rotate(x, c, s, shift_up, shift_dn):
        partner = jnp.where(first_half, pltpu.roll(x, shift_up, 1), pltpu.roll(x, shift_dn, 1))
        return x * c + partner * s

    epi = _tile_epi(j, 0)
    is_plain = epi == EPI_PLAIN

    @pl.when((j < KV_TILE) & is_plain)
    def _():
        for e in range(N_W_BLOCKS):
            cols = slice(e * MXU_WIDTH, (e + 1) * MXU_WIDTH)
            o_ref[:, cols] = jnp.dot(h_ref[...], weights(e), preferred_element_type=F32).astype(o_ref.dtype)

    @pl.when((j < KV_TILE) & jnp.logical_not(is_plain))
    def _():
        is_a = epi == EPI_ROPE_A
        scale = jnp.where(epi == EPI_ROPE_B_SCALED, HD_B ** -0.5, 1.0).astype(F32)
        c = jnp.where(is_a, ca_ref[...], cb_ref[...]) * scale
        s = jnp.where(is_a, sa_ref[...], sb_ref[...]) * scale
        shift_up = jnp.where(is_a, LANES - HD_A // 2, HD_B // 2)
        shift_dn = jnp.where(is_a, HD_A // 2, HD_B // 2)
        for e in range(N_W_BLOCKS):
            acc = jnp.dot(h_ref[...], weights(e), preferred_element_type=F32)
            for gg in range(MXU_WIDTH // LANES):
                cols = slice(e * MXU_WIDTH + gg * LANES, e * MXU_WIDTH + (gg + 1) * LANES)
                out = rotate(acc[:, gg * LANES:(gg + 1) * LANES], c, s, shift_up, shift_dn)
                o_ref[:, cols] = out.astype(o_ref.dtype)

    @pl.when(j == KV_TILE)
    def _():
        if emit_w:
            wkv_ref[...] = wkv_in_ref[...].astype(BF16)
        acc = jnp.dot(h_ref[...], wkv_ref[...], preferred_element_type=F32)
        for g in range(COL_TILE // LANES):
            cols = slice(g * LANES, (g + 1) * LANES)
            out = acc[:, cols]
            if g < KV_ROPE_GROUPS:
                out = rotate(out, ca_ref[...], sa_ref[...], LANES - HD_A // 2, HD_A // 2)
            o_ref[:, cols] = out.astype(o_ref.dtype)
            tail_ref[:, cols] = out[tm - WINDOW:, :]
        o_ref[:, COL_TILE:] = jnp.zeros((tm, IN_TILE - COL_TILE), o_ref.dtype)
        if emit_w:
            w_tile_ref[...] = jnp.zeros(w_tile_ref.shape, BF16)


def _in_proj(x, norm_w, w_src, wkv_src, d, tables, out_dtype, tm, emit_w):
    m = x.shape[0]
    n_tiles = D_PROJ_R // IN_TILE
    assert m == tm or not emit_w
    n_src_blocks = (6 * W_A + 2 * N_KV_A * HD_A) // MXU_WIDTH

    def src_block(j, e):
        sec = j // TILES_PER_SEC
        blk = sec * (W_A // MXU_WIDTH) + jnp.minimum(sec, 1) + (j % TILES_PER_SEC) * N_W_BLOCKS + e
        return jnp.minimum(blk, n_src_blocks - 1)

    if emit_w:
        w_specs = [pl.BlockSpec((None, D_MODEL, MXU_WIDTH),
                                lambda i, s, e=e: (d, 0, src_block(_step_tile(s), e)))
                   for e in range(N_W_BLOCKS)]
        wkv_spec = pl.BlockSpec((None, D_MODEL, MXU_WIDTH), lambda i, j: (d, 0, W_A // MXU_WIDTH))
    else:
        w_specs = [pl.BlockSpec((D_MODEL, IN_TILE),
                                lambda i, s: (0, jnp.minimum(_step_tile(s), KV_TILE - 1)))]
        wkv_spec = pl.BlockSpec((D_MODEL, COL_TILE), lambda i, j: (0, 0))
    tab_spec = pl.BlockSpec((tm, LANES), lambda i, j: (i, 0))

    return pl.pallas_call(
        functools.partial(_in_proj_kernel, row_chunk=min(tm, 256), emit_w=emit_w),
        grid=(m // tm, n_tiles),
        in_specs=[
            pl.BlockSpec((tm, D_MODEL), lambda i, j: (i, 0)),
            pl.BlockSpec((1, D_MODEL), lambda i, j: (0, 0)),
        ] + w_specs + [wkv_spec, tab_spec, tab_spec, tab_spec, tab_spec],
        out_specs=[
            pl.BlockSpec((tm, IN_TILE), lambda i, s: (i, _step_tile(s))),
            pl.BlockSpec((WINDOW, COL_TILE), lambda i, j: (0, 0)),
        ] + ([pl.BlockSpec((D_MODEL, IN_TILE), lambda i, s: (0, _step_tile(s))),
              pl.BlockSpec((D_MODEL, COL_TILE), lambda i, j: (0, 0))] if emit_w else []),
        out_shape=[
            jax.ShapeDtypeStruct((m, D_PROJ_R), out_dtype),
            jax.ShapeDtypeStruct((WINDOW, COL_TILE), F32),
        ] + ([jax.ShapeDtypeStruct((D_MODEL, D_PROJ_R), BF16),
              jax.ShapeDtypeStruct((D_MODEL, COL_TILE), BF16)] if emit_w else []),
        scratch_shapes=[pltpu.VMEM((tm, D_MODEL), BF16)],
        compiler_params=pltpu.CompilerParams(
            dimension_semantics=("arbitrary", "arbitrary"),
            vmem_limit_bytes=VMEM_LIMIT),
        name="in_proj",
    )(x, norm_w, *([w_src] * len(w_specs)), wkv_src, *tables)


def _out_proj_kernel(y_ref, x_ref, w_ref, fw_ref, o_ref, wout_ref, *, final):
    wout_ref[...] = w_ref[...].astype(BF16)
    acc = x_ref[...] + jnp.dot(y_ref[...], wout_ref[...], preferred_element_type=F32)
    if final:
        ms = jnp.mean(acc * acc, axis=-1, keepdims=True)
        acc = acc * lax.rsqrt(ms + EPS) * fw_ref[...]
    o_ref[...] = acc


def _out_proj(y, x, w_all, d, final_w, final):
    m = x.shape[0]
    return pl.pallas_call(
        functools.partial(_out_proj_kernel, final=final),
        grid=(1,),
        in_specs=[
            pl.BlockSpec((m, D_MODEL), lambda i: (0, 0)),
            pl.BlockSpec((m, D_MODEL), lambda i: (0, 0)),
            pl.BlockSpec((None, D_MODEL, D_MODEL), lambda i: (d, 0, 0), pipeline_mode=pl.Buffered(1)),
            pl.BlockSpec((1, D_MODEL), lambda i: (0, 0)),
        ],
        out_specs=[
            pl.BlockSpec((m, D_MODEL), lambda i: (0, 0)),
            pl.BlockSpec((D_MODEL, D_MODEL), lambda i: (0, 0)),
        ],
        out_shape=[
            jax.ShapeDtypeStruct((m, D_MODEL), F32),
            jax.ShapeDtypeStruct((D_MODEL, D_MODEL), BF16),
        ],
        compiler_params=pltpu.CompilerParams(
            dimension_semantics=("arbitrary",),
            vmem_limit_bytes=VMEM_LIMIT),
        name="out_proj",
    )(y, x, w_all, final_w)


def _nt_dot(a, b):
    return lax.dot_general(a, b, (((1,), (1,)), ((), ())), preferred_element_type=F32)


def _tn_dot(a, b):
    return lax.dot_general(a, b, (((0,), (0,)), ((), ())), preferred_element_type=F32)


MIX_CHUNKS = 4
OUT_PIECES = D_MODEL // MXU_WIDTH
PIECES_PER_CHUNK = OUT_PIECES // MIX_CHUNKS


def _mix_prompt_kernel(sink_ref, qa_ref, ga_ref, qb_ref, kb_ref, vb_ref, gb_ref,
                       kv_ref, gnw_ref, x_ref, wo_ref, fw_ref, xo_ref, r_ref,
                       dmask_ref, qdec_ref, kdec_ref, kvd_ref,
                       qs_ref, s_ref, p_ref, o_ref, scb_ref, inter_ref, y_ref, *, final):
    step = pl.program_id(0)
    last_step = pl.num_programs(0) - 1
    slot = step % 2
    prev_slot = 1 - slot
    C = RET_CHUNK
    row = lax.broadcasted_iota(jnp.int32, (C, LANES), 0)
    lane = lax.broadcasted_iota(jnp.int32, (C, LANES), 1)

    @pl.when(step == 0)
    def _():
        r_ref[...] = jnp.zeros(r_ref.shape, F32)
        kvd_ref[...] = jnp.zeros(kvd_ref.shape, kvd_ref.dtype)
        y_ref[1] = jnp.zeros(y_ref.shape[1:], y_ref.dtype)
        rowf = row.astype(F32)
        diff = rowf - lane.astype(F32)
        for h in range(N_HEADS_B):
            dmask_ref[h] = jnp.where(diff >= 0, jnp.exp(LOG_G[h] * jnp.maximum(diff, 0.0)), 0.0)
            qdec_ref[h] = jnp.exp(LOG_G[h] * (rowf + 1.0))
            kdec_ref[h] = jnp.exp(LOG_G[h] * (C - 1.0 - rowf))

    tri = lane <= row
    lo = lane < HD_A
    neg_inf = jnp.float32(-jnp.inf)
    half_group = GROUP_A // 2
    kcol = lambda g: slice(g * LANES, (g + 1) * LANES)
    vcol = lambda g: slice((N_KV_A + g) * LANES, (N_KV_A + g + 1) * LANES)
    pair_col = lambda g, pi: slice((g * half_group + pi) * LANES, (g * half_group + pi + 1) * LANES)
    head_rows = lambda hh: slice(hh * C, (hh + 1) * C)

    def out_piece(e):
        cols = slice(e * MXU_WIDTH, (e + 1) * MXU_WIDTH)
        xo_ref[:, cols] = x_ref[:, cols] + jnp.dot(y_ref[prev_slot], wo_ref[:, cols],
                                                   preferred_element_type=F32)

    def finish_out():
        if final:
            acc = xo_ref[...]
            ms = jnp.mean(acc * acc, axis=-1, keepdims=True)
            xo_ref[...] = acc * lax.rsqrt(ms + EPS) * fw_ref[...]

    def chunk(cc):
        rows = slice(cc * C, (cc + 1) * C)
        has_prev = step * MIX_CHUNKS + cc > 0
        for t in range(2):
            w = kv_ref[rows, t * LANES:(t + 1) * LANES].astype(F32)
            w_swapped = pltpu.roll(w, HD_A, 1)
            kvd_ref[C:, (2 * t) * LANES:(2 * t + 1) * LANES] = jnp.where(lo, w, w_swapped).astype(BF16)
            kvd_ref[C:, (2 * t + 1) * LANES:(2 * t + 2) * LANES] = jnp.where(lo, w_swapped, w).astype(BF16)
        kv2 = lambda cols: kvd_ref[:, cols]

        for g in range(N_KV_A):
            for pi in range(half_group):
                q_pair = qa_ref[rows, pair_col(g, pi)].astype(F32)
                qs_ref[g, head_rows(2 * pi), :] = jnp.where(lo, q_pair, 0.0).astype(BF16)
                qs_ref[g, head_rows(2 * pi + 1), :] = jnp.where(lo, 0.0, q_pair).astype(BF16)
            s_ref[g] = _nt_dot(qs_ref[g], kv2(kcol(g)))

        out_piece(cc * PIECES_PER_CHUNK)

        for h in range(N_HEADS_B):
            hc = slice(h * HD_B, (h + 1) * HD_B)
            q, k, v = qb_ref[rows, hc], kb_ref[rows, hc], vb_ref[rows, hc]
            scb_ref[h] = (_nt_dot(q, k) * dmask_ref[h]).astype(BF16)
            r = r_ref[h]
            inter_ref[h] = jnp.dot(q, r.astype(BF16), preferred_element_type=F32) * qdec_ref[h]
            kd = (k.astype(F32) * kdec_ref[h]).astype(BF16)
            r_ref[h] = math.exp(LOG_G[h] * C) * r + _tn_dot(kd, v)

        for g in range(N_KV_A):
            for hh in range(GROUP_A):
                hr = head_rows(hh)
                s_prev = jnp.where(has_prev, s_ref[g, hr, :LANES], neg_inf)
                s = jnp.where(tri, s_ref[g, hr, LANES:], s_prev) * (HD_A ** -0.5)
                sink = sink_ref[g * GROUP_A + hh]
                m = jnp.maximum(jnp.max(s, axis=-1, keepdims=True), sink)
                e = jnp.exp(s - m)
                denom = jnp.sum(e, axis=-1, keepdims=True) + jnp.exp(sink - m)
                p = e / denom
                p_ref[g, hr, :LANES] = jnp.where(tri, 0.0, p).astype(BF16)
                p_ref[g, hr, LANES:] = jnp.where(tri, p, 0.0).astype(BF16)

        for g in range(N_KV_A):
            o_ref[g] = jnp.dot(p_ref[g], kv2(vcol(g)), preferred_element_type=F32)

        for e in range(cc * PIECES_PER_CHUNK + 1, (cc + 1) * PIECES_PER_CHUNK):
            out_piece(e)

        for h in range(N_HEADS_B):
            hc = slice(h * HD_B, (h + 1) * HD_B)
            o = jnp.dot(scb_ref[h], vb_ref[rows, hc], preferred_element_type=F32) + inter_ref[h]
            mu = jnp.mean(o, axis=-1, keepdims=True)
            d = o - mu
            var = jnp.mean(d * d, axis=-1, keepdims=True)
            yh = d * lax.rsqrt(var + EPS) * gnw_ref[:, hc] * _silu(gb_ref[rows, hc].astype(F32))
            y_ref[slot, rows, W_A + h * HD_B: W_A + (h + 1) * HD_B] = yh.astype(y_ref.dtype)

        for g in range(N_KV_A):
            for pi in range(half_group):
                pc = pair_col(g, pi)
                o_pair = jnp.where(lo, o_ref[g, head_rows(2 * pi), :], o_ref[g, head_rows(2 * pi + 1), :])
                y_ref[slot, rows, pc] = (o_pair * _silu(ga_ref[rows, pc].astype(F32))).astype(y_ref.dtype)

        kvd_ref[:C, :] = kvd_ref[C:, :]

    @pl.when(step < last_step)
    def _():
        for cc in range(MIX_CHUNKS):
            chunk(cc)
        finish_out()

    @pl.when(step == last_step)
    def _():
        for e in range(OUT_PIECES):
            out_piece(e)
        finish_out()


def _mix_prompt(p, x, sinks, gn_w, w_out_b, final_w, final):
    t = p.shape[0]
    C = RET_CHUNK
    rows = MIX_CHUNKS * C
    n_blocks = t // rows
    cur = lambda c: jnp.minimum(c, n_blocks - 1)
    prev = lambda c: jnp.maximum(c - 1, 0)
    sec_spec = lambda s: pl.BlockSpec((rows, W_A), lambda c, s=s: (cur(c), s))
    return pl.pallas_call(
        functools.partial(_mix_prompt_kernel, final=final),
        grid=(n_blocks + 1,),
        in_specs=[
            pl.BlockSpec(memory_space=pltpu.SMEM),
            sec_spec(SEC_QA), sec_spec(SEC_GA), sec_spec(SEC_QB),
            sec_spec(SEC_KB), sec_spec(SEC_VB), sec_spec(SEC_GB),
            pl.BlockSpec((rows, COL_TILE), lambda c: (cur(c), KV_BLOCK)),
            pl.BlockSpec((1, W_B), lambda c: (0, 0)),
            pl.BlockSpec((rows, D_MODEL), lambda c: (prev(c), 0)),
            pl.BlockSpec((D_MODEL, D_MODEL), lambda c: (0, 0), pipeline_mode=pl.Buffered(1)),
            pl.BlockSpec((1, D_MODEL), lambda c: (0, 0)),
        ],
        out_specs=[
            pl.BlockSpec((rows, D_MODEL), lambda c: (prev(c), 0)),
            pl.BlockSpec((N_HEADS_B, HD_B, HD_B), lambda c: (0, 0, 0)),
        ],
        out_shape=[
            jax.ShapeDtypeStruct((t, D_MODEL), F32),
            jax.ShapeDtypeStruct((N_HEADS_B, HD_B, HD_B), F32),
        ],
        scratch_shapes=[pltpu.VMEM((N_HEADS_B, C, LANES), F32)] * 3 + [
            pltpu.VMEM((2 * C, 2 * COL_TILE), BF16),
            pltpu.VMEM((N_KV_A, GROUP_A * C, LANES), BF16),
            pltpu.VMEM((N_KV_A, GROUP_A * C, 2 * LANES), F32),
            pltpu.VMEM((N_KV_A, GROUP_A * C, 2 * LANES), BF16),
            pltpu.VMEM((N_KV_A, GROUP_A * C, LANES), F32),
            pltpu.VMEM((N_HEADS_B, C, HD_B), BF16),
            pltpu.VMEM((N_HEADS_B, C, HD_B), F32),
            pltpu.VMEM((2, rows, D_MODEL), BF16),
        ],
        compiler_params=pltpu.CompilerParams(
            dimension_semantics=("arbitrary",),
            vmem_limit_bytes=VMEM_LIMIT),
        name="mix_prompt",
    )(sinks, p, p, p, p, p, p, p, gn_w, x, w_out_b, final_w)


SAMPLE_BB = 16
LANE_BATCH = LANES // 4


def _mix_sample_kernel(sinkrow_ref, q3_ref, ga3_ref, qb_ref, kb_ref, vb_ref, gb_ref, kvn_ref,
                       kbt_ref, vbg_ref, ck_ref, cv_ref, st_ref, gnw_ref, *rest, t_new):
    ya3_ref, yb_ref, nk_ref, nv_ref, nst_ref, s_ref, p_ref, vall_ref, kt_ref, x_ref, sc_ref = rest[-11:]
    i = pl.program_id(0)
    T = t_new
    BB = SAMPLE_BB
    L = ck_ref.shape[1]
    n_keys = L + T
    QR = T * GROUP_A
    R = BB * T
    head_sl = lambda g: slice(g * GROUP_A, (g + 1) * GROUP_A)
    lane_sl = lambda g: slice(g * HD_A, (g + 1) * HD_A)
    tok_rows = lambda bb: slice(bb * T, (bb + 1) * T)


    for bb in range(BB):
        k_new = kvn_ref[tok_rows(bb), :LANES]
        v_new = kvn_ref[tok_rows(bb), LANES:]
        k_all = jnp.concatenate([ck_ref[bb], k_new], axis=0)
        v_all = jnp.concatenate([cv_ref[bb], v_new], axis=0)
        nk_ref[bb] = k_all[T:, :]
        nv_ref[bb] = v_all[T:, :]
        vall_ref[bb] = v_all.astype(BF16)
        k_all_b = k_all.astype(BF16)
        for g in range(N_KV_A):
            q = q3_ref[tok_rows(bb), head_sl(g), :].reshape(T * GROUP_A, HD_A).astype(BF16)
            s_ref[g, bb * QR:(bb + 1) * QR, :] = _nt_dot(q, k_all_b[:, lane_sl(g)])

    lane_t = (lax.broadcasted_iota(jnp.int32, (1, LANES), 1) % T).astype(F32)
    lane_b = lax.broadcasted_iota(jnp.int32, (HD_B, LANES), 1) // T
    slot0 = (i % (LANE_BATCH // BB)) * BB
    for h in range(N_HEADS_B):
        hc = slice(h * HD_B, (h + 1) * HD_B)
        lg = LOG_G[h]
        kdec = jnp.exp(lg * (T - 1.0 - lane_t))
        kt_dec = kbt_ref[hc, :] * kdec
        for bb in range(BB):
            kt_ref[h, bb * HD_B:(bb + 1) * HD_B, :] = jnp.where(lane_b == slot0 + bb, kt_dec, 0.0).astype(BF16)
        u = jnp.dot(kt_ref[h], vbg_ref[:, hc].astype(BF16), preferred_element_type=F32)
        for bb in range(BB):
            r = st_ref[bb, h]
            pr = slice((bb // 2) * 2 * T, (bb // 2 + 1) * 2 * T)
            x_ref[bb * N_HEADS_B + h] = jnp.dot(qb_ref[pr, hc].astype(BF16), r.astype(BF16),
                                                 preferred_element_type=F32)
            nst_ref[bb, h] = math.exp(lg * T) * r + u[bb * HD_B:(bb + 1) * HD_B, :]

    ri = lax.broadcasted_iota(jnp.int32, (R, R), 0)
    ci = lax.broadcasted_iota(jnp.int32, (R, R), 1)
    same_b = (ri // T) == (ci // T)
    dt = (ri % T - ci % T).astype(F32)
    keep = same_b & (dt >= 0)
    for h in range(N_HEADS_B):
        hc = slice(h * HD_B, (h + 1) * HD_B)
        dmask = jnp.where(keep, jnp.exp(LOG_G[h] * jnp.maximum(dt, 0.0)), 0.0)
        sc = _nt_dot(qb_ref[:, hc].astype(BF16), kb_ref[:, hc].astype(BF16)) * dmask
        sc_ref[h] = sc.astype(BF16)

    rows_q = (lax.broadcasted_iota(jnp.int32, (BB * QR, n_keys), 0) % QR) // GROUP_A
    key = lax.broadcasted_iota(jnp.int32, (BB * QR, n_keys), 1)
    delta = jnp.where(key < L, rows_q + L - key, rows_q - (key - L))
    valid = (delta >= 0) & (delta < WINDOW)
    for g in range(N_KV_A):
        s = jnp.where(valid, s_ref[g] * (HD_A ** -0.5), -jnp.inf)
        sink = sinkrow_ref[g]
        m = jnp.maximum(jnp.max(s, axis=-1, keepdims=True), sink)
        e = jnp.exp(s - m)
        denom = jnp.sum(e, axis=-1, keepdims=True) + jnp.exp(sink - m)
        p_ref[g] = (e / denom).astype(BF16)

    tposf = (lax.broadcasted_iota(jnp.int32, (R, 1), 0) % T).astype(F32)
    upper = (lax.broadcasted_iota(jnp.int32, (2 * T, 1), 0) < T)
    for h in range(N_HEADS_B):
        hc = slice(h * HD_B, (h + 1) * HD_B)
        intra = jnp.dot(sc_ref[h], vb_ref[:, hc].astype(BF16), preferred_element_type=F32)
        inter = jnp.concatenate(
            [jnp.where(upper, x_ref[(2 * pair) * N_HEADS_B + h], x_ref[(2 * pair + 1) * N_HEADS_B + h])
             for pair in range(BB // 2)], axis=0)
        o = intra + inter * jnp.exp(LOG_G[h] * (tposf + 1.0))
        mu = jnp.mean(o, axis=-1, keepdims=True)
        d = o - mu
        var = jnp.mean(d * d, axis=-1, keepdims=True)
        yb_ref[:, hc] = d * lax.rsqrt(var + EPS) * gnw_ref[:, hc] * _silu(gb_ref[:, hc])

    for bb in range(BB):
        for g in range(N_KV_A):
            o = jnp.dot(p_ref[g, bb * QR:(bb + 1) * QR, :], vall_ref[bb, :, lane_sl(g)],
                        preferred_element_type=F32)
            gate = _silu(ga3_ref[tok_rows(bb), head_sl(g), :].reshape(QR, HD_A))
            ya3_ref[tok_rows(bb), head_sl(g), :] = (o * gate).reshape(T, GROUP_A, HD_A)


def _mix_sample(ps, sinks, gn_w, cache_k, cache_v, state, d, prev_out, t_new):
    m = ps.shape[0]
    nb = m // t_new
    L = cache_k.shape[2]
    BB = SAMPLE_BB
    n_alias = len(prev_out)
    n_in = 14
    rows = BB * t_new
    qa = ps[:, SEC_QA * W_A:(SEC_QA + 1) * W_A].reshape(m, N_HEADS_A, HD_A)
    ga = ps[:, SEC_GA * W_A:(SEC_GA + 1) * W_A].reshape(m, N_HEADS_A, HD_A)
    kbt = ps[:, SEC_KB * W_A:(SEC_KB + 1) * W_A].T
    sinkrow = jnp.tile(sinks.reshape(N_KV_A, 1, GROUP_A), (1, BB * t_new, 1)).reshape(N_KV_A, -1, 1)
    sec_spec = lambda s: pl.BlockSpec((rows, W_A), lambda i, s=s: (i, s))
    steps_per_lane_group = LANE_BATCH // BB
    return pl.pallas_call(
        functools.partial(_mix_sample_kernel, t_new=t_new),
        grid=(nb // BB,),
        in_specs=[
            pl.BlockSpec((N_KV_A, BB * t_new * GROUP_A, 1), lambda i: (0, 0, 0)),
            pl.BlockSpec((rows, N_HEADS_A, HD_A), lambda i: (i, 0, 0)),
            pl.BlockSpec((rows, N_HEADS_A, HD_A), lambda i: (i, 0, 0)),
            sec_spec(SEC_QB), sec_spec(SEC_KB), sec_spec(SEC_VB), sec_spec(SEC_GB),
            pl.BlockSpec((rows, COL_TILE), lambda i: (i, KV_BLOCK)),
            pl.BlockSpec((W_B, LANES), lambda i: (0, i // steps_per_lane_group)),
            pl.BlockSpec((LANES, W_B), lambda i: (i // steps_per_lane_group, SEC_VB)),
            pl.BlockSpec((None, BB, L, LANES), lambda i: (d, i, 0, 0)),
            pl.BlockSpec((None, BB, L, LANES), lambda i: (d, i, 0, 0)),
            pl.BlockSpec((None, BB, N_HEADS_B, HD_B, HD_B), lambda i: (d, i, 0, 0, 0)),
            pl.BlockSpec((1, W_B), lambda i: (0, 0)),
        ] + [pl.BlockSpec(memory_space=pl.ANY)] * n_alias,
        out_specs=[
            pl.BlockSpec((rows, N_HEADS_A, HD_A), lambda i: (i, 0, 0)),
            pl.BlockSpec((rows, W_B), lambda i: (i, 0)),
            pl.BlockSpec((None, BB, L, LANES), lambda i: (d, i, 0, 0)),
            pl.BlockSpec((None, BB, L, LANES), lambda i: (d, i, 0, 0)),
            pl.BlockSpec((None, BB, N_HEADS_B, HD_B, HD_B), lambda i: (d, i, 0, 0, 0)),
        ],
        out_shape=[
            jax.ShapeDtypeStruct((m, N_HEADS_A, HD_A), F32),
            jax.ShapeDtypeStruct((m, W_B), F32),
            jax.ShapeDtypeStruct(cache_k.shape, F32),
            jax.ShapeDtypeStruct(cache_v.shape, F32),
            jax.ShapeDtypeStruct(state.shape, F32),
        ],
        input_output_aliases={n_in + a: 2 + a for a in range(n_alias)},
        scratch_shapes=[
            pltpu.VMEM((N_KV_A, BB * t_new * GROUP_A, L + t_new), F32),
            pltpu.VMEM((N_KV_A, BB * t_new * GROUP_A, L + t_new), BF16),
            pltpu.VMEM((BB, L + t_new, LANES), BF16),
            pltpu.VMEM((N_HEADS_B, BB * HD_B, LANES), BF16),
            pltpu.VMEM((BB * N_HEADS_B, 2 * t_new, HD_B), F32),
            pltpu.VMEM((N_HEADS_B, rows, rows), BF16),
        ],
        compiler_params=pltpu.CompilerParams(
            dimension_semantics=("arbitrary",),
            vmem_limit_bytes=VMEM_LIMIT),
        name="mix_sample",
    )(sinkrow, qa, ga, ps, ps, ps, ps, ps, kbt, ps, cache_k, cache_v, state, gn_w, *prev_out)


def _rope_tables(pos=None, n=None):
    def tab(hd):
        half = hd // 2
        inv = ROPE_THETA ** (-jnp.arange(half, dtype=F32) / half)
        if pos is not None:
            ang = pos.astype(F32)[:, None] * inv[None, :]
            cos, sin = jnp.cos(ang), jnp.sin(ang)
        else:
            hi = (jnp.arange(n // LANES, dtype=F32) * LANES)[:, None] * inv[None, :]
            lo = jnp.arange(LANES, dtype=F32)[:, None] * inv[None, :]
            ch, sh = jnp.cos(hi)[:, None, :], jnp.sin(hi)[:, None, :]
            cl, sl = jnp.cos(lo)[None], jnp.sin(lo)[None]
            cos = (ch * cl - sh * sl).reshape(n, half)
            sin = (sh * cl + ch * sl).reshape(n, half)
        reps = LANES // hd
        return (jnp.tile(jnp.concatenate([cos, cos], axis=1), (1, reps)),
                jnp.tile(jnp.concatenate([-sin, sin], axis=1), (1, reps)))
    return tab(HD_A) + tab(HD_B)


def kernel(x_prompt, x_sample, cache_k_win, cache_v_win, state_ret, norm_w, w_in, attn_sinks,
           ret_norm_w, w_out, final_norm_w):
    bp, tp, _ = x_prompt.shape
    bs, ts, _ = x_sample.shape
    assert bp == 1 and tp % RET_CHUNK == 0 and min(WINDOW, tp) == WINDOW
    L = cache_k_win.shape[2]
    depth = w_in.shape[0]

    tab_p = _rope_tables(n=tp)
    tab_s = _rope_tables(pos=jnp.tile(PAST_LEN + jnp.arange(ts, dtype=jnp.int32), bs))
    fw = final_norm_w.reshape(1, D_MODEL)

    xp = x_prompt.reshape(tp, D_MODEL)
    xs = x_sample.reshape(bs * ts, D_MODEL)
    ck = cache_k_win.reshape(depth, bs, L, N_KV_A * HD_A)
    cv = cache_v_win.reshape(depth, bs, L, N_KV_A * HD_A)

    kp_l, vp_l, rp_l = [], [], []
    sample_out = ()
    for d in range(depth):
        final = d == depth - 1
        nw = norm_w[d].reshape(1, D_MODEL)
        gnw = ret_norm_w[d].reshape(1, W_B)
        ps, _, w_in_b, w_kv_b = _in_proj(xs, nw, w_in, w_in, d, tab_s, F32, tm=bs * ts, emit_w=True)
        ya3, yb, *sample_out = _mix_sample(ps, attn_sinks[d], gnw, ck, cv, state_ret, d, sample_out, ts)
        ys = jnp.concatenate([ya3.reshape(bs * ts, W_A), yb], axis=1).astype(BF16)
        xs, w_out_b = _out_proj(ys, xs, w_out, d, fw, final)
        p, tail = _in_proj(xp, nw, w_in_b, w_kv_b, d, tab_p, BF16, tm=1024, emit_w=False)
        xp, r_fin = _mix_prompt(p, xp, attn_sinks[d], gnw, w_out_b, fw, final)
        kp_l.append(tail[:, :LANES].reshape(WINDOW, N_KV_A, HD_A))
        vp_l.append(tail[:, LANES:].reshape(WINDOW, N_KV_A, HD_A))
        rp_l.append(r_fin)

    keep = min(WINDOW, tp)
    nk, nv, nst = sample_out
    return (xp.reshape(bp, tp, D_MODEL), xs.reshape(bs, ts, D_MODEL),
            jnp.stack(kp_l).reshape(depth, bp, keep, N_KV_A, HD_A),
            jnp.stack(vp_l).reshape(depth, bp, keep, N_KV_A, HD_A),
            jnp.stack(rp_l).reshape(depth, bp, N_HEADS_B, HD_B, HD_B),
            nk.reshape(depth, bs, L, N_KV_A, HD_A), nv.reshape(depth, bs, L, N_KV_A, HD_A), nst)
```

```python
import functools
import math

import jax
import jax.numpy as jnp
from jax import lax
from jax.experimental import pallas as pl
from jax.experimental.pallas import tpu as pltpu

F32 = jnp.float32
BF16 = jnp.bfloat16

D_MODEL = 2048
DEPTH = 2
PAST_LEN = 8192
WINDOW = 128
HD_A = 64
N_HEADS_A = 16
N_KV_A = 2
GROUP_A = N_HEADS_A // N_KV_A
W_A = N_HEADS_A * HD_A
HD_B = 128
N_HEADS_B = 8
W_B = N_HEADS_B * HD_B
RET_CHUNK = 128
ROPE_THETA = 10000.0
EPS = 1e-6

LANES = 128
MXU_WIDTH = 256
SEC_QA, SEC_GA, SEC_QB, SEC_KB, SEC_VB, SEC_GB = 0, 1, 2, 3, 4, 5
IN_TILE = 1024
TILES_PER_SEC = W_A // IN_TILE
KV_TILE = 6 * TILES_PER_SEC
COL_TILE = 2 * N_KV_A * HD_A
KV_BLOCK = KV_TILE * IN_TILE // COL_TILE
D_PROJ_R = (KV_TILE + 1) * IN_TILE
VMEM_LIMIT = 56 * 1024 * 1024

LOG_G = [math.log1p(-(2.0 ** (-5.0 - h))) for h in range(N_HEADS_B)]


def _silu(g):
    return g * (1.0 / (1.0 + jnp.exp(-g)))


EPI_ROPE_A, EPI_ROPE_B, EPI_ROPE_B_SCALED, EPI_PLAIN = 0, 1, 2, 3
N_EPI = 4
_SEC_EPI = (EPI_ROPE_A, EPI_PLAIN, EPI_ROPE_B, EPI_ROPE_B_SCALED, EPI_PLAIN, EPI_PLAIN)


KV_ROPE_GROUPS = N_KV_A * HD_A // LANES
N_W_BLOCKS = IN_TILE // MXU_WIDTH


def _step_tile(step):
    return jnp.where(step == 0, KV_TILE, step - 1)


def _tile_epi(tile, hi):
    sec = tile // TILES_PER_SEC
    t = jnp.int32(EPI_PLAIN if hi else EPI_ROPE_A)
    for s, e in enumerate(_SEC_EPI):
        t = jnp.where(sec == s, e, t)
    return t


def _in_proj_kernel(x_ref, nw_ref, *refs, row_chunk, emit_w):
    n_w = N_W_BLOCKS if emit_w else 1
    w_refs = refs[:n_w]
    wkv_in_ref, ca_ref, sa_ref, cb_ref, sb_ref, o_ref, tail_ref = refs[n_w:n_w + 7]
    w_tile_ref = refs[n_w + 7] if emit_w else w_refs[0]
    wkv_ref = refs[n_w + 8] if emit_w else wkv_in_ref
    h_ref = refs[-1]
    step = pl.program_id(1)
    j = _step_tile(step)

    def weights(e):
        cols = slice(e * MXU_WIDTH, (e + 1) * MXU_WIDTH)
        if emit_w:
            w_tile_ref[:, cols] = w_refs[e][...].astype(BF16)
        return w_tile_ref[:, cols]

    tm = x_ref.shape[0]
    n_chunks = tm // row_chunk

    @pl.when(step == 0)
    def _():
        def body(r, carry):
            rows = pl.ds(pl.multiple_of(r * row_chunk, row_chunk), row_chunk)
            x = x_ref[rows, :]
            ms = jnp.mean(x * x, axis=-1, keepdims=True)
            h_ref[rows, :] = (x * lax.rsqrt(ms + EPS) * nw_ref[...]).astype(BF16)
            return carry
        lax.fori_loop(0, n_chunks, body, 0)

    lane = lax.broadcasted_iota(jnp.int32, (tm, LANES), 1)
    first_half = (lane % HD_A) < HD_A // 2

    def rotate(x, c, s, shift_up, shift_dn):
        partner = jnp.where(first_half, pltpu.roll(x, shift_up, 1), pltpu.roll(x, shift_dn, 1))
        return x * c + partner * s

    epi = _tile_epi(j, 0)
    is_plain = epi == EPI_PLAIN

    @pl.when((j < KV_TILE) & is_plain)
    def _():
        for e in range(N_W_BLOCKS):
            cols = slice(e * MXU_WIDTH, (e + 1) * MXU_WIDTH)
            o_ref[:, cols] = jnp.dot(h_ref[...], weights(e), preferred_element_type=F32).astype(o_ref.dtype)

    @pl.when((j < KV_TILE) & jnp.logical_not(is_plain))
    def _():
        is_a = epi == EPI_ROPE_A
        scale = jnp.where(epi == EPI_ROPE_B_SCALED, HD_B ** -0.5, 1.0).astype(F32)
        c = jnp.where(is_a, ca_ref[...], cb_ref[...]) * scale
        s = jnp.where(is_a, sa_ref[...], sb_ref[...]) * scale
        shift_up = jnp.where(is_a, LANES - HD_A // 2, HD_B // 2)
        shift_dn = jnp.where(is_a, HD_A // 2, HD_B // 2)
        for e in range(N_W_BLOCKS):
            acc = jnp.dot(h_ref[...], weights(e), preferred_element_type=F32)
            for gg in range(MXU_WIDTH // LANES):
                cols = slice(e * MXU_WIDTH + gg * LANES, e * MXU_WIDTH + (gg + 1) * LANES)
                out = rotate(acc[:, gg * LANES:(gg + 1) * LANES], c, s, shift_up, shift_dn)
                o_ref[:, cols] = out.astype(o_ref.dtype)

    @pl.when(j == KV_TILE)
    def _():
        if emit_w:
            wkv_ref[...] = wkv_in_ref[...].astype(BF16)
        acc = jnp.dot(h_ref[...], wkv_ref[...], preferred_element_type=F32)
        for g in range(COL_TILE // LANES):
            cols = slice(g * LANES, (g + 1) * LANES)
            out = acc[:, cols]
            if g < KV_ROPE_GROUPS:
                out = rotate(out, ca_ref[...], sa_ref[...], LANES - HD_A // 2, HD_A // 2)
            o_ref[:, cols] = out.astype(o_ref.dtype)
            tail_ref[:, cols] = out[tm - WINDOW:, :]
        o_ref[:, COL_TILE:] = jnp.zeros((tm, IN_TILE - COL_TILE), o_ref.dtype)
        if emit_w:
            w_tile_ref[...] = jnp.zeros(w_tile_ref.shape, BF16)


def _in_proj(x, norm_w, w_src, wkv_src, d, tables, out_dtype, tm, emit_w):
    m = x.shape[0]
    n_tiles = D_PROJ_R // IN_TILE
    assert m == tm or not emit_w
    n_src_blocks = (6 * W_A + 2 * N_KV_A * HD_A) // MXU_WIDTH

    def src_block(j, e):
        sec = j // TILES_PER_SEC
        blk = sec * (W_A // MXU_WIDTH) + jnp.minimum(sec, 1) + (j % TILES_PER_SEC) * N_W_BLOCKS + e
        return jnp.minimum(blk, n_src_blocks - 1)

    if emit_w:
        w_specs = [pl.BlockSpec((None, D_MODEL, MXU_WIDTH),
                                lambda i, s, e=e: (d, 0, src_block(_step_tile(s), e)))
                   for e in range(N_W_BLOCKS)]
        wkv_spec = pl.BlockSpec((None, D_MODEL, MXU_WIDTH), lambda i, j: (d, 0, W_A // MXU_WIDTH))
    else:
        w_specs = [pl.BlockSpec((D_MODEL, IN_TILE),
                                lambda i, s: (0, jnp.minimum(_step_tile(s), KV_TILE - 1)))]
        wkv_spec = pl.BlockSpec((D_MODEL, COL_TILE), lambda i, j: (0, 0))
    tab_spec = pl.BlockSpec((tm, LANES), lambda i, j: (i, 0))

    return pl.pallas_call(
        functools.partial(_in_proj_kernel, row_chunk=min(tm, 256), emit_w=emit_w),
        grid=(m // tm, n_tiles),
        in_specs=[
            pl.BlockSpec((tm, D_MODEL), lambda i, j: (i, 0)),
            pl.BlockSpec((1, D_MODEL), lambda i, j: (0, 0)),
        ] + w_specs + [wkv_spec, tab_spec, tab_spec, tab_spec, tab_spec],
        out_specs=[
            pl.BlockSpec((tm, IN_TILE), lambda i, s: (i, _step_tile(s))),
            pl.BlockSpec((WINDOW, COL_TILE), lambda i, j: (0, 0)),
        ] + ([pl.BlockSpec((D_MODEL, IN_TILE), lambda i, s: (0, _step_tile(s))),
              pl.BlockSpec((D_MODEL, COL_TILE), lambda i, j: (0, 0))] if emit_w else []),
        out_shape=[
            jax.ShapeDtypeStruct((m, D_PROJ_R), out_dtype),
            jax.ShapeDtypeStruct((WINDOW, COL_TILE), F32),
        ] + ([jax.ShapeDtypeStruct((D_MODEL, D_PROJ_R), BF16),
              jax.ShapeDtypeStruct((D_MODEL, COL_TILE), BF16)] if emit_w else []),
        scratch_shapes=[pltpu.VMEM((tm, D_MODEL), BF16)],
        compiler_params=pltpu.CompilerParams(
            dimension_semantics=("arbitrary", "arbitrary"),
            vmem_limit_bytes=VMEM_LIMIT),
        name="in_proj",
    )(x, norm_w, *([w_src] * len(w_specs)), wkv_src, *tables)


def _out_proj_kernel(y_ref, x_ref, w_ref, fw_ref, o_ref, wout_ref, *, final):
    wout_ref[...] = w_ref[...].astype(BF16)
    acc = x_ref[...] + jnp.dot(y_ref[...], wout_ref[...], preferred_element_type=F32)
    if final:
        ms = jnp.mean(acc * acc, axis=-1, keepdims=True)
        acc = acc * lax.rsqrt(ms + EPS) * fw_ref[...]
    o_ref[...] = acc


def _out_proj(y, x, w_all, d, final_w, final):
    m = x.shape[0]
    return pl.pallas_call(
        functools.partial(_out_proj_kernel, final=final),
        grid=(1,),
        in_specs=[
            pl.BlockSpec((m, D_MODEL), lambda i: (0, 0)),
            pl.BlockSpec((m, D_MODEL), lambda i: (0, 0)),
            pl.BlockSpec((None, D_MODEL, D_MODEL), lambda i: (d, 0, 0), pipeline_mode=pl.Buffered(1)),
            pl.BlockSpec((1, D_MODEL), lambda i: (0, 0)),
        ],
        out_specs=[
            pl.BlockSpec((m, D_MODEL), lambda i: (0, 0)),
            pl.BlockSpec((D_MODEL, D_MODEL), lambda i: (0, 0)),
        ],
        out_shape=[
            jax.ShapeDtypeStruct((m, D_MODEL), F32),
            jax.ShapeDtypeStruct((D_MODEL, D_MODEL), BF16),
        ],
        compiler_params=pltpu.CompilerParams(
            dimension_semantics=("arbitrary",),
            vmem_limit_bytes=VMEM_LIMIT),
        name="out_proj",
    )(y, x, w_all, final_w)


def _nt_dot(a, b):
    return lax.dot_general(a, b, (((1,), (1,)), ((), ())), preferred_element_type=F32)


def _tn_dot(a, b):
    return lax.dot_general(a, b, (((0,), (0,)), ((), ())), preferred_element_type=F32)


MIX_CHUNKS = 4
OUT_PIECES = D_MODEL // MXU_WIDTH
PIECES_PER_CHUNK = OUT_PIECES // MIX_CHUNKS


def _mix_prompt_kernel(sink_ref, qa_ref, ga_ref, qb_ref, kb_ref, vb_ref, gb_ref,
                       kv_ref, gnw_ref, x_ref, wo_ref, fw_ref, xo_ref, r_ref,
                       dmask_ref, qdec_ref, kdec_ref, kvd_ref,
                       qs_ref, s_ref, p_ref, o_ref, scb_ref, inter_ref, y_ref, *, final):
    step = pl.program_id(0)
    last_step = pl.num_programs(0) - 1
    slot = step % 2
    prev_slot = 1 - slot
    C = RET_CHUNK
    row = lax.broadcasted_iota(jnp.int32, (C, LANES), 0)
    lane = lax.broadcasted_iota(jnp.int32, (C, LANES), 1)

    @pl.when(step == 0)
    def _():
        r_ref[...] = jnp.zeros(r_ref.shape, F32)
        kvd_ref[...] = jnp.zeros(kvd_ref.shape, kvd_ref.dtype)
        y_ref[1] = jnp.zeros(y_ref.shape[1:], y_ref.dtype)
        rowf = row.astype(F32)
        diff = rowf - lane.astype(F32)
        for h in range(N_HEADS_B):
            dmask_ref[h] = jnp.where(diff >= 0, jnp.exp(LOG_G[h] * jnp.maximum(diff, 0.0)), 0.0)
            qdec_ref[h] = jnp.exp(LOG_G[h] * (rowf + 1.0))
            kdec_ref[h] = jnp.exp(LOG_G[h] * (C - 1.0 - rowf))

    tri = lane <= row
    lo = lane < HD_A
    neg_inf = jnp.float32(-jnp.inf)
    half_group = GROUP_A // 2
    kcol = lambda g: slice(g * LANES, (g + 1) * LANES)
    vcol = lambda g: slice((N_KV_A + g) * LANES, (N_KV_A + g + 1) * LANES)
    pair_col = lambda g, pi: slice((g * half_group + pi) * LANES, (g * half_group + pi + 1) * LANES)
    head_rows = lambda hh: slice(hh * C, (hh + 1) * C)

    def out_piece(e):
        cols = slice(e * MXU_WIDTH, (e + 1) * MXU_WIDTH)
        xo_ref[:, cols] = x_ref[:, cols] + jnp.dot(y_ref[prev_slot], wo_ref[:, cols],
                                                   preferred_element_type=F32)

    def finish_out():
        if final:
            acc = xo_ref[...]
            ms = jnp.mean(acc * acc, axis=-1, keepdims=True)
            xo_ref[...] = acc * lax.rsqrt(ms + EPS) * fw_ref[...]

    def chunk(cc):
        rows = slice(cc * C, (cc + 1) * C)
        has_prev = step * MIX_CHUNKS + cc > 0
        for t in range(2):
            w = kv_ref[rows, t * LANES:(t + 1) * LANES].astype(F32)
            w_swapped = pltpu.roll(w, HD_A, 1)
            kvd_ref[C:, (2 * t) * LANES:(2 * t + 1) * LANES] = jnp.where(lo, w, w_swapped).astype(BF16)
            kvd_ref[C:, (2 * t + 1) * LANES:(2 * t + 2) * LANES] = jnp.where(lo, w_swapped, w).astype(BF16)
        kv2 = lambda cols: kvd_ref[:, cols]

        for g in range(N_KV_A):
            for pi in range(half_group):
                q_pair = qa_ref[rows, pair_col(g, pi)].astype(F32)
                qs_ref[g, head_rows(2 * pi), :] = jnp.where(lo, q_pair, 0.0).astype(BF16)
                qs_ref[g, head_rows(2 * pi + 1), :] = jnp.where(lo, 0.0, q_pair).astype(BF16)
            s_ref[g] = _nt_dot(qs_ref[g], kv2(kcol(g)))

        out_piece(cc * PIECES_PER_CHUNK)

        for h in range(N_HEADS_B):
            hc = slice(h * HD_B, (h + 1) * HD_B)
            q, k, v = qb_ref[rows, hc], kb_ref[rows, hc], vb_ref[rows, hc]
            scb_ref[h] = (_nt_dot(q, k) * dmask_ref[h]).astype(BF16)
            r = r_ref[h]
            inter_ref[h] = jnp.dot(q, r.astype(BF16), preferred_element_type=F32) * qdec_ref[h]
            kd = (k.astype(F32) * kdec_ref[h]).astype(BF16)
            r_ref[h] = math.exp(LOG_G[h] * C) * r + _tn_dot(kd, v)

        for g in range(N_KV_A):
            for hh in range(GROUP_A):
                hr = head_rows(hh)
                s_prev = jnp.where(has_prev, s_ref[g, hr, :LANES], neg_inf)
                s = jnp.where(tri, s_ref[g, hr, LANES:], s_prev) * (HD_A ** -0.5)
                sink = sink_ref[g * GROUP_A + hh]
                m = jnp.maximum(jnp.max(s, axis=-1, keepdims=True), sink)
                e = jnp.exp(s - m)
                denom = jnp.sum(e, axis=-1, keepdims=True) + jnp.exp(sink - m)
                p = e / denom
                p_ref[g, hr, :LANES] = jnp.where(tri, 0.0, p).astype(BF16)
                p_ref[g, hr, LANES:] = jnp.where(tri, p, 0.0).astype(BF16)

        for g in range(N_KV_A):
            o_ref[g] = jnp.dot(p_ref[g], kv2(vcol(g)), preferred_element_type=F32)

        for e in range(cc * PIECES_PER_CHUNK + 1, (cc + 1) * PIECES_PER_CHUNK):
            out_piece(e)

        for h in range(N_HEADS_B):
            hc = slice(h * HD_B, (h + 1) * HD_B)
            o = jnp.dot(scb_ref[h], vb_ref[rows, hc], preferred_element_type=F32) + inter_ref[h]
            mu = jnp.mean(o, axis=-1, keepdims=True)
            d = o - mu
            var = jnp.mean(d * d, axis=-1, keepdims=True)
            yh = d * lax.rsqrt(var + EPS) * gnw_ref[:, hc] * _silu(gb_ref[rows, hc].astype(F32))
            y_ref[slot, rows, W_A + h * HD_B: W_A + (h + 1) * HD_B] = yh.astype(y_ref.dtype)

        for g in range(N_KV_A):
            for pi in range(half_group):
                pc = pair_col(g, pi)
                o_pair = jnp.where(lo, o_ref[g, head_rows(2 * pi), :], o_ref[g, head_rows(2 * pi + 1), :])
                y_ref[slot, rows, pc] = (o_pair * _silu(ga_ref[rows, pc].astype(F32))).astype(y_ref.dtype)

        kvd_ref[:C, :] = kvd_ref[C:, :]

    @pl.when(step < last_step)
    def _():
        for cc in range(MIX_CHUNKS):
            chunk(cc)
        finish_out()

    @pl.when(step == last_step)
    def _():
        for e in range(OUT_PIECES):
            out_piece(e)
        finish_out()


def _mix_prompt(p, x, sinks, gn_w, w_out_b, final_w, final):
    t = p.shape[0]
    C = RET_CHUNK
    rows = MIX_CHUNKS * C
    n_blocks = t // rows
    cur = lambda c: jnp.minimum(c, n_blocks - 1)
    prev = lambda c: jnp.maximum(c - 1, 0)
    sec_spec = lambda s: pl.BlockSpec((rows, W_A), lambda c, s=s: (cur(c), s))
    return pl.pallas_call(
        functools.partial(_mix_prompt_kernel, final=final),
        grid=(n_blocks + 1,),
        in_specs=[
            pl.BlockSpec(memory_space=pltpu.SMEM),
            sec_spec(SEC_QA), sec_spec(SEC_GA), sec_spec(SEC_QB),
            sec_spec(SEC_KB), sec_spec(SEC_VB), sec_spec(SEC_GB),
            pl.BlockSpec((rows, COL_TILE), lambda c: (cur(c), KV_BLOCK)),
            pl.BlockSpec((1, W_B), lambda c: (0, 0)),
            pl.BlockSpec((rows, D_MODEL), lambda c: (prev(c), 0)),
            pl.BlockSpec((D_MODEL, D_MODEL), lambda c: (0, 0), pipeline_mode=pl.Buffered(1)),
            pl.BlockSpec((1, D_MODEL), lambda c: (0, 0)),
        ],
        out_specs=[
            pl.BlockSpec((rows, D_MODEL), lambda c: (prev(c), 0)),
            pl.BlockSpec((N_HEADS_B, HD_B, HD_B), lambda c: (0, 0, 0)),
        ],
        out_shape=[
            jax.ShapeDtypeStruct((t, D_MODEL), F32),
            jax.ShapeDtypeStruct((N_HEADS_B, HD_B, HD_B), F32),
        ],
        scratch_shapes=[pltpu.VMEM((N_HEADS_B, C, LANES), F32)] * 3 + [
            pltpu.VMEM((2 * C, 2 * COL_TILE), BF16),
            pltpu.VMEM((N_KV_A, GROUP_A * C, LANES), BF16),
            pltpu.VMEM((N_KV_A, GROUP_A * C, 2 * LANES), F32),
            pltpu.VMEM((N_KV_A, GROUP_A * C, 2 * LANES), BF16),
            pltpu.VMEM((N_KV_A, GROUP_A * C, LANES), F32),
            pltpu.VMEM((N_HEADS_B, C, HD_B), BF16),
            pltpu.VMEM((N_HEADS_B, C, HD_B), F32),
            pltpu.VMEM((2, rows, D_MODEL), BF16),
        ],
        compiler_params=pltpu.CompilerParams(
            dimension_semantics=("arbitrary",),
            vmem_limit_bytes=VMEM_LIMIT),
        name="mix_prompt",
    )(sinks, p, p, p, p, p, p, p, gn_w, x, w_out_b, final_w)


SAMPLE_BB = 16
LANE_BATCH = LANES // 4


def _mix_sample_kernel(sinkrow_ref, q3_ref, ga3_ref, qb_ref, kb_ref, vb_ref, gb_ref, kvn_ref,
                       kbt_ref, vbg_ref, ck_ref, cv_ref, st_ref, gnw_ref, *rest, t_new):
    ya3_ref, yb_ref, nk_ref, nv_ref, nst_ref, s_ref, p_ref, vall_ref, kt_ref, x_ref, sc_ref = rest[-11:]
    i = pl.program_id(0)
    T = t_new
    BB = SAMPLE_BB
    L = ck_ref.shape[1]
    n_keys = L + T
    QR = T * GROUP_A
    R = BB * T
    head_sl = lambda g: slice(g * GROUP_A, (g + 1) * GROUP_A)
    lane_sl = lambda g: slice(g * HD_A, (g + 1) * HD_A)
    tok_rows = lambda bb: slice(bb * T, (bb + 1) * T)


    for bb in range(BB):
        k_new = kvn_ref[tok_rows(bb), :LANES]
        v_new = kvn_ref[tok_rows(bb), LANES:]
        k_all = jnp.concatenate([ck_ref[bb], k_new], axis=0)
        v_all = jnp.concatenate([cv_ref[bb], v_new], axis=0)
        nk_ref[bb] = k_all[T:, :]
        nv_ref[bb] = v_all[T:, :]
        vall_ref[bb] = v_all.astype(BF16)
        k_all_b = k_all.astype(BF16)
        for g in range(N_KV_A):
            q = q3_ref[tok_rows(bb), head_sl(g), :].reshape(T * GROUP_A, HD_A).astype(BF16)
            s_ref[g, bb * QR:(bb + 1) * QR, :] = _nt_dot(q, k_all_b[:, lane_sl(g)])

    lane_t = (lax.broadcasted_iota(jnp.int32, (1, LANES), 1) % T).astype(F32)
    lane_b = lax.broadcasted_iota(jnp.int32, (HD_B, LANES), 1) // T
    slot0 = (i % (LANE_BATCH // BB)) * BB
    for h in range(N_HEADS_B):
        hc = slice(h * HD_B, (h + 1) * HD_B)
        lg = LOG_G[h]
        kdec = jnp.exp(lg * (T - 1.0 - lane_t))
        kt_dec = kbt_ref[hc, :] * kdec
        for bb in range(BB):
            kt_ref[h, bb * HD_B:(bb + 1) * HD_B, :] = jnp.where(lane_b == slot0 + bb, kt_dec, 0.0).astype(BF16)
        u = jnp.dot(kt_ref[h], vbg_ref[:, hc].astype(BF16), preferred_element_type=F32)
        for bb in range(BB):
            r = st_ref[bb, h]
            pr = slice((bb // 2) * 2 * T, (bb // 2 + 1) * 2 * T)
            x_ref[bb * N_HEADS_B + h] = jnp.dot(qb_ref[pr, hc].astype(BF16), r.astype(BF16),
                                                 preferred_element_type=F32)
            nst_ref[bb, h] = math.exp(lg * T) * r + u[bb * HD_B:(bb + 1) * HD_B, :]

    ri = lax.broadcasted_iota(jnp.int32, (R, R), 0)
    ci = lax.broadcasted_iota(jnp.int32, (R, R), 1)
    same_b = (ri // T) == (ci // T)
    dt = (ri % T - ci % T).astype(F32)
    keep = same_b & (dt >= 0)
    for h in range(N_HEADS_B):
        hc = slice(h * HD_B, (h + 1) * HD_B)
        dmask = jnp.where(keep, jnp.exp(LOG_G[h] * jnp.maximum(dt, 0.0)), 0.0)
        sc = _nt_dot(qb_ref[:, hc].astype(BF16), kb_ref[:, hc].astype(BF16)) * dmask
        sc_ref[h] = sc.astype(BF16)

    rows_q = (lax.broadcasted_iota(jnp.int32, (BB * QR, n_keys), 0) % QR) // GROUP_A
    key = lax.broadcasted_iota(jnp.int32, (BB * QR, n_keys), 1)
    delta = jnp.where(key < L, rows_q + L - key, rows_q - (key - L))
    valid = (delta >= 0) & (delta < WINDOW)
    for g in range(N_KV_A):
        s = jnp.where(valid, s_ref[g] * (HD_A ** -0.5), -jnp.inf)
        sink = sinkrow_ref[g]
        m = jnp.maximum(jnp.max(s, axis=-1, keepdims=True), sink)
        e = jnp.exp(s - m)
        denom = jnp.sum(e, axis=-1, keepdims=True) + jnp.exp(sink - m)
        p_ref[g] = (e / denom).astype(BF16)

    tposf = (lax.broadcasted_iota(jnp.int32, (R, 1), 0) % T).astype(F32)
    upper = (lax.broadcasted_iota(jnp.int32, (2 * T, 1), 0) < T)
    for h in range(N_HEADS_B):
        hc = slice(h * HD_B, (h + 1) * HD_B)
        intra = jnp.dot(sc_ref[h], vb_ref[:, hc].astype(BF16), preferred_element_type=F32)
        inter = jnp.concatenate(
            [jnp.where(upper, x_ref[(2 * pair) * N_HEADS_B + h], x_ref[(2 * pair + 1) * N_HEADS_B + h])
             for pair in range(BB // 2)], axis=0)
        o = intra + inter * jnp.exp(LOG_G[h] * (tposf + 1.0))
        mu = jnp.mean(o, axis=-1, keepdims=True)
        d = o - mu
        var = jnp.mean(d * d, axis=-1, keepdims=True)
        yb_ref[:, hc] = d * lax.rsqrt(var + EPS) * gnw_ref[:, hc] * _silu(gb_ref[:, hc])

    for bb in range(BB):
        for g in range(N_KV_A):
            o = jnp.dot(p_ref[g, bb * QR:(bb + 1) * QR, :], vall_ref[bb, :, lane_sl(g)],
                        preferred_element_type=F32)
            gate = _silu(ga3_ref[tok_rows(bb), head_sl(g), :].reshape(QR, HD_A))
            ya3_ref[tok_rows(bb), head_sl(g), :] = (o * gate).reshape(T, GROUP_A, HD_A)


def _mix_sample(ps, sinks, gn_w, cache_k, cache_v, state, d, prev_out, t_new):
    m = ps.shape[0]
    nb = m // t_new
    L = cache_k.shape[2]
    BB = SAMPLE_BB
    n_alias = len(prev_out)
    n_in = 14
    rows = BB * t_new
    qa = ps[:, SEC_QA * W_A:(SEC_QA + 1) * W_A].reshape(m, N_HEADS_A, HD_A)
    ga = ps[:, SEC_GA * W_A:(SEC_GA + 1) * W_A].reshape(m, N_HEADS_A, HD_A)
    kbt = ps[:, SEC_KB * W_A:(SEC_KB + 1) * W_A].T
    sinkrow = jnp.tile(sinks.reshape(N_KV_A, 1, GROUP_A), (1, BB * t_new, 1)).reshape(N_KV_A, -1, 1)
    sec_spec = lambda s: pl.BlockSpec((rows, W_A), lambda i, s=s: (i, s))
    steps_per_lane_group = LANE_BATCH // BB
    return pl.pallas_call(
        functools.partial(_mix_sample_kernel, t_new=t_new),
        grid=(nb // BB,),
        in_specs=[
            pl.BlockSpec((N_KV_A, BB * t_new * GROUP_A, 1), lambda i: (0, 0, 0)),
            pl.BlockSpec((rows, N_HEADS_A, HD_A), lambda i: (i, 0, 0)),
            pl.BlockSpec((rows, N_HEADS_A, HD_A), lambda i: (i, 0, 0)),
            sec_spec(SEC_QB), sec_spec(SEC_KB), sec_spec(SEC_VB), sec_spec(SEC_GB),
            pl.BlockSpec((rows, COL_TILE), lambda i: (i, KV_BLOCK)),
            pl.BlockSpec((W_B, LANES), lambda i: (0, i // steps_per_lane_group)),
            pl.BlockSpec((LANES, W_B), lambda i: (i // steps_per_lane_group, SEC_VB)),
            pl.BlockSpec((None, BB, L, LANES), lambda i: (d, i, 0, 0)),
            pl.BlockSpec((None, BB, L, LANES), lambda i: (d, i, 0, 0)),
            pl.BlockSpec((None, BB, N_HEADS_B, HD_B, HD_B), lambda i: (d, i, 0, 0, 0)),
            pl.BlockSpec((1, W_B), lambda i: (0, 0)),
        ] + [pl.BlockSpec(memory_space=pl.ANY)] * n_alias,
        out_specs=[
            pl.BlockSpec((rows, N_HEADS_A, HD_A), lambda i: (i, 0, 0)),
            pl.BlockSpec((rows, W_B), lambda i: (i, 0)),
            pl.BlockSpec((None, BB, L, LANES), lambda i: (d, i, 0, 0)),
            pl.BlockSpec((None, BB, L, LANES), lambda i: (d, i, 0, 0)),
            pl.BlockSpec((None, BB, N_HEADS_B, HD_B, HD_B), lambda i: (d, i, 0, 0, 0)),
        ],
        out_shape=[
            jax.ShapeDtypeStruct((m, N_HEADS_A, HD_A), F32),
            jax.ShapeDtypeStruct((m, W_B), F32),
            jax.ShapeDtypeStruct(cache_k.shape, F32),
            jax.ShapeDtypeStruct(cache_v.shape, F32),
            jax.ShapeDtypeStruct(state.shape, F32),
        ],
        input_output_aliases={n_in + a: 2 + a for a in range(n_alias)},
        scratch_shapes=[
            pltpu.VMEM((N_KV_A, BB * t_new * GROUP_A, L + t_new), F32),
            pltpu.VMEM((N_KV_A, BB * t_new * GROUP_A, L + t_new), BF16),
            pltpu.VMEM((BB, L + t_new, LANES), BF16),
            pltpu.VMEM((N_HEADS_B, BB * HD_B, LANES), BF16),
            pltpu.VMEM((BB * N_HEADS_B, 2 * t_new, HD_B), F32),
            pltpu.VMEM((N_HEADS_B, rows, rows), BF16),
        ],
        compiler_params=pltpu.CompilerParams(
            dimension_semantics=("arbitrary",),
            vmem_limit_bytes=VMEM_LIMIT),
        name="mix_sample",
    )(sinkrow, qa, ga, ps, ps, ps, ps, ps, kbt, ps, cache_k, cache_v, state, gn_w, *prev_out)


def _rope_tables(pos=None, n=None):
    def tab(hd):
        half = hd // 2
        lane = jnp.arange(LANES)
        inv = ROPE_THETA ** (-(lane % half).astype(F32) / half)
        sign = jnp.where((lane % hd) < half, -1.0, 1.0).astype(F32)
        if pos is not None:
            ang = pos.astype(F32)[:, None] * inv[None, :]
            return jnp.cos(ang), jnp.sin(ang) * sign
        hi = (jnp.arange(n // LANES, dtype=F32) * LANES)[:, None, None] * inv
        lo = jnp.arange(LANES, dtype=F32)[None, :, None] * inv
        ch, sh, cl, sl = jnp.cos(hi), jnp.sin(hi), jnp.cos(lo), jnp.sin(lo)
        return ((ch * cl - sh * sl).reshape(n, LANES), ((sh * cl + ch * sl) * sign).reshape(n, LANES))
    return tab(HD_A) + tab(HD_B)


def kernel(x_prompt, x_sample, cache_k_win, cache_v_win, state_ret, norm_w, w_in, attn_sinks,
           ret_norm_w, w_out, final_norm_w):
    bp, tp, _ = x_prompt.shape
    bs, ts, _ = x_sample.shape
    assert bp == 1 and tp % RET_CHUNK == 0 and min(WINDOW, tp) == WINDOW
    L = cache_k_win.shape[2]
    depth = w_in.shape[0]

    tab_p = _rope_tables(n=tp)
    tab_s = _rope_tables(pos=jnp.tile(PAST_LEN + jnp.arange(ts, dtype=jnp.int32), bs))
    fw = final_norm_w.reshape(1, D_MODEL)

    xp = x_prompt.reshape(tp, D_MODEL)
    xs = x_sample.reshape(bs * ts, D_MODEL)
    ck = cache_k_win.reshape(depth, bs, L, N_KV_A * HD_A)
    cv = cache_v_win.reshape(depth, bs, L, N_KV_A * HD_A)

    kp_l, vp_l, rp_l = [], [], []
    sample_out = ()
    for d in range(depth):
        final = d == depth - 1
        nw = norm_w[d].reshape(1, D_MODEL)
        gnw = ret_norm_w[d].reshape(1, W_B)
        ps, _, w_in_b, w_kv_b = _in_proj(xs, nw, w_in, w_in, d, tab_s, F32, tm=bs * ts, emit_w=True)
        ya3, yb, *sample_out = _mix_sample(ps, attn_sinks[d], gnw, ck, cv, state_ret, d, sample_out, ts)
        ys = jnp.concatenate([ya3.reshape(bs * ts, W_A), yb], axis=1).astype(BF16)
        xs, w_out_b = _out_proj(ys, xs, w_out, d, fw, final)
        p, tail = _in_proj(xp, nw, w_in_b, w_kv_b, d, tab_p, BF16, tm=1024, emit_w=False)
        xp, r_fin = _mix_prompt(p, xp, attn_sinks[d], gnw, w_out_b, fw, final)
        kp_l.append(tail[:, :LANES].reshape(WINDOW, N_KV_A, HD_A))
        vp_l.append(tail[:, LANES:].reshape(WINDOW, N_KV_A, HD_A))
        rp_l.append(r_fin)

    keep = min(WINDOW, tp)
    nk, nv, nst = sample_out
    return (xp.reshape(bp, tp, D_MODEL), xs.reshape(bs, ts, D_MODEL),
            jnp.stack(kp_l).reshape(depth, bp, keep, N_KV_A, HD_A),
            jnp.stack(vp_l).reshape(depth, bp, keep, N_KV_A, HD_A),
            jnp.stack(rp_l).reshape(depth, bp, N_HEADS_B, HD_B, HD_B),
            nk.reshape(depth, bs, L, N_KV_A, HD_A), nv.reshape(depth, bs, L, N_KV_A, HD_A), nst)
```

```python
import functools
import math

import jax
import jax.numpy as jnp
from jax import lax
from jax.experimental import pallas as pl
from jax.experimental.pallas import tpu as pltpu

F32 = jnp.float32
BF16 = jnp.bfloat16

D_MODEL = 2048
DEPTH = 2
PAST_LEN = 8192
WINDOW = 128
HD_A = 64
N_HEADS_A = 16
N_KV_A = 2
GROUP_A = N_HEADS_A // N_KV_A
W_A = N_HEADS_A * HD_A
HD_B = 128
N_HEADS_B = 8
W_B = N_HEADS_B * HD_B
RET_CHUNK = 128
ROPE_THETA = 10000.0
EPS = 1e-6

LANES = 128
MXU_WIDTH = 256
SEC_QA, SEC_GA, SEC_QB, SEC_KB, SEC_VB, SEC_GB = 0, 1, 2, 3, 4, 5
IN_TILE = 1024
TILES_PER_SEC = W_A // IN_TILE
KV_TILE = 6 * TILES_PER_SEC
COL_TILE = 2 * N_KV_A * HD_A
KV_BLOCK = KV_TILE * IN_TILE // COL_TILE
D_PROJ_R = (KV_TILE + 1) * IN_TILE
VMEM_LIMIT = 56 * 1024 * 1024

LOG_G = [math.log1p(-(2.0 ** (-5.0 - h))) for h in range(N_HEADS_B)]


def _silu(g):
    return g * (1.0 / (1.0 + jnp.exp(-g)))


EPI_ROPE_A, EPI_ROPE_B, EPI_ROPE_B_SCALED, EPI_PLAIN = 0, 1, 2, 3
N_EPI = 4
_SEC_EPI = (EPI_ROPE_A, EPI_PLAIN, EPI_ROPE_B, EPI_ROPE_B_SCALED, EPI_PLAIN, EPI_PLAIN)


KV_ROPE_GROUPS = N_KV_A * HD_A // LANES
N_W_BLOCKS = IN_TILE // MXU_WIDTH


def _step_tile(step):
    return jnp.where(step == 0, KV_TILE, step - 1)


def _tile_epi(tile, hi):
    sec = tile // TILES_PER_SEC
    t = jnp.int32(EPI_PLAIN if hi else EPI_ROPE_A)
    for s, e in enumerate(_SEC_EPI):
        t = jnp.where(sec == s, e, t)
    return t


def _in_proj_kernel(x_ref, nw_ref, *refs, row_chunk, emit_w):
    n_w = N_W_BLOCKS if emit_w else 1
    w_refs = refs[:n_w]
    wkv_in_ref, ca_ref, sa_ref, cb_ref, sb_ref, o_ref, tail_ref = refs[n_w:n_w + 7]
    w_tile_ref = refs[n_w + 7] if emit_w else w_refs[0]
    wkv_ref = refs[n_w + 8] if emit_w else wkv_in_ref
    h_ref = refs[-1]
    step = pl.program_id(1)
    j = _step_tile(step)

    def weights(e):
        cols = slice(e * MXU_WIDTH, (e + 1) * MXU_WIDTH)
        if emit_w:
            w_tile_ref[:, cols] = w_refs[e][...].astype(BF16)
        return w_tile_ref[:, cols]

    tm = x_ref.shape[0]
    n_chunks = tm // row_chunk

    @pl.when(step == 0)
    def _():
        def body(r, carry):
            rows = pl.ds(pl.multiple_of(r * row_chunk, row_chunk), row_chunk)
            x = x_ref[rows, :]
            ms = jnp.mean(x * x, axis=-1, keepdims=True)
            h_ref[rows, :] = (x * lax.rsqrt(ms + EPS) * nw_ref[...]).astype(BF16)
            return carry
        lax.fori_loop(0, n_chunks, body, 0)

    lane = lax.broadcasted_iota(jnp.int32, (tm, LANES), 1)
    first_half = (lane % HD_A) < HD_A // 2

    def rotate(x, c, s, shift_up, shift_dn):
        partner = jnp.where(first_half, pltpu.roll(x, shift_up, 1), pltpu.roll(x, shift_dn, 1))
        return x * c + partner * s

    epi = _tile_epi(j, 0)
    is_plain = epi == EPI_PLAIN

    @pl.when((j < KV_TILE) & is_plain)
    def _():
        for e in range(N_W_BLOCKS):
            cols = slice(e * MXU_WIDTH, (e + 1) * MXU_WIDTH)
            o_ref[:, cols] = jnp.dot(h_ref[...], weights(e), preferred_element_type=F32).astype(o_ref.dtype)

    @pl.when((j < KV_TILE) & jnp.logical_not(is_plain))
    def _():
        is_a = epi == EPI_ROPE_A
        scale = jnp.where(epi == EPI_ROPE_B_SCALED, HD_B ** -0.5, 1.0).astype(F32)
        c = jnp.where(is_a, ca_ref[...], cb_ref[...]) * scale
        s = jnp.where(is_a, sa_ref[...], sb_ref[...]) * scale
        shift_up = jnp.where(is_a, LANES - HD_A // 2, HD_B // 2)
        shift_dn = jnp.where(is_a, HD_A // 2, HD_B // 2)
        for e in range(N_W_BLOCKS):
            acc = jnp.dot(h_ref[...], weights(e), preferred_element_type=F32)
            for gg in range(MXU_WIDTH // LANES):
                cols = slice(e * MXU_WIDTH + gg * LANES, e * MXU_WIDTH + (gg + 1) * LANES)
                out = rotate(acc[:, gg * LANES:(gg + 1) * LANES], c, s, shift_up, shift_dn)
                o_ref[:, cols] = out.astype(o_ref.dtype)

    @pl.when(j == KV_TILE)
    def _():
        if emit_w:
            wkv_ref[...] = wkv_in_ref[...].astype(BF16)
        acc = jnp.dot(h_ref[...], wkv_ref[...], preferred_element_type=F32)
        for g in range(COL_TILE // LANES):
            cols = slice(g * LANES, (g + 1) * LANES)
            out = acc[:, cols]
            if g < KV_ROPE_GROUPS:
                out = rotate(out, ca_ref[...], sa_ref[...], LANES - HD_A // 2, HD_A // 2)
            o_ref[:, cols] = out.astype(o_ref.dtype)
            tail_ref[:, cols] = out[tm - WINDOW:, :]
        o_ref[:, COL_TILE:] = jnp.zeros((tm, IN_TILE - COL_TILE), o_ref.dtype)
        if emit_w:
            w_tile_ref[...] = jnp.zeros(w_tile_ref.shape, BF16)


def _in_proj(x, norm_w, w_src, wkv_src, d, tables, out_dtype, tm, emit_w):
    m = x.shape[0]
    n_tiles = D_PROJ_R // IN_TILE
    assert m == tm or not emit_w
    n_src_blocks = (6 * W_A + 2 * N_KV_A * HD_A) // MXU_WIDTH

    def src_block(j, e):
        sec = j // TILES_PER_SEC
        blk = sec * (W_A // MXU_WIDTH) + jnp.minimum(sec, 1) + (j % TILES_PER_SEC) * N_W_BLOCKS + e
        return jnp.minimum(blk, n_src_blocks - 1)

    if emit_w:
        w_specs = [pl.BlockSpec((None, D_MODEL, MXU_WIDTH),
                                lambda i, s, e=e: (d, 0, src_block(jnp.maximum(s - 1, 0), e)))
                   for e in range(N_W_BLOCKS)]
        wkv_spec = pl.BlockSpec((None, D_MODEL, MXU_WIDTH), lambda i, j: (d, 0, W_A // MXU_WIDTH))
    else:
        w_specs = [pl.BlockSpec((D_MODEL, IN_TILE),
                                lambda i, s: (0, jnp.minimum(_step_tile(s), KV_TILE - 1)))]
        wkv_spec = pl.BlockSpec((D_MODEL, COL_TILE), lambda i, j: (0, 0))
    tab_spec = pl.BlockSpec((tm, LANES), lambda i, j: (i, 0))

    return pl.pallas_call(
        functools.partial(_in_proj_kernel, row_chunk=min(tm, 256), emit_w=emit_w),
        grid=(m // tm, n_tiles),
        in_specs=[
            pl.BlockSpec((tm, D_MODEL), lambda i, j: (i, 0)),
            pl.BlockSpec((1, D_MODEL), lambda i, j: (0, 0)),
        ] + w_specs + [wkv_spec, tab_spec, tab_spec, tab_spec, tab_spec],
        out_specs=[
            pl.BlockSpec((tm, IN_TILE), lambda i, s: (i, _step_tile(s))),
            pl.BlockSpec((WINDOW, COL_TILE), lambda i, j: (0, 0)),
        ] + ([pl.BlockSpec((D_MODEL, IN_TILE), lambda i, s: (0, _step_tile(s))),
              pl.BlockSpec((D_MODEL, COL_TILE), lambda i, j: (0, 0))] if emit_w else []),
        out_shape=[
            jax.ShapeDtypeStruct((m, D_PROJ_R), out_dtype),
            jax.ShapeDtypeStruct((WINDOW, COL_TILE), F32),
        ] + ([jax.ShapeDtypeStruct((D_MODEL, D_PROJ_R), BF16),
              jax.ShapeDtypeStruct((D_MODEL, COL_TILE), BF16)] if emit_w else []),
        scratch_shapes=[pltpu.VMEM((tm, D_MODEL), BF16)],
        compiler_params=pltpu.CompilerParams(
            dimension_semantics=("arbitrary", "arbitrary"),
            vmem_limit_bytes=VMEM_LIMIT),
        name="in_proj",
    )(x, norm_w, *([w_src] * len(w_specs)), wkv_src, *tables)


OUT_COL_STEPS = 4


def _out_proj_kernel(y_ref, x_ref, w_ref, fw_ref, o_ref, wout_ref, *, final):
    c = pl.program_id(0)
    tn = w_ref.shape[1]
    for cc in range(OUT_COL_STEPS):
        @pl.when(c == cc)
        def _(cc=cc):
            cols = slice(cc * tn, (cc + 1) * tn)
            wout_ref[...] = w_ref[...].astype(BF16)
            o_ref[:, cols] = x_ref[:, cols] + jnp.dot(y_ref[...], wout_ref[...], preferred_element_type=F32)
    if final:
        @pl.when(c == OUT_COL_STEPS - 1)
        def _():
            acc = o_ref[...]
            ms = jnp.mean(acc * acc, axis=-1, keepdims=True)
            o_ref[...] = acc * lax.rsqrt(ms + EPS) * fw_ref[...]


def _out_proj(y, x, w_all, d, final_w, final):
    m = x.shape[0]
    tn = D_MODEL // OUT_COL_STEPS
    return pl.pallas_call(
        functools.partial(_out_proj_kernel, final=final),
        grid=(OUT_COL_STEPS,),
        in_specs=[
            pl.BlockSpec((m, D_MODEL), lambda c: (0, 0)),
            pl.BlockSpec((m, D_MODEL), lambda c: (0, 0)),
            pl.BlockSpec((None, D_MODEL, tn), lambda c: (d, 0, c)),
            pl.BlockSpec((1, D_MODEL), lambda c: (0, 0)),
        ],
        out_specs=[
            pl.BlockSpec((m, D_MODEL), lambda c: (0, 0)),
            pl.BlockSpec((D_MODEL, tn), lambda c: (0, c)),
        ],
        out_shape=[
            jax.ShapeDtypeStruct((m, D_MODEL), F32),
            jax.ShapeDtypeStruct((D_MODEL, D_MODEL), BF16),
        ],
        compiler_params=pltpu.CompilerParams(
            dimension_semantics=("arbitrary",),
            vmem_limit_bytes=VMEM_LIMIT),
        name="out_proj",
    )(y, x, w_all, final_w)


def _nt_dot(a, b):
    return lax.dot_general(a, b, (((1,), (1,)), ((), ())), preferred_element_type=F32)


def _tn_dot(a, b):
    return lax.dot_general(a, b, (((0,), (0,)), ((), ())), preferred_element_type=F32)


MIX_CHUNKS = 4
OUT_PIECES = D_MODEL // MXU_WIDTH
PIECES_PER_CHUNK = OUT_PIECES // MIX_CHUNKS


def _mix_prompt_kernel(sink_ref, qa_ref, ga_ref, qb_ref, kb_ref, vb_ref, gb_ref,
                       kv_ref, gnw_ref, x_ref, wo_ref, fw_ref, xo_ref, r_ref,
                       dmask_ref, qdec_ref, kdec_ref, kvd_ref,
                       qs_ref, s_ref, p_ref, o_ref, scb_ref, inter_ref, y_ref, *, final):
    step = pl.program_id(0)
    last_step = pl.num_programs(0) - 1
    slot = step % 2
    prev_slot = 1 - slot
    C = RET_CHUNK
    row = lax.broadcasted_iota(jnp.int32, (C, LANES), 0)
    lane = lax.broadcasted_iota(jnp.int32, (C, LANES), 1)

    @pl.when(step == 0)
    def _():
        r_ref[...] = jnp.zeros(r_ref.shape, F32)
        kvd_ref[...] = jnp.zeros(kvd_ref.shape, kvd_ref.dtype)
        y_ref[1] = jnp.zeros(y_ref.shape[1:], y_ref.dtype)
        rowf = row.astype(F32)
        diff = rowf - lane.astype(F32)
        for h in range(N_HEADS_B):
            dmask_ref[h] = jnp.where(diff >= 0, jnp.exp(LOG_G[h] * jnp.maximum(diff, 0.0)), 0.0)
            qdec_ref[h] = jnp.exp(LOG_G[h] * (rowf + 1.0))
            kdec_ref[h] = jnp.exp(LOG_G[h] * (C - 1.0 - rowf))

    tri = lane <= row
    lo = lane < HD_A
    neg_inf = jnp.float32(-jnp.inf)
    half_group = GROUP_A // 2
    kcol = lambda g: slice(g * LANES, (g + 1) * LANES)
    vcol = lambda g: slice((N_KV_A + g) * LANES, (N_KV_A + g + 1) * LANES)
    pair_col = lambda g, pi: slice((g * half_group + pi) * LANES, (g * half_group + pi + 1) * LANES)
    head_rows = lambda hh: slice(hh * C, (hh + 1) * C)

    def out_piece(e):
        cols = slice(e * MXU_WIDTH, (e + 1) * MXU_WIDTH)
        xo_ref[:, cols] = x_ref[:, cols] + jnp.dot(y_ref[prev_slot], wo_ref[:, cols],
                                                   preferred_element_type=F32)

    def finish_out():
        if final:
            acc = xo_ref[...]
            ms = jnp.mean(acc * acc, axis=-1, keepdims=True)
            xo_ref[...] = acc * lax.rsqrt(ms + EPS) * fw_ref[...]

    def chunk(cc):
        rows = slice(cc * C, (cc + 1) * C)
        has_prev = step * MIX_CHUNKS + cc > 0
        for t in range(2):
            w = kv_ref[rows, t * LANES:(t + 1) * LANES].astype(F32)
            w_swapped = pltpu.roll(w, HD_A, 1)
            kvd_ref[C:, (2 * t) * LANES:(2 * t + 1) * LANES] = jnp.where(lo, w, w_swapped).astype(BF16)
            kvd_ref[C:, (2 * t + 1) * LANES:(2 * t + 2) * LANES] = jnp.where(lo, w_swapped, w).astype(BF16)
        kv2 = lambda cols: kvd_ref[:, cols]

        for g in range(N_KV_A):
            for pi in range(half_group):
                q_pair = qa_ref[rows, pair_col(g, pi)].astype(F32)
                qs_ref[g, head_rows(2 * pi), :] = jnp.where(lo, q_pair, 0.0).astype(BF16)
                qs_ref[g, head_rows(2 * pi + 1), :] = jnp.where(lo, 0.0, q_pair).astype(BF16)
            s_ref[g] = _nt_dot(qs_ref[g], kv2(kcol(g)))

        out_piece(cc * PIECES_PER_CHUNK)

        for h in range(N_HEADS_B):
            hc = slice(h * HD_B, (h + 1) * HD_B)
            q, k, v = qb_ref[rows, hc], kb_ref[rows, hc], vb_ref[rows, hc]
            scb_ref[h] = (_nt_dot(q, k) * dmask_ref[h]).astype(BF16)
            r = r_ref[h]
            inter_ref[h] = jnp.dot(q, r.astype(BF16), preferred_element_type=F32) * qdec_ref[h]
            kd = (k.astype(F32) * kdec_ref[h]).astype(BF16)
            r_ref[h] = math.exp(LOG_G[h] * C) * r + _tn_dot(kd, v)

        for g in range(N_KV_A):
            for hh in range(GROUP_A):
                hr = head_rows(hh)
                s_prev = jnp.where(has_prev, s_ref[g, hr, :LANES], neg_inf)
                s = jnp.where(tri, s_ref[g, hr, LANES:], s_prev) * (HD_A ** -0.5)
                sink = sink_ref[g * GROUP_A + hh]
                m = jnp.maximum(jnp.max(s, axis=-1, keepdims=True), sink)
                e = jnp.exp(s - m)
                denom = jnp.sum(e, axis=-1, keepdims=True) + jnp.exp(sink - m)
                p = e / denom
                p_ref[g, hr, :LANES] = jnp.where(tri, 0.0, p).astype(BF16)
                p_ref[g, hr, LANES:] = jnp.where(tri, p, 0.0).astype(BF16)

        for g in range(N_KV_A):
            o_ref[g] = jnp.dot(p_ref[g], kv2(vcol(g)), preferred_element_type=F32)

        for e in range(cc * PIECES_PER_CHUNK + 1, (cc + 1) * PIECES_PER_CHUNK):
            out_piece(e)

        for h in range(N_HEADS_B):
            hc = slice(h * HD_B, (h + 1) * HD_B)
            o = jnp.dot(scb_ref[h], vb_ref[rows, hc], preferred_element_type=F32) + inter_ref[h]
            mu = jnp.mean(o, axis=-1, keepdims=True)
            d = o - mu
            var = jnp.mean(d * d, axis=-1, keepdims=True)
            yh = d * lax.rsqrt(var + EPS) * gnw_ref[:, hc] * _silu(gb_ref[rows, hc].astype(F32))
            y_ref[slot, rows, W_A + h * HD_B: W_A + (h + 1) * HD_B] = yh.astype(y_ref.dtype)

        for g in range(N_KV_A):
            for pi in range(half_group):
                pc = pair_col(g, pi)
                o_pair = jnp.where(lo, o_ref[g, head_rows(2 * pi), :], o_ref[g, head_rows(2 * pi + 1), :])
                y_ref[slot, rows, pc] = (o_pair * _silu(ga_ref[rows, pc].astype(F32))).astype(y_ref.dtype)

        kvd_ref[:C, :] = kvd_ref[C:, :]

    @pl.when(step < last_step)
    def _():
        for cc in range(MIX_CHUNKS):
            chunk(cc)
        finish_out()

    @pl.when(step == last_step)
    def _():
        for e in range(OUT_PIECES):
            out_piece(e)
        finish_out()


def _mix_prompt(p, x, sinks, gn_w, w_out_b, final_w, final):
    t = p.shape[0]
    C = RET_CHUNK
    rows = MIX_CHUNKS * C
    n_blocks = t // rows
    cur = lambda c: jnp.minimum(c, n_blocks - 1)
    prev = lambda c: jnp.maximum(c - 1, 0)
    sec_spec = lambda s: pl.BlockSpec((rows, W_A), lambda c, s=s: (cur(c), s))
    return pl.pallas_call(
        functools.partial(_mix_prompt_kernel, final=final),
        grid=(n_blocks + 1,),
        in_specs=[
            pl.BlockSpec(memory_space=pltpu.SMEM),
            sec_spec(SEC_QA), sec_spec(SEC_GA), sec_spec(SEC_QB),
            sec_spec(SEC_KB), sec_spec(SEC_VB), sec_spec(SEC_GB),
            pl.BlockSpec((rows, COL_TILE), lambda c: (cur(c), KV_BLOCK)),
            pl.BlockSpec((1, W_B), lambda c: (0, 0)),
            pl.BlockSpec((rows, D_MODEL), lambda c: (prev(c), 0)),
            pl.BlockSpec((D_MODEL, D_MODEL), lambda c: (0, 0), pipeline_mode=pl.Buffered(1)),
            pl.BlockSpec((1, D_MODEL), lambda c: (0, 0)),
        ],
        out_specs=[
            pl.BlockSpec((rows, D_MODEL), lambda c: (prev(c), 0)),
            pl.BlockSpec((N_HEADS_B, HD_B, HD_B), lambda c: (0, 0, 0)),
        ],
        out_shape=[
            jax.ShapeDtypeStruct((t, D_MODEL), F32),
            jax.ShapeDtypeStruct((N_HEADS_B, HD_B, HD_B), F32),
        ],
        scratch_shapes=[pltpu.VMEM((N_HEADS_B, C, LANES), F32)] * 3 + [
            pltpu.VMEM((2 * C, 2 * COL_TILE), BF16),
            pltpu.VMEM((N_KV_A, GROUP_A * C, LANES), BF16),
            pltpu.VMEM((N_KV_A, GROUP_A * C, 2 * LANES), F32),
            pltpu.VMEM((N_KV_A, GROUP_A * C, 2 * LANES), BF16),
            pltpu.VMEM((N_KV_A, GROUP_A * C, LANES), F32),
            pltpu.VMEM((N_HEADS_B, C, HD_B), BF16),
            pltpu.VMEM((N_HEADS_B, C, HD_B), F32),
            pltpu.VMEM((2, rows, D_MODEL), BF16),
        ],
        compiler_params=pltpu.CompilerParams(
            dimension_semantics=("arbitrary",),
            vmem_limit_bytes=VMEM_LIMIT),
        name="mix_prompt",
    )(sinks, p, p, p, p, p, p, p, gn_w, x, w_out_b, final_w)


SAMPLE_BB = 16
LANE_BATCH = LANES // 4


def _mix_sample_kernel(sinkrow_ref, q3_ref, ga3_ref, qb_ref, kb_ref, vb_ref, gb_ref, kvn_ref,
                       kbt_ref, vbg_ref, ck_ref, cv_ref, st_ref, gnw_ref, *rest, t_new):
    ya3_ref, yb_ref, nk_ref, nv_ref, nst_ref, s_ref, p_ref, vall_ref, kt_ref, x_ref, sc_ref = rest[-11:]
    i = pl.program_id(0)
    T = t_new
    BB = SAMPLE_BB
    L = ck_ref.shape[1]
    n_keys = L + T
    QR = T * GROUP_A
    R = BB * T
    head_sl = lambda g: slice(g * GROUP_A, (g + 1) * GROUP_A)
    lane_sl = lambda g: slice(g * HD_A, (g + 1) * HD_A)
    tok_rows = lambda bb: slice(bb * T, (bb + 1) * T)


    for bb in range(BB):
        k_new = kvn_ref[tok_rows(bb), :LANES]
        v_new = kvn_ref[tok_rows(bb), LANES:]
        k_all = jnp.concatenate([ck_ref[bb], k_new], axis=0)
        v_all = jnp.concatenate([cv_ref[bb], v_new], axis=0)
        nk_ref[bb] = k_all[T:, :]
        nv_ref[bb] = v_all[T:, :]
        vall_ref[bb] = v_all.astype(BF16)
        k_all_b = k_all.astype(BF16)
        for g in range(N_KV_A):
            q = q3_ref[tok_rows(bb), head_sl(g), :].reshape(T * GROUP_A, HD_A).astype(BF16)
            s_ref[g, bb * QR:(bb + 1) * QR, :] = _nt_dot(q, k_all_b[:, lane_sl(g)])

    lane_t = (lax.broadcasted_iota(jnp.int32, (1, LANES), 1) % T).astype(F32)
    lane_b = lax.broadcasted_iota(jnp.int32, (HD_B, LANES), 1) // T
    slot0 = (i % (LANE_BATCH // BB)) * BB
    for h in range(N_HEADS_B):
        hc = slice(h * HD_B, (h + 1) * HD_B)
        lg = LOG_G[h]
        kdec = jnp.exp(lg * (T - 1.0 - lane_t))
        kt_dec = kbt_ref[hc, :] * kdec
        for bb in range(BB):
            kt_ref[h, bb * HD_B:(bb + 1) * HD_B, :] = jnp.where(lane_b == slot0 + bb, kt_dec, 0.0).astype(BF16)
        u = jnp.dot(kt_ref[h], vbg_ref[:, hc].astype(BF16), preferred_element_type=F32)
        for bb in range(BB):
            r = st_ref[bb, h]
            pr = slice((bb // 2) * 2 * T, (bb // 2 + 1) * 2 * T)
            x_ref[bb * N_HEADS_B + h] = jnp.dot(qb_ref[pr, hc].astype(BF16), r.astype(BF16),
                                                 preferred_element_type=F32)
            nst_ref[bb, h] = math.exp(lg * T) * r + u[bb * HD_B:(bb + 1) * HD_B, :]

    ri = lax.broadcasted_iota(jnp.int32, (R, R), 0)
    ci = lax.broadcasted_iota(jnp.int32, (R, R), 1)
    same_b = (ri // T) == (ci // T)
    dt = (ri % T - ci % T).astype(F32)
    keep = same_b & (dt >= 0)
    for h in range(N_HEADS_B):
        hc = slice(h * HD_B, (h + 1) * HD_B)
        dmask = jnp.where(keep, jnp.exp(LOG_G[h] * jnp.maximum(dt, 0.0)), 0.0)
        sc = _nt_dot(qb_ref[:, hc].astype(BF16), kb_ref[:, hc].astype(BF16)) * dmask
        sc_ref[h] = sc.astype(BF16)

    rows_q = (lax.broadcasted_iota(jnp.int32, (BB * QR, n_keys), 0) % QR) // GROUP_A
    key = lax.broadcasted_iota(jnp.int32, (BB * QR, n_keys), 1)
    delta = jnp.where(key < L, rows_q + L - key, rows_q - (key - L))
    valid = (delta >= 0) & (delta < WINDOW)
    for g in range(N_KV_A):
        s = jnp.where(valid, s_ref[g] * (HD_A ** -0.5), -jnp.inf)
        sink = sinkrow_ref[g]
        m = jnp.maximum(jnp.max(s, axis=-1, keepdims=True), sink)
        e = jnp.exp(s - m)
        denom = jnp.sum(e, axis=-1, keepdims=True) + jnp.exp(sink - m)
        p_ref[g] = (e / denom).astype(BF16)

    tposf = (lax.broadcasted_iota(jnp.int32, (R, 1), 0) % T).astype(F32)
    upper = (lax.broadcasted_iota(jnp.int32, (2 * T, 1), 0) < T)
    for h in range(N_HEADS_B):
        hc = slice(h * HD_B, (h + 1) * HD_B)
        intra = jnp.dot(sc_ref[h], vb_ref[:, hc].astype(BF16), preferred_element_type=F32)
        inter = jnp.concatenate(
            [jnp.where(upper, x_ref[(2 * pair) * N_HEADS_B + h], x_ref[(2 * pair + 1) * N_HEADS_B + h])
             for pair in range(BB // 2)], axis=0)
        o = intra + inter * jnp.exp(LOG_G[h] * (tposf + 1.0))
        mu = jnp.mean(o, axis=-1, keepdims=True)
        d = o - mu
        var = jnp.mean(d * d, axis=-1, keepdims=True)
        yb_ref[:, hc] = d * lax.rsqrt(var + EPS) * gnw_ref[:, hc] * _silu(gb_ref[:, hc])

    for bb in range(BB):
        for g in range(N_KV_A):
            o = jnp.dot(p_ref[g, bb * QR:(bb + 1) * QR, :], vall_ref[bb, :, lane_sl(g)],
                        preferred_element_type=F32)
            gate = _silu(ga3_ref[tok_rows(bb), head_sl(g), :].reshape(QR, HD_A))
            ya3_ref[tok_rows(bb), head_sl(g), :] = (o * gate).reshape(T, GROUP_A, HD_A)


def _mix_sample(ps, sinks, gn_w, cache_k, cache_v, state, d, prev_out, t_new):
    m = ps.shape[0]
    nb = m // t_new
    L = cache_k.shape[2]
    BB = SAMPLE_BB
    n_alias = len(prev_out)
    n_in = 14
    rows = BB * t_new
    qa = ps[:, SEC_QA * W_A:(SEC_QA + 1) * W_A].reshape(m, N_HEADS_A, HD_A)
    ga = ps[:, SEC_GA * W_A:(SEC_GA + 1) * W_A].reshape(m, N_HEADS_A, HD_A)
    kbt = ps[:, SEC_KB * W_A:(SEC_KB + 1) * W_A].T
    sinkrow = jnp.tile(sinks.reshape(N_KV_A, 1, GROUP_A), (1, BB * t_new, 1)).reshape(N_KV_A, -1, 1)
    sec_spec = lambda s: pl.BlockSpec((rows, W_A), lambda i, s=s: (i, s))
    steps_per_lane_group = LANE_BATCH // BB
    return pl.pallas_call(
        functools.partial(_mix_sample_kernel, t_new=t_new),
        grid=(nb // BB,),
        in_specs=[
            pl.BlockSpec((N_KV_A, BB * t_new * GROUP_A, 1), lambda i: (0, 0, 0)),
            pl.BlockSpec((rows, N_HEADS_A, HD_A), lambda i: (i, 0, 0)),
            pl.BlockSpec((rows, N_HEADS_A, HD_A), lambda i: (i, 0, 0)),
            sec_spec(SEC_QB), sec_spec(SEC_KB), sec_spec(SEC_VB), sec_spec(SEC_GB),
            pl.BlockSpec((rows, COL_TILE), lambda i: (i, KV_BLOCK)),
            pl.BlockSpec((W_B, LANES), lambda i: (0, i // steps_per_lane_group)),
            pl.BlockSpec((LANES, W_B), lambda i: (i // steps_per_lane_group, SEC_VB)),
            pl.BlockSpec((None, BB, L, LANES), lambda i: (d, i, 0, 0)),
            pl.BlockSpec((None, BB, L, LANES), lambda i: (d, i, 0, 0)),
            pl.BlockSpec((None, BB, N_HEADS_B, HD_B, HD_B), lambda i: (d, i, 0, 0, 0)),
            pl.BlockSpec((1, W_B), lambda i: (0, 0)),
        ] + [pl.BlockSpec(memory_space=pl.ANY)] * n_alias,
        out_specs=[
            pl.BlockSpec((rows, N_HEADS_A, HD_A), lambda i: (i, 0, 0)),
            pl.BlockSpec((rows, W_B), lambda i: (i, 0)),
            pl.BlockSpec((None, BB, L, LANES), lambda i: (d, i, 0, 0)),
            pl.BlockSpec((None, BB, L, LANES), lambda i: (d, i, 0, 0)),
            pl.BlockSpec((None, BB, N_HEADS_B, HD_B, HD_B), lambda i: (d, i, 0, 0, 0)),
        ],
        out_shape=[
            jax.ShapeDtypeStruct((m, N_HEADS_A, HD_A), F32),
            jax.ShapeDtypeStruct((m, W_B), F32),
            jax.ShapeDtypeStruct(cache_k.shape, F32),
            jax.ShapeDtypeStruct(cache_v.shape, F32),
            jax.ShapeDtypeStruct(state.shape, F32),
        ],
        input_output_aliases={n_in + a: 2 + a for a in range(n_alias)},
        scratch_shapes=[
            pltpu.VMEM((N_KV_A, BB * t_new * GROUP_A, L + t_new), F32),
            pltpu.VMEM((N_KV_A, BB * t_new * GROUP_A, L + t_new), BF16),
            pltpu.VMEM((BB, L + t_new, LANES), BF16),
            pltpu.VMEM((N_HEADS_B, BB * HD_B, LANES), BF16),
            pltpu.VMEM((BB * N_HEADS_B, 2 * t_new, HD_B), F32),
            pltpu.VMEM((N_HEADS_B, rows, rows), BF16),
        ],
        compiler_params=pltpu.CompilerParams(
            dimension_semantics=("arbitrary",),
            vmem_limit_bytes=VMEM_LIMIT),
        name="mix_sample",
    )(sinkrow, qa, ga, ps, ps, ps, ps, ps, kbt, ps, cache_k, cache_v, state, gn_w, *prev_out)


def _rope_tables(pos=None, n=None):
    def tab(hd):
        half = hd // 2
        lane = jnp.arange(LANES)
        inv = ROPE_THETA ** (-(lane % half).astype(F32) / half)
        sign = jnp.where((lane % hd) < half, -1.0, 1.0).astype(F32)
        if pos is not None:
            ang = pos.astype(F32)[:, None] * inv[None, :]
            return jnp.cos(ang), jnp.sin(ang) * sign
        hi = (jnp.arange(n // LANES, dtype=F32) * LANES)[:, None, None] * inv
        lo = jnp.arange(LANES, dtype=F32)[None, :, None] * inv
        ch, sh, cl, sl = jnp.cos(hi), jnp.sin(hi), jnp.cos(lo), jnp.sin(lo)
        return ((ch * cl - sh * sl).reshape(n, LANES), ((sh * cl + ch * sl) * sign).reshape(n, LANES))
    return tab(HD_A) + tab(HD_B)


def kernel(x_prompt, x_sample, cache_k_win, cache_v_win, state_ret, norm_w, w_in, attn_sinks,
           ret_norm_w, w_out, final_norm_w):
    bp, tp, _ = x_prompt.shape
    bs, ts, _ = x_sample.shape
    assert bp == 1 and tp % RET_CHUNK == 0 and min(WINDOW, tp) == WINDOW
    L = cache_k_win.shape[2]
    depth = w_in.shape[0]

    tab_p = _rope_tables(n=tp)
    tab_s = _rope_tables(pos=jnp.tile(PAST_LEN + jnp.arange(ts, dtype=jnp.int32), bs))
    fw = final_norm_w.reshape(1, D_MODEL)

    xp = x_prompt.reshape(tp, D_MODEL)
    xs = x_sample.reshape(bs * ts, D_MODEL)
    ck = cache_k_win.reshape(depth, bs, L, N_KV_A * HD_A)
    cv = cache_v_win.reshape(depth, bs, L, N_KV_A * HD_A)

    kp_l, vp_l, rp_l = [], [], []
    sample_out = ()
    for d in range(depth):
        final = d == depth - 1
        nw = norm_w[d].reshape(1, D_MODEL)
        gnw = ret_norm_w[d].reshape(1, W_B)
        ps, _, w_in_b, w_kv_b = _in_proj(xs, nw, w_in, w_in, d, tab_s, F32, tm=bs * ts, emit_w=True)
        ya3, yb, *sample_out = _mix_sample(ps, attn_sinks[d], gnw, ck, cv, state_ret, d, sample_out, ts)
        ys = jnp.concatenate([ya3.reshape(bs * ts, W_A), yb], axis=1).astype(BF16)
        xs, w_out_b = _out_proj(ys, xs, w_out, d, fw, final)
        p, tail = _in_proj(xp, nw, w_in_b, w_kv_b, d, tab_p, BF16, tm=1024, emit_w=False)
        xp, r_fin = _mix_prompt(p, xp, attn_sinks[d], gnw, w_out_b, fw, final)
        kp_l.append(tail[:, :LANES].reshape(WINDOW, N_KV_A, HD_A))
        vp_l.append(tail[:, LANES:].reshape(WINDOW, N_KV_A, HD_A))
        rp_l.append(r_fin)

    keep = min(WINDOW, tp)
    nk, nv, nst = sample_out
    return (xp.reshape(bp, tp, D_MODEL), xs.reshape(bs, ts, D_MODEL),
            jnp.stack(kp_l).reshape(depth, bp, keep, N_KV_A, HD_A),
            jnp.stack(vp_l).reshape(depth, bp, keep, N_KV_A, HD_A),
            jnp.stack(rp_l).reshape(depth, bp, N_HEADS_B, HD_B, HD_B),
            nk.reshape(depth, bs, L, N_KV_A, HD_A), nv.reshape(depth, bs, L, N_KV_A, HD_A), nst)
```

```python
import functools
import math

import jax
import jax.numpy as jnp
from jax import lax
from jax.experimental import pallas as pl
from jax.experimental.pallas import tpu as pltpu

F32 = jnp.float32
BF16 = jnp.bfloat16

D_MODEL = 2048
PAST_LEN = 8192
WINDOW = 128
HD_A = 64
N_HEADS_A = 16
N_KV_A = 2
GROUP_A = N_HEADS_A // N_KV_A
W_A = N_HEADS_A * HD_A
HD_B = 128
N_HEADS_B = 8
W_B = N_HEADS_B * HD_B
RET_CHUNK = 128
ROPE_THETA = 10000.0
EPS = 1e-6

LANES = 128
MXU_WIDTH = 256
SEC_QA, SEC_GA, SEC_QB, SEC_KB, SEC_VB, SEC_GB = 0, 1, 2, 3, 4, 5
IN_TILE = 1024
TILES_PER_SEC = W_A // IN_TILE
KV_TILE = 6 * TILES_PER_SEC
COL_TILE = 2 * N_KV_A * HD_A
KV_BLOCK = KV_TILE * IN_TILE // COL_TILE
D_PROJ_R = (KV_TILE + 1) * IN_TILE
VMEM_LIMIT = 56 * 1024 * 1024
IN_ROWS = 1024
NORM_ROWS = 256

LOG_G = [math.log1p(-(2.0 ** (-5.0 - h))) for h in range(N_HEADS_B)]


def _silu(g):
    return g * (1.0 / (1.0 + jnp.exp(-g)))


EPI_ROPE_A, EPI_ROPE_B, EPI_ROPE_B_SCALED, EPI_PLAIN = 0, 1, 2, 3
_SEC_EPI = (EPI_ROPE_A, EPI_PLAIN, EPI_ROPE_B, EPI_ROPE_B_SCALED, EPI_PLAIN, EPI_PLAIN)


KV_ROPE_GROUPS = N_KV_A * HD_A // LANES
N_W_BLOCKS = IN_TILE // MXU_WIDTH


def _step_tile(step):
    return jnp.where(step == 0, KV_TILE, step - 1)


def _tile_epi(tile):
    sec = tile // TILES_PER_SEC
    t = jnp.int32(EPI_ROPE_A)
    for s, e in enumerate(_SEC_EPI):
        t = jnp.where(sec == s, e, t)
    return t


def _in_proj_kernel(x_ref, nw_ref, *refs, row_chunk, emit_w):
    n_w = N_W_BLOCKS if emit_w else 1
    w_refs = refs[:n_w]
    wkv_in_ref, ca_ref, sa_ref, cb_ref, sb_ref, o_ref, tail_ref = refs[n_w:n_w + 7]
    w_tile_ref = refs[n_w + 7] if emit_w else w_refs[0]
    wkv_ref = refs[n_w + 8] if emit_w else wkv_in_ref
    h_ref = refs[-1]
    step = pl.program_id(1)
    j = _step_tile(step)

    def weights(e):
        cols = slice(e * MXU_WIDTH, (e + 1) * MXU_WIDTH)
        if emit_w:
            w_tile_ref[:, cols] = w_refs[e][...].astype(BF16)
        return w_tile_ref[:, cols]

    tm = x_ref.shape[0]
    n_chunks = tm // row_chunk

    @pl.when(step == 0)
    def _():
        def body(r, carry):
            rows = pl.ds(pl.multiple_of(r * row_chunk, row_chunk), row_chunk)
            x = x_ref[rows, :]
            ms = jnp.mean(x * x, axis=-1, keepdims=True)
            h_ref[rows, :] = (x * lax.rsqrt(ms + EPS) * nw_ref[...]).astype(BF16)
            return carry
        lax.fori_loop(0, n_chunks, body, 0)

    lane = lax.broadcasted_iota(jnp.int32, (tm, LANES), 1)
    first_half = (lane % HD_A) < HD_A // 2

    def rotate(x, c, s, shift_up, shift_dn):
        partner = jnp.where(first_half, pltpu.roll(x, shift_up, 1), pltpu.roll(x, shift_dn, 1))
        return x * c + partner * s

    epi = _tile_epi(j)
    is_plain = epi == EPI_PLAIN

    @pl.when((j < KV_TILE) & is_plain)
    def _():
        for e in range(N_W_BLOCKS):
            cols = slice(e * MXU_WIDTH, (e + 1) * MXU_WIDTH)
            o_ref[:, cols] = jnp.dot(h_ref[...], weights(e), preferred_element_type=F32).astype(o_ref.dtype)

    def rotated_tile(rot):
        for e in range(N_W_BLOCKS):
            acc = jnp.dot(h_ref[...], weights(e), preferred_element_type=F32)
            for gg in range(MXU_WIDTH // LANES):
                cols = slice(e * MXU_WIDTH + gg * LANES, e * MXU_WIDTH + (gg + 1) * LANES)
                o_ref[:, cols] = rot(acc[:, gg * LANES:(gg + 1) * LANES]).astype(o_ref.dtype)

    @pl.when((j < KV_TILE) & (epi == EPI_ROPE_A))
    def _():
        c, s = ca_ref[...], sa_ref[...]
        rotated_tile(lambda x: rotate(x, c, s, LANES - HD_A // 2, HD_A // 2))

    @pl.when((j < KV_TILE) & ((epi == EPI_ROPE_B) | (epi == EPI_ROPE_B_SCALED)))
    def _():
        scale = jnp.where(epi == EPI_ROPE_B_SCALED, HD_B ** -0.5, 1.0).astype(F32)
        c, s = cb_ref[...] * scale, sb_ref[...] * scale
        rotated_tile(lambda x: x * c + pltpu.roll(x, HD_B // 2, 1) * s)

    @pl.when(j == KV_TILE)
    def _():
        if emit_w:
            wkv_ref[...] = wkv_in_ref[...].astype(BF16)
        acc = jnp.dot(h_ref[...], wkv_ref[...], preferred_element_type=F32)
        for g in range(COL_TILE // LANES):
            cols = slice(g * LANES, (g + 1) * LANES)
            out = acc[:, cols]
            if g < KV_ROPE_GROUPS:
                out = rotate(out, ca_ref[...], sa_ref[...], LANES - HD_A // 2, HD_A // 2)
            o_ref[:, cols] = out.astype(o_ref.dtype)
            tail_ref[:, cols] = out[tm - WINDOW:, :]
        o_ref[:, COL_TILE:] = jnp.zeros((tm, IN_TILE - COL_TILE), o_ref.dtype)
        if emit_w:
            w_tile_ref[...] = jnp.zeros(w_tile_ref.shape, BF16)


def _in_proj(x, norm_w, w_src, wkv_src, d, tables, out_dtype, tm, emit_w):
    m = x.shape[0]
    n_tiles = D_PROJ_R // IN_TILE
    assert m == tm or not emit_w
    n_src_blocks = (6 * W_A + 2 * N_KV_A * HD_A) // MXU_WIDTH

    def src_block(j, e):
        sec = j // TILES_PER_SEC
        blk = sec * (W_A // MXU_WIDTH) + jnp.minimum(sec, 1) + (j % TILES_PER_SEC) * N_W_BLOCKS + e
        return jnp.minimum(blk, n_src_blocks - 1)

    if emit_w:
        w_specs = [pl.BlockSpec((None, D_MODEL, MXU_WIDTH),
                                lambda i, s, e=e: (d, 0, src_block(jnp.maximum(s - 1, 0), e)))
                   for e in range(N_W_BLOCKS)]
        wkv_spec = pl.BlockSpec((None, D_MODEL, MXU_WIDTH), lambda i, j: (d, 0, W_A // MXU_WIDTH))
    else:
        w_specs = [pl.BlockSpec((D_MODEL, IN_TILE),
                                lambda i, s: (0, jnp.minimum(_step_tile(s), KV_TILE - 1)))]
        wkv_spec = pl.BlockSpec((D_MODEL, COL_TILE), lambda i, j: (0, 0))
    tab_spec = pl.BlockSpec((tm, LANES), lambda i, j: (i, 0))

    return pl.pallas_call(
        functools.partial(_in_proj_kernel, row_chunk=min(tm, NORM_ROWS), emit_w=emit_w),
        grid=(m // tm, n_tiles),
        in_specs=[
            pl.BlockSpec((tm, D_MODEL), lambda i, j: (i, 0)),
            pl.BlockSpec((1, D_MODEL), lambda i, j: (0, 0)),
        ] + w_specs + [wkv_spec, tab_spec, tab_spec, tab_spec, tab_spec],
        out_specs=[
            pl.BlockSpec((tm, IN_TILE), lambda i, s: (i, _step_tile(s))),
            pl.BlockSpec((WINDOW, COL_TILE), lambda i, j: (0, 0)),
        ] + ([pl.BlockSpec((D_MODEL, IN_TILE), lambda i, s: (0, _step_tile(s))),
              pl.BlockSpec((D_MODEL, COL_TILE), lambda i, j: (0, 0))] if emit_w else []),
        out_shape=[
            jax.ShapeDtypeStruct((m, D_PROJ_R), out_dtype),
            jax.ShapeDtypeStruct((WINDOW, COL_TILE), F32),
        ] + ([jax.ShapeDtypeStruct((D_MODEL, D_PROJ_R), BF16),
              jax.ShapeDtypeStruct((D_MODEL, COL_TILE), BF16)] if emit_w else []),
        scratch_shapes=[pltpu.VMEM((tm, D_MODEL), BF16)],
        compiler_params=pltpu.CompilerParams(
            dimension_semantics=("arbitrary", "arbitrary"),
            vmem_limit_bytes=VMEM_LIMIT),
        name="in_proj",
    )(x, norm_w, *([w_src] * len(w_specs)), wkv_src, *tables)


OUT_COL_STEPS = 4


def _out_proj_kernel(y_ref, x_ref, w_ref, fw_ref, o_ref, wout_ref, *, final):
    c = pl.program_id(0)
    tn = w_ref.shape[1]
    for cc in range(OUT_COL_STEPS):
        @pl.when(c == cc)
        def _(cc=cc):
            cols = slice(cc * tn, (cc + 1) * tn)
            wout_ref[...] = w_ref[...].astype(BF16)
            o_ref[:, cols] = x_ref[:, cols] + jnp.dot(y_ref[...], wout_ref[...], preferred_element_type=F32)
    if final:
        @pl.when(c == OUT_COL_STEPS - 1)
        def _():
            acc = o_ref[...]
            ms = jnp.mean(acc * acc, axis=-1, keepdims=True)
            o_ref[...] = acc * lax.rsqrt(ms + EPS) * fw_ref[...]


def _out_proj(y, x, w_all, d, final_w, final):
    m = x.shape[0]
    tn = D_MODEL // OUT_COL_STEPS
    return pl.pallas_call(
        functools.partial(_out_proj_kernel, final=final),
        grid=(OUT_COL_STEPS,),
        in_specs=[
            pl.BlockSpec((m, D_MODEL), lambda c: (0, 0)),
            pl.BlockSpec((m, D_MODEL), lambda c: (0, 0)),
            pl.BlockSpec((None, D_MODEL, tn), lambda c: (d, 0, c)),
            pl.BlockSpec((1, D_MODEL), lambda c: (0, 0)),
        ],
        out_specs=[
            pl.BlockSpec((m, D_MODEL), lambda c: (0, 0)),
            pl.BlockSpec((D_MODEL, tn), lambda c: (0, c)),
        ],
        out_shape=[
            jax.ShapeDtypeStruct((m, D_MODEL), F32),
            jax.ShapeDtypeStruct((D_MODEL, D_MODEL), BF16),
        ],
        compiler_params=pltpu.CompilerParams(
            dimension_semantics=("arbitrary",),
            vmem_limit_bytes=VMEM_LIMIT),
        name="out_proj",
    )(y, x, w_all, final_w)


def _nt_dot(a, b):
    return lax.dot_general(a, b, (((1,), (1,)), ((), ())), preferred_element_type=F32)


def _tn_dot(a, b):
    return lax.dot_general(a, b, (((0,), (0,)), ((), ())), preferred_element_type=F32)


MIX_CHUNKS = 4
OUT_PIECES = D_MODEL // MXU_WIDTH
PIECES_PER_CHUNK = OUT_PIECES // MIX_CHUNKS


def _mix_prompt_kernel(sink_ref, qa_ref, ga_ref, qb_ref, kb_ref, vb_ref, gb_ref,
                       kv_ref, gnw_ref, x_ref, wo_ref, fw_ref, xo_ref, r_ref,
                       dmask_ref, qdec_ref, kdec_ref, kvd_ref,
                       qs_ref, s_ref, p_ref, o_ref, scb_ref, inter_ref, y_ref, *, final):
    step = pl.program_id(0)
    last_step = pl.num_programs(0) - 1
    slot = step % 2
    prev_slot = 1 - slot
    C = RET_CHUNK
    row = lax.broadcasted_iota(jnp.int32, (C, LANES), 0)
    lane = lax.broadcasted_iota(jnp.int32, (C, LANES), 1)

    @pl.when(step == 0)
    def _():
        r_ref[...] = jnp.zeros(r_ref.shape, F32)
        kvd_ref[...] = jnp.zeros(kvd_ref.shape, kvd_ref.dtype)
        y_ref[1] = jnp.zeros(y_ref.shape[1:], y_ref.dtype)
        rowf = row.astype(F32)
        diff = rowf - lane.astype(F32)
        for h in range(N_HEADS_B):
            dmask_ref[h] = jnp.where(diff >= 0, jnp.exp(LOG_G[h] * jnp.maximum(diff, 0.0)), 0.0)
            qdec_ref[h] = jnp.exp(LOG_G[h] * (rowf + 1.0))
            kdec_ref[h] = jnp.exp(LOG_G[h] * (C - 1.0 - rowf))

    tri = lane <= row
    lo = lane < HD_A
    neg_inf = jnp.float32(-jnp.inf)
    half_group = GROUP_A // 2
    kcol = lambda g: slice(g * LANES, (g + 1) * LANES)
    vcol = lambda g: slice((N_KV_A + g) * LANES, (N_KV_A + g + 1) * LANES)
    pair_col = lambda g, pi: slice((g * half_group + pi) * LANES, (g * half_group + pi + 1) * LANES)
    head_rows = lambda hh: slice(hh * C, (hh + 1) * C)

    def out_piece(e):
        cols = slice(e * MXU_WIDTH, (e + 1) * MXU_WIDTH)
        xo_ref[:, cols] = x_ref[:, cols] + jnp.dot(y_ref[prev_slot], wo_ref[:, cols],
                                                   preferred_element_type=F32)

    def finish_out():
        if final:
            acc = xo_ref[...]
            ms = jnp.mean(acc * acc, axis=-1, keepdims=True)
            xo_ref[...] = acc * lax.rsqrt(ms + EPS) * fw_ref[...]

    def chunk(cc):
        rows = slice(cc * C, (cc + 1) * C)
        has_prev = step * MIX_CHUNKS + cc > 0
        for t in range(2):
            w = kv_ref[rows, t * LANES:(t + 1) * LANES].astype(F32)
            w_swapped = pltpu.roll(w, HD_A, 1)
            kvd_ref[C:, (2 * t) * LANES:(2 * t + 1) * LANES] = jnp.where(lo, w, w_swapped).astype(BF16)
            kvd_ref[C:, (2 * t + 1) * LANES:(2 * t + 2) * LANES] = jnp.where(lo, w_swapped, w).astype(BF16)
        kv2 = lambda cols: kvd_ref[:, cols]

        for g in range(N_KV_A):
            for pi in range(half_group):
                q_pair = qa_ref[rows, pair_col(g, pi)].astype(F32)
                qs_ref[g, head_rows(2 * pi), :] = jnp.where(lo, q_pair, 0.0).astype(BF16)
                qs_ref[g, head_rows(2 * pi + 1), :] = jnp.where(lo, 0.0, q_pair).astype(BF16)
            s_ref[g] = _nt_dot(qs_ref[g], kv2(kcol(g)))

        out_piece(cc * PIECES_PER_CHUNK)

        for h in range(N_HEADS_B):
            hc = slice(h * HD_B, (h + 1) * HD_B)
            q, k, v = qb_ref[rows, hc], kb_ref[rows, hc], vb_ref[rows, hc]
            scb_ref[h] = (_nt_dot(q, k) * dmask_ref[h]).astype(BF16)
            r = r_ref[h]
            inter_ref[h] = jnp.dot(q, r.astype(BF16), preferred_element_type=F32) * qdec_ref[h]
            kd = (k.astype(F32) * kdec_ref[h]).astype(BF16)
            r_ref[h] = math.exp(LOG_G[h] * C) * r + _tn_dot(kd, v)

        for g in range(N_KV_A):
            for hh in range(GROUP_A):
                hr = head_rows(hh)
                s_prev = jnp.where(has_prev, s_ref[g, hr, :LANES], neg_inf)
                s = jnp.where(tri, s_ref[g, hr, LANES:], s_prev) * (HD_A ** -0.5)
                sink = sink_ref[g * GROUP_A + hh]
                m = jnp.maximum(jnp.max(s, axis=-1, keepdims=True), sink)
                e = jnp.exp(s - m)
                denom = jnp.sum(e, axis=-1, keepdims=True) + jnp.exp(sink - m)
                p = e / denom
                p_ref[g, hr, :LANES] = jnp.where(tri, 0.0, p).astype(BF16)
                p_ref[g, hr, LANES:] = jnp.where(tri, p, 0.0).astype(BF16)

        for g in range(N_KV_A):
            o_ref[g] = jnp.dot(p_ref[g], kv2(vcol(g)), preferred_element_type=F32)

        for e in range(cc * PIECES_PER_CHUNK + 1, (cc + 1) * PIECES_PER_CHUNK):
            out_piece(e)

        for h in range(N_HEADS_B):
            hc = slice(h * HD_B, (h + 1) * HD_B)
            o = jnp.dot(scb_ref[h], vb_ref[rows, hc], preferred_element_type=F32) + inter_ref[h]
            mu = jnp.mean(o, axis=-1, keepdims=True)
            d = o - mu
            var = jnp.mean(d * d, axis=-1, keepdims=True)
            yh = d * lax.rsqrt(var + EPS) * gnw_ref[:, hc] * _silu(gb_ref[rows, hc].astype(F32))
            y_ref[slot, rows, W_A + h * HD_B: W_A + (h + 1) * HD_B] = yh.astype(y_ref.dtype)

        for g in range(N_KV_A):
            for pi in range(half_group):
                pc = pair_col(g, pi)
                o_pair = jnp.where(lo, o_ref[g, head_rows(2 * pi), :], o_ref[g, head_rows(2 * pi + 1), :])
                y_ref[slot, rows, pc] = (o_pair * _silu(ga_ref[rows, pc].astype(F32))).astype(y_ref.dtype)

        kvd_ref[:C, :] = kvd_ref[C:, :]

    @pl.when(step < last_step)
    def _():
        for cc in range(MIX_CHUNKS):
            chunk(cc)
        finish_out()

    @pl.when(step == last_step)
    def _():
        for e in range(OUT_PIECES):
            out_piece(e)
        finish_out()


def _mix_prompt(p, x, sinks, gn_w, w_out_b, final_w, final):
    t = p.shape[0]
    C = RET_CHUNK
    rows = MIX_CHUNKS * C
    n_blocks = t // rows
    cur = lambda c: jnp.minimum(c, n_blocks - 1)
    prev = lambda c: jnp.maximum(c - 1, 0)
    sec_spec = lambda s: pl.BlockSpec((rows, W_A), lambda c, s=s: (cur(c), s))
    return pl.pallas_call(
        functools.partial(_mix_prompt_kernel, final=final),
        grid=(n_blocks + 1,),
        in_specs=[
            pl.BlockSpec(memory_space=pltpu.SMEM),
            sec_spec(SEC_QA), sec_spec(SEC_GA), sec_spec(SEC_QB),
            sec_spec(SEC_KB), sec_spec(SEC_VB), sec_spec(SEC_GB),
            pl.BlockSpec((rows, COL_TILE), lambda c: (cur(c), KV_BLOCK)),
            pl.BlockSpec((1, W_B), lambda c: (0, 0)),
            pl.BlockSpec((rows, D_MODEL), lambda c: (prev(c), 0)),
            pl.BlockSpec((D_MODEL, D_MODEL), lambda c: (0, 0), pipeline_mode=pl.Buffered(1)),
            pl.BlockSpec((1, D_MODEL), lambda c: (0, 0)),
        ],
        out_specs=[
            pl.BlockSpec((rows, D_MODEL), lambda c: (prev(c), 0)),
            pl.BlockSpec((N_HEADS_B, HD_B, HD_B), lambda c: (0, 0, 0)),
        ],
        out_shape=[
            jax.ShapeDtypeStruct((t, D_MODEL), F32),
            jax.ShapeDtypeStruct((N_HEADS_B, HD_B, HD_B), F32),
        ],
        scratch_shapes=[pltpu.VMEM((N_HEADS_B, C, LANES), F32)] * 3 + [
            pltpu.VMEM((2 * C, 2 * COL_TILE), BF16),
            pltpu.VMEM((N_KV_A, GROUP_A * C, LANES), BF16),
            pltpu.VMEM((N_KV_A, GROUP_A * C, 2 * LANES), F32),
            pltpu.VMEM((N_KV_A, GROUP_A * C, 2 * LANES), BF16),
            pltpu.VMEM((N_KV_A, GROUP_A * C, LANES), F32),
            pltpu.VMEM((N_HEADS_B, C, HD_B), BF16),
            pltpu.VMEM((N_HEADS_B, C, HD_B), F32),
            pltpu.VMEM((2, rows, D_MODEL), BF16),
        ],
        compiler_params=pltpu.CompilerParams(
            dimension_semantics=("arbitrary",),
            vmem_limit_bytes=VMEM_LIMIT),
        name="mix_prompt",
    )(sinks, p, p, p, p, p, p, p, gn_w, x, w_out_b, final_w)


SAMPLE_BB = 16
LANE_BATCH = LANES // 4


def _mix_sample_kernel(sinkrow_ref, q3_ref, ga3_ref, qb_ref, kb_ref, vb_ref, gb_ref, kvn_ref,
                       kbt_ref, vbg_ref, ck_ref, cv_ref, st_ref, gnw_ref, *rest, t_new):
    ya3_ref, yb_ref, nk_ref, nv_ref, nst_ref, s_ref, p_ref, vall_ref, kt_ref, x_ref, sc_ref = rest[-11:]
    i = pl.program_id(0)
    T = t_new
    BB = SAMPLE_BB
    L = ck_ref.shape[1]
    n_keys = L + T
    QR = T * GROUP_A
    R = BB * T
    head_sl = lambda g: slice(g * GROUP_A, (g + 1) * GROUP_A)
    lane_sl = lambda g: slice(g * HD_A, (g + 1) * HD_A)
    tok_rows = lambda bb: slice(bb * T, (bb + 1) * T)


    for bb in range(BB):
        k_new = kvn_ref[tok_rows(bb), :LANES]
        v_new = kvn_ref[tok_rows(bb), LANES:]
        k_all = jnp.concatenate([ck_ref[bb], k_new], axis=0)
        v_all = jnp.concatenate([cv_ref[bb], v_new], axis=0)
        nk_ref[bb] = k_all[T:, :]
        nv_ref[bb] = v_all[T:, :]
        vall_ref[bb] = v_all.astype(BF16)
        k_all_b = k_all.astype(BF16)
        for g in range(N_KV_A):
            q = q3_ref[tok_rows(bb), head_sl(g), :].reshape(T * GROUP_A, HD_A).astype(BF16)
            s_ref[g, bb * QR:(bb + 1) * QR, :] = _nt_dot(q, k_all_b[:, lane_sl(g)])

    lane_t = (lax.broadcasted_iota(jnp.int32, (1, LANES), 1) % T).astype(F32)
    lane_b = lax.broadcasted_iota(jnp.int32, (HD_B, LANES), 1) // T
    slot0 = (i % (LANE_BATCH // BB)) * BB
    for h in range(N_HEADS_B):
        hc = slice(h * HD_B, (h + 1) * HD_B)
        lg = LOG_G[h]
        kdec = jnp.exp(lg * (T - 1.0 - lane_t))
        kt_dec = kbt_ref[hc, :] * kdec
        for bb in range(BB):
            kt_ref[h, bb * HD_B:(bb + 1) * HD_B, :] = jnp.where(lane_b == slot0 + bb, kt_dec, 0.0).astype(BF16)
        u = jnp.dot(kt_ref[h], vbg_ref[:, hc].astype(BF16), preferred_element_type=F32)
        for bb in range(BB):
            r = st_ref[bb, h]
            pr = slice((bb // 2) * 2 * T, (bb // 2 + 1) * 2 * T)
            x_ref[bb * N_HEADS_B + h] = jnp.dot(qb_ref[pr, hc].astype(BF16), r.astype(BF16),
                                                 preferred_element_type=F32)
            nst_ref[bb, h] = math.exp(lg * T) * r + u[bb * HD_B:(bb + 1) * HD_B, :]

    ri = lax.broadcasted_iota(jnp.int32, (R, R), 0)
    ci = lax.broadcasted_iota(jnp.int32, (R, R), 1)
    same_b = (ri // T) == (ci // T)
    dt = (ri % T - ci % T).astype(F32)
    keep = same_b & (dt >= 0)
    for h in range(N_HEADS_B):
        hc = slice(h * HD_B, (h + 1) * HD_B)
        dmask = jnp.where(keep, jnp.exp(LOG_G[h] * jnp.maximum(dt, 0.0)), 0.0)
        sc = _nt_dot(qb_ref[:, hc].astype(BF16), kb_ref[:, hc].astype(BF16)) * dmask
        sc_ref[h] = sc.astype(BF16)

    rows_q = (lax.broadcasted_iota(jnp.int32, (BB * QR, n_keys), 0) % QR) // GROUP_A
    key = lax.broadcasted_iota(jnp.int32, (BB * QR, n_keys), 1)
    delta = jnp.where(key < L, rows_q + L - key, rows_q - (key - L))
    valid = (delta >= 0) & (delta < WINDOW)
    for g in range(N_KV_A):
        s = jnp.where(valid, s_ref[g] * (HD_A ** -0.5), -jnp.inf)
        sink = sinkrow_ref[g]
        m = jnp.maximum(jnp.max(s, axis=-1, keepdims=True), sink)
        e = jnp.exp(s - m)
        denom = jnp.sum(e, axis=-1, keepdims=True) + jnp.exp(sink - m)
        p_ref[g] = (e / denom).astype(BF16)

    tposf = (lax.broadcasted_iota(jnp.int32, (R, 1), 0) % T).astype(F32)
    upper = (lax.broadcasted_iota(jnp.int32, (2 * T, 1), 0) < T)
    for h in range(N_HEADS_B):
        hc = slice(h * HD_B, (h + 1) * HD_B)
        intra = jnp.dot(sc_ref[h], vb_ref[:, hc].astype(BF16), preferred_element_type=F32)
        inter = jnp.concatenate(
            [jnp.where(upper, x_ref[(2 * pair) * N_HEADS_B + h], x_ref[(2 * pair + 1) * N_HEADS_B + h])
             for pair in range(BB // 2)], axis=0)
        o = intra + inter * jnp.exp(LOG_G[h] * (tposf + 1.0))
        mu = jnp.mean(o, axis=-1, keepdims=True)
        d = o - mu
        var = jnp.mean(d * d, axis=-1, keepdims=True)
        yb_ref[:, hc] = d * lax.rsqrt(var + EPS) * gnw_ref[:, hc] * _silu(gb_ref[:, hc])

    for bb in range(BB):
        for g in range(N_KV_A):
            o = jnp.dot(p_ref[g, bb * QR:(bb + 1) * QR, :], vall_ref[bb, :, lane_sl(g)],
                        preferred_element_type=F32)
            gate = _silu(ga3_ref[tok_rows(bb), head_sl(g), :].reshape(QR, HD_A))
            ya3_ref[tok_rows(bb), head_sl(g), :] = (o * gate).reshape(T, GROUP_A, HD_A)


def _mix_sample(ps, sinks, gn_w, cache_k, cache_v, state, d, prev_out, t_new):
    m = ps.shape[0]
    nb = m // t_new
    L = cache_k.shape[2]
    BB = SAMPLE_BB
    n_alias = len(prev_out)
    rows = BB * t_new
    assert LANE_BATCH * t_new == LANES and LANE_BATCH % BB == 0 and nb % LANE_BATCH == 0
    qa = ps[:, SEC_QA * W_A:(SEC_QA + 1) * W_A].reshape(m, N_HEADS_A, HD_A)
    ga = ps[:, SEC_GA * W_A:(SEC_GA + 1) * W_A].reshape(m, N_HEADS_A, HD_A)
    kbt = ps[:, SEC_KB * W_A:(SEC_KB + 1) * W_A].T
    sinkrow = jnp.tile(sinks.reshape(N_KV_A, 1, GROUP_A), (1, BB * t_new, 1)).reshape(N_KV_A, -1, 1)
    sec_spec = lambda s: pl.BlockSpec((rows, W_A), lambda i, s=s: (i, s))
    steps_per_lane_group = LANE_BATCH // BB
    operands = (sinkrow, qa, ga, ps, ps, ps, ps, ps, kbt, ps, cache_k, cache_v, state, gn_w)
    n_in = len(operands)
    return pl.pallas_call(
        functools.partial(_mix_sample_kernel, t_new=t_new),
        grid=(nb // BB,),
        in_specs=[
            pl.BlockSpec((N_KV_A, BB * t_new * GROUP_A, 1), lambda i: (0, 0, 0)),
            pl.BlockSpec((rows, N_HEADS_A, HD_A), lambda i: (i, 0, 0)),
            pl.BlockSpec((rows, N_HEADS_A, HD_A), lambda i: (i, 0, 0)),
            sec_spec(SEC_QB), sec_spec(SEC_KB), sec_spec(SEC_VB), sec_spec(SEC_GB),
            pl.BlockSpec((rows, COL_TILE), lambda i: (i, KV_BLOCK)),
            pl.BlockSpec((W_B, LANES), lambda i: (0, i // steps_per_lane_group)),
            pl.BlockSpec((LANES, W_B), lambda i: (i // steps_per_lane_group, SEC_VB)),
            pl.BlockSpec((None, BB, L, LANES), lambda i: (d, i, 0, 0)),
            pl.BlockSpec((None, BB, L, LANES), lambda i: (d, i, 0, 0)),
            pl.BlockSpec((None, BB, N_HEADS_B, HD_B, HD_B), lambda i: (d, i, 0, 0, 0)),
            pl.BlockSpec((1, W_B), lambda i: (0, 0)),
        ] + [pl.BlockSpec(memory_space=pl.ANY)] * n_alias,
        out_specs=[
            pl.BlockSpec((rows, N_HEADS_A, HD_A), lambda i: (i, 0, 0)),
            pl.BlockSpec((rows, W_B), lambda i: (i, 0)),
            pl.BlockSpec((None, BB, L, LANES), lambda i: (d, i, 0, 0)),
            pl.BlockSpec((None, BB, L, LANES), lambda i: (d, i, 0, 0)),
            pl.BlockSpec((None, BB, N_HEADS_B, HD_B, HD_B), lambda i: (d, i, 0, 0, 0)),
        ],
        out_shape=[
            jax.ShapeDtypeStruct((m, N_HEADS_A, HD_A), F32),
            jax.ShapeDtypeStruct((m, W_B), F32),
            jax.ShapeDtypeStruct(cache_k.shape, F32),
            jax.ShapeDtypeStruct(cache_v.shape, F32),
            jax.ShapeDtypeStruct(state.shape, F32),
        ],
        input_output_aliases={n_in + a: 2 + a for a in range(n_alias)},
        scratch_shapes=[
            pltpu.VMEM((N_KV_A, BB * t_new * GROUP_A, L + t_new), F32),
            pltpu.VMEM((N_KV_A, BB * t_new * GROUP_A, L + t_new), BF16),
            pltpu.VMEM((BB, L + t_new, LANES), BF16),
            pltpu.VMEM((N_HEADS_B, BB * HD_B, LANES), BF16),
            pltpu.VMEM((BB * N_HEADS_B, 2 * t_new, HD_B), F32),
            pltpu.VMEM((N_HEADS_B, rows, rows), BF16),
        ],
        compiler_params=pltpu.CompilerParams(
            dimension_semantics=("arbitrary",),
            vmem_limit_bytes=VMEM_LIMIT),
        name="mix_sample",
    )(*operands, *prev_out)


def _rope_tables(pos=None, n=None):
    def tab(hd):
        half = hd // 2
        lane = jnp.arange(LANES)
        inv = ROPE_THETA ** (-(lane % half).astype(F32) / half)
        sign = jnp.where((lane % hd) < half, -1.0, 1.0).astype(F32)
        if pos is not None:
            ang = pos.astype(F32)[:, None] * inv[None, :]
            return jnp.cos(ang), jnp.sin(ang) * sign
        hi = (jnp.arange(n // LANES, dtype=F32) * LANES)[:, None, None] * inv
        lo = jnp.arange(LANES, dtype=F32)[None, :, None] * inv
        ch, sh, cl, sl = jnp.cos(hi), jnp.sin(hi), jnp.cos(lo), jnp.sin(lo)
        return ((ch * cl - sh * sl).reshape(n, LANES), ((sh * cl + ch * sl) * sign).reshape(n, LANES))
    return tab(HD_A) + tab(HD_B)


def kernel(x_prompt, x_sample, cache_k_win, cache_v_win, state_ret, norm_w, w_in, attn_sinks,
           ret_norm_w, w_out, final_norm_w):
    bp, tp, _ = x_prompt.shape
    bs, ts, _ = x_sample.shape
    assert bp == 1 and tp % RET_CHUNK == 0 and min(WINDOW, tp) == WINDOW
    L = cache_k_win.shape[2]
    depth = w_in.shape[0]

    tab_p = _rope_tables(n=tp)
    tab_s = _rope_tables(pos=jnp.tile(PAST_LEN + jnp.arange(ts, dtype=jnp.int32), bs))
    fw = final_norm_w.reshape(1, D_MODEL)

    xp = x_prompt.reshape(tp, D_MODEL)
    xs = x_sample.reshape(bs * ts, D_MODEL)
    ck = cache_k_win.reshape(depth, bs, L, N_KV_A * HD_A)
    cv = cache_v_win.reshape(depth, bs, L, N_KV_A * HD_A)

    kp_l, vp_l, rp_l = [], [], []
    sample_out = ()
    for d in range(depth):
        final = d == depth - 1
        nw = norm_w[d].reshape(1, D_MODEL)
        gnw = ret_norm_w[d].reshape(1, W_B)
        ps, _, w_in_b, w_kv_b = _in_proj(xs, nw, w_in, w_in, d, tab_s, F32, tm=bs * ts, emit_w=True)
        ya3, yb, *sample_out = _mix_sample(ps, attn_sinks[d], gnw, ck, cv, state_ret, d, sample_out, ts)
        ys = jnp.concatenate([ya3.reshape(bs * ts, W_A), yb], axis=1).astype(BF16)
        xs, w_out_b = _out_proj(ys, xs, w_out, d, fw, final)
        p, tail = _in_proj(xp, nw, w_in_b, w_kv_b, d, tab_p, BF16, tm=IN_ROWS, emit_w=False)
        xp, r_fin = _mix_prompt(p, xp, attn_sinks[d], gnw, w_out_b, fw, final)
        kp_l.append(tail[:, :LANES].reshape(WINDOW, N_KV_A, HD_A))
        vp_l.append(tail[:, LANES:].reshape(WINDOW, N_KV_A, HD_A))
        rp_l.append(r_fin)

    keep = min(WINDOW, tp)
    nk, nv, nst = sample_out
    return (xp.reshape(bp, tp, D_MODEL), xs.reshape(bs, ts, D_MODEL),
            jnp.stack(kp_l).reshape(depth, bp, keep, N_KV_A, HD_A),
            jnp.stack(vp_l).reshape(depth, bp, keep, N_KV_A, HD_A),
            jnp.stack(rp_l).reshape(depth, bp, N_HEADS_B, HD_B, HD_B),
            nk.reshape(depth, bs, L, N_KV_A, HD_A), nv.reshape(depth, bs, L, N_KV_A, HD_A), nst)
```

```python
import functools
import math

import jax
import jax.numpy as jnp
from jax import lax
from jax.experimental import pallas as pl
from jax.experimental.pallas import tpu as pltpu

F32 = jnp.float32
BF16 = jnp.bfloat16

D_MODEL = 2048
PAST_LEN = 8192
WINDOW = 128
HD_A = 64
N_HEADS_A = 16
N_KV_A = 2
GROUP_A = N_HEADS_A // N_KV_A
W_A = N_HEADS_A * HD_A
HD_B = 128
N_HEADS_B = 8
W_B = N_HEADS_B * HD_B
RET_CHUNK = 128
ROPE_THETA = 10000.0
EPS = 1e-6

LANES = 128
MXU_WIDTH = 256
SEC_QA, SEC_GA, SEC_QB, SEC_KB, SEC_VB, SEC_GB = 0, 1, 2, 3, 4, 5
IN_TILE = 1024
TILES_PER_SEC = W_A // IN_TILE
KV_TILE = 6 * TILES_PER_SEC
COL_TILE = 2 * N_KV_A * HD_A
KV_BLOCK = KV_TILE * IN_TILE // COL_TILE
D_PROJ_R = (KV_TILE + 1) * IN_TILE
VMEM_LIMIT = 56 * 1024 * 1024
IN_ROWS = 1024
NORM_ROWS = 256

LOG_G = [math.log1p(-(2.0 ** (-5.0 - h))) for h in range(N_HEADS_B)]


def _silu(g):
    return g * (1.0 / (1.0 + jnp.exp(-g)))


EPI_ROPE_A, EPI_ROPE_B, EPI_ROPE_B_SCALED, EPI_PLAIN = 0, 1, 2, 3
_SEC_EPI = (EPI_ROPE_A, EPI_PLAIN, EPI_ROPE_B, EPI_ROPE_B_SCALED, EPI_PLAIN, EPI_PLAIN)


KV_ROPE_GROUPS = N_KV_A * HD_A // LANES
N_W_BLOCKS = IN_TILE // MXU_WIDTH


def _step_tile(step):
    return jnp.where(step == 0, KV_TILE, step - 1)


def _tile_epi(tile):
    sec = tile // TILES_PER_SEC
    t = jnp.int32(EPI_ROPE_A)
    for s, e in enumerate(_SEC_EPI):
        t = jnp.where(sec == s, e, t)
    return t


def _in_proj_kernel(x_ref, nw_ref, *refs, row_chunk, emit_w):
    n_w = N_W_BLOCKS if emit_w else 1
    w_refs = refs[:n_w]
    wkv_in_ref, ca_ref, sa_ref, cb_ref, sb_ref, o_ref, tail_ref = refs[n_w:n_w + 7]
    w_tile_ref = refs[n_w + 7] if emit_w else w_refs[0]
    wkv_ref = refs[n_w + 8] if emit_w else wkv_in_ref
    h_ref = refs[-1]
    step = pl.program_id(1)
    j = _step_tile(step)

    def weights(e):
        cols = slice(e * MXU_WIDTH, (e + 1) * MXU_WIDTH)
        if emit_w:
            w_tile_ref[:, cols] = w_refs[e][...].astype(BF16)
        return w_tile_ref[:, cols]

    tm = x_ref.shape[0]
    n_chunks = tm // row_chunk

    @pl.when(step == 0)
    def _():
        def body(r, carry):
            rows = pl.ds(pl.multiple_of(r * row_chunk, row_chunk), row_chunk)
            x = x_ref[rows, :]
            ms = jnp.mean(x * x, axis=-1, keepdims=True)
            h_ref[rows, :] = (x * lax.rsqrt(ms + EPS) * nw_ref[...]).astype(BF16)
            return carry
        lax.fori_loop(0, n_chunks, body, 0)

    lane = lax.broadcasted_iota(jnp.int32, (tm, LANES), 1)
    first_half = (lane % HD_A) < HD_A // 2

    def rotate(x, c, s, shift_up, shift_dn):
        partner = jnp.where(first_half, pltpu.roll(x, shift_up, 1), pltpu.roll(x, shift_dn, 1))
        return x * c + partner * s

    epi = _tile_epi(j)
    is_plain = epi == EPI_PLAIN

    @pl.when((j < KV_TILE) & is_plain)
    def _():
        for e in range(N_W_BLOCKS):
            cols = slice(e * MXU_WIDTH, (e + 1) * MXU_WIDTH)
            o_ref[:, cols] = jnp.dot(h_ref[...], weights(e), preferred_element_type=F32).astype(o_ref.dtype)

    def rotated_tile(rot):
        for e in range(N_W_BLOCKS):
            acc = jnp.dot(h_ref[...], weights(e), preferred_element_type=F32)
            for gg in range(MXU_WIDTH // LANES):
                cols = slice(e * MXU_WIDTH + gg * LANES, e * MXU_WIDTH + (gg + 1) * LANES)
                o_ref[:, cols] = rot(acc[:, gg * LANES:(gg + 1) * LANES]).astype(o_ref.dtype)

    @pl.when((j < KV_TILE) & (epi == EPI_ROPE_A))
    def _():
        c, s = ca_ref[...], sa_ref[...]
        rotated_tile(lambda x: rotate(x, c, s, LANES - HD_A // 2, HD_A // 2))

    @pl.when((j < KV_TILE) & ((epi == EPI_ROPE_B) | (epi == EPI_ROPE_B_SCALED)))
    def _():
        scale = jnp.where(epi == EPI_ROPE_B_SCALED, HD_B ** -0.5, 1.0).astype(F32)
        c, s = cb_ref[...] * scale, sb_ref[...] * scale
        rotated_tile(lambda x: x * c + pltpu.roll(x, HD_B // 2, 1) * s)

    @pl.when(j == KV_TILE)
    def _():
        if emit_w:
            wkv_ref[...] = wkv_in_ref[...].astype(BF16)
        acc = jnp.dot(h_ref[...], wkv_ref[...], preferred_element_type=F32)
        for g in range(COL_TILE // LANES):
            cols = slice(g * LANES, (g + 1) * LANES)
            out = acc[:, cols]
            if g < KV_ROPE_GROUPS:
                out = rotate(out, ca_ref[...], sa_ref[...], LANES - HD_A // 2, HD_A // 2)
            o_ref[:, cols] = out.astype(o_ref.dtype)
            tail_ref[:, cols] = out[tm - WINDOW:, :]
        o_ref[:, COL_TILE:] = jnp.zeros((tm, IN_TILE - COL_TILE), o_ref.dtype)
        if emit_w:
            w_tile_ref[...] = jnp.zeros(w_tile_ref.shape, BF16)


def _in_proj(x, norm_w, w_src, wkv_src, d, tables, out_dtype, tm, emit_w):
    m = x.shape[0]
    n_tiles = D_PROJ_R // IN_TILE
    assert m == tm or not emit_w
    n_src_blocks = (6 * W_A + 2 * N_KV_A * HD_A) // MXU_WIDTH

    def src_block(j, e):
        sec = j // TILES_PER_SEC
        blk = sec * (W_A // MXU_WIDTH) + jnp.minimum(sec, 1) + (j % TILES_PER_SEC) * N_W_BLOCKS + e
        return jnp.minimum(blk, n_src_blocks - 1)

    if emit_w:
        w_specs = [pl.BlockSpec((None, D_MODEL, MXU_WIDTH),
                                lambda i, s, e=e: (d, 0, src_block(jnp.maximum(s - 1, 0), e)))
                   for e in range(N_W_BLOCKS)]
        wkv_spec = pl.BlockSpec((None, D_MODEL, MXU_WIDTH), lambda i, j: (d, 0, W_A // MXU_WIDTH))
    else:
        w_specs = [pl.BlockSpec((D_MODEL, IN_TILE),
                                lambda i, s: (0, jnp.minimum(_step_tile(s), KV_TILE - 1)))]
        wkv_spec = pl.BlockSpec((D_MODEL, COL_TILE), lambda i, j: (0, 0))
    tab_spec = pl.BlockSpec((tm, LANES), lambda i, j: (i, 0))

    return pl.pallas_call(
        functools.partial(_in_proj_kernel, row_chunk=min(tm, NORM_ROWS), emit_w=emit_w),
        grid=(m // tm, n_tiles),
        in_specs=[
            pl.BlockSpec((tm, D_MODEL), lambda i, j: (i, 0)),
            pl.BlockSpec((1, D_MODEL), lambda i, j: (0, 0)),
        ] + w_specs + [wkv_spec, tab_spec, tab_spec, tab_spec, tab_spec],
        out_specs=[
            pl.BlockSpec((tm, IN_TILE), lambda i, s: (i, _step_tile(s))),
            pl.BlockSpec((WINDOW, COL_TILE), lambda i, j: (0, 0)),
        ] + ([pl.BlockSpec((D_MODEL, IN_TILE), lambda i, s: (0, _step_tile(s))),
              pl.BlockSpec((D_MODEL, COL_TILE), lambda i, j: (0, 0))] if emit_w else []),
        out_shape=[
            jax.ShapeDtypeStruct((m, D_PROJ_R), out_dtype),
            jax.ShapeDtypeStruct((WINDOW, COL_TILE), F32),
        ] + ([jax.ShapeDtypeStruct((D_MODEL, D_PROJ_R), BF16),
              jax.ShapeDtypeStruct((D_MODEL, COL_TILE), BF16)] if emit_w else []),
        scratch_shapes=[pltpu.VMEM((tm, D_MODEL), BF16)],
        compiler_params=pltpu.CompilerParams(
            dimension_semantics=("arbitrary", "arbitrary"),
            vmem_limit_bytes=VMEM_LIMIT),
        name="in_proj",
    )(x, norm_w, *([w_src] * len(w_specs)), wkv_src, *tables)


OUT_COL_STEPS = 4


def _out_proj_kernel(y_ref, x_ref, w_ref, fw_ref, o_ref, wout_ref, *, final):
    c = pl.program_id(0)
    tn = w_ref.shape[1]
    for cc in range(OUT_COL_STEPS):
        @pl.when(c == cc)
        def _(cc=cc):
            cols = slice(cc * tn, (cc + 1) * tn)
            wout_ref[...] = w_ref[...].astype(BF16)
            o_ref[:, cols] = x_ref[:, cols] + jnp.dot(y_ref[...], wout_ref[...], preferred_element_type=F32)
    if final:
        @pl.when(c == OUT_COL_STEPS - 1)
        def _():
            acc = o_ref[...]
            ms = jnp.mean(acc * acc, axis=-1, keepdims=True)
            o_ref[...] = acc * lax.rsqrt(ms + EPS) * fw_ref[...]


def _out_proj(y, x, w_all, d, final_w, final):
    m = x.shape[0]
    tn = D_MODEL // OUT_COL_STEPS
    return pl.pallas_call(
        functools.partial(_out_proj_kernel, final=final),
        grid=(OUT_COL_STEPS,),
        in_specs=[
            pl.BlockSpec((m, D_MODEL), lambda c: (0, 0)),
            pl.BlockSpec((m, D_MODEL), lambda c: (0, 0)),
            pl.BlockSpec((None, D_MODEL, tn), lambda c: (d, 0, c)),
            pl.BlockSpec((1, D_MODEL), lambda c: (0, 0)),
        ],
        out_specs=[
            pl.BlockSpec((m, D_MODEL), lambda c: (0, 0)),
            pl.BlockSpec((D_MODEL, tn), lambda c: (0, c)),
        ],
        out_shape=[
            jax.ShapeDtypeStruct((m, D_MODEL), F32),
            jax.ShapeDtypeStruct((D_MODEL, D_MODEL), BF16),
        ],
        compiler_params=pltpu.CompilerParams(
            dimension_semantics=("arbitrary",),
            vmem_limit_bytes=VMEM_LIMIT),
        name="out_proj",
    )(y, x, w_all, final_w)


def _nt_dot(a, b):
    return lax.dot_general(a, b, (((1,), (1,)), ((), ())), preferred_element_type=F32)


def _tn_dot(a, b):
    return lax.dot_general(a, b, (((0,), (0,)), ((), ())), preferred_element_type=F32)


MIX_CHUNKS = 4
OUT_PIECES = D_MODEL // MXU_WIDTH
PIECES_PER_CHUNK = OUT_PIECES // MIX_CHUNKS


def _mix_prompt_kernel(sink_ref, qa_ref, ga_ref, qb_ref, kb_ref, vb_ref, gb_ref,
                       kv_ref, gnw_ref, x_ref, wo_ref, fw_ref, xo_ref, r_ref,
                       dmask_ref, qdec_ref, kdec_ref, kvd_ref,
                       qs_ref, s_ref, p_ref, o_ref, scb_ref, inter_ref, y_ref, *, final):
    step = pl.program_id(0)
    last_step = pl.num_programs(0) - 1
    slot = step % 2
    prev_slot = 1 - slot
    C = RET_CHUNK
    row = lax.broadcasted_iota(jnp.int32, (C, LANES), 0)
    lane = lax.broadcasted_iota(jnp.int32, (C, LANES), 1)

    @pl.when(step == 0)
    def _():
        r_ref[...] = jnp.zeros(r_ref.shape, F32)
        kvd_ref[...] = jnp.zeros(kvd_ref.shape, kvd_ref.dtype)
        rowf = row.astype(F32)
        diff = rowf - lane.astype(F32)
        for h in range(N_HEADS_B):
            dmask_ref[h] = jnp.where(diff >= 0, jnp.exp(LOG_G[h] * jnp.maximum(diff, 0.0)), 0.0)
            qdec_ref[h] = jnp.exp(LOG_G[h] * (rowf + 1.0))
            kdec_ref[h] = jnp.exp(LOG_G[h] * (C - 1.0 - rowf))

    tri = lane <= row
    lo = lane < HD_A
    neg_inf = jnp.float32(-jnp.inf)
    half_group = GROUP_A // 2
    kcol = lambda g: slice(g * LANES, (g + 1) * LANES)
    vcol = lambda g: slice((N_KV_A + g) * LANES, (N_KV_A + g + 1) * LANES)
    pair_col = lambda g, pi: slice((g * half_group + pi) * LANES, (g * half_group + pi + 1) * LANES)
    head_rows = lambda hh: slice(hh * C, (hh + 1) * C)

    def out_piece(e):
        cols = slice(e * MXU_WIDTH, (e + 1) * MXU_WIDTH)
        xo_ref[:, cols] = x_ref[:, cols] + jnp.dot(y_ref[prev_slot], wo_ref[:, cols],
                                                   preferred_element_type=F32)

    def finish_out():
        if final:
            acc = xo_ref[...]
            ms = jnp.mean(acc * acc, axis=-1, keepdims=True)
            xo_ref[...] = acc * lax.rsqrt(ms + EPS) * fw_ref[...]

    def chunk(cc, with_pieces):
        rows = slice(cc * C, (cc + 1) * C)
        has_prev = step * MIX_CHUNKS + cc > 0
        for t in range(2):
            w = kv_ref[rows, t * LANES:(t + 1) * LANES].astype(F32)
            w_swapped = pltpu.roll(w, HD_A, 1)
            kvd_ref[C:, (2 * t) * LANES:(2 * t + 1) * LANES] = jnp.where(lo, w, w_swapped).astype(BF16)
            kvd_ref[C:, (2 * t + 1) * LANES:(2 * t + 2) * LANES] = jnp.where(lo, w_swapped, w).astype(BF16)
        kv2 = lambda cols: kvd_ref[:, cols]

        for g in range(N_KV_A):
            for pi in range(half_group):
                q_pair = qa_ref[rows, pair_col(g, pi)].astype(F32)
                qs_ref[g, head_rows(2 * pi), :] = jnp.where(lo, q_pair, 0.0).astype(BF16)
                qs_ref[g, head_rows(2 * pi + 1), :] = jnp.where(lo, 0.0, q_pair).astype(BF16)
            s_ref[g] = _nt_dot(qs_ref[g], kv2(kcol(g)))

        if with_pieces:
            out_piece(cc * PIECES_PER_CHUNK)

        for h in range(N_HEADS_B):
            hc = slice(h * HD_B, (h + 1) * HD_B)
            q, k, v = qb_ref[rows, hc], kb_ref[rows, hc], vb_ref[rows, hc]
            scb_ref[h] = (_nt_dot(q, k) * dmask_ref[h]).astype(BF16)
            r = r_ref[h]
            inter_ref[h] = jnp.dot(q, r.astype(BF16), preferred_element_type=F32) * qdec_ref[h]
            kd = (k.astype(F32) * kdec_ref[h]).astype(BF16)
            r_ref[h] = math.exp(LOG_G[h] * C) * r + _tn_dot(kd, v)

        for g in range(N_KV_A):
            for hh in range(GROUP_A):
                hr = head_rows(hh)
                s_prev = jnp.where(has_prev, s_ref[g, hr, :LANES], neg_inf)
                s = jnp.where(tri, s_ref[g, hr, LANES:], s_prev) * (HD_A ** -0.5)
                sink = sink_ref[g * GROUP_A + hh]
                m = jnp.maximum(jnp.max(s, axis=-1, keepdims=True), sink)
                e = jnp.exp(s - m)
                denom = jnp.sum(e, axis=-1, keepdims=True) + jnp.exp(sink - m)
                p = e / denom
                p_ref[g, hr, :LANES] = jnp.where(tri, 0.0, p).astype(BF16)
                p_ref[g, hr, LANES:] = jnp.where(tri, p, 0.0).astype(BF16)

        for g in range(N_KV_A):
            o_ref[g] = jnp.dot(p_ref[g], kv2(vcol(g)), preferred_element_type=F32)

        for e in range(cc * PIECES_PER_CHUNK + 1, (cc + 1) * PIECES_PER_CHUNK):
            if with_pieces:
                out_piece(e)

        for h in range(N_HEADS_B):
            hc = slice(h * HD_B, (h + 1) * HD_B)
            o = jnp.dot(scb_ref[h], vb_ref[rows, hc], preferred_element_type=F32) + inter_ref[h]
            mu = jnp.mean(o, axis=-1, keepdims=True)
            d = o - mu
            var = jnp.mean(d * d, axis=-1, keepdims=True)
            yh = d * lax.rsqrt(var + EPS) * gnw_ref[:, hc] * _silu(gb_ref[rows, hc].astype(F32))
            y_ref[slot, rows, W_A + h * HD_B: W_A + (h + 1) * HD_B] = yh.astype(y_ref.dtype)

        for g in range(N_KV_A):
            for pi in range(half_group):
                pc = pair_col(g, pi)
                o_pair = jnp.where(lo, o_ref[g, head_rows(2 * pi), :], o_ref[g, head_rows(2 * pi + 1), :])
                y_ref[slot, rows, pc] = (o_pair * _silu(ga_ref[rows, pc].astype(F32))).astype(y_ref.dtype)

        kvd_ref[:C, :] = kvd_ref[C:, :]

    @pl.when(step == 0)
    def _():
        for cc in range(MIX_CHUNKS):
            chunk(cc, with_pieces=False)

    @pl.when((step > 0) & (step < last_step))
    def _():
        for cc in range(MIX_CHUNKS):
            chunk(cc, with_pieces=True)
        finish_out()

    @pl.when(step == last_step)
    def _():
        for e in range(OUT_PIECES):
            out_piece(e)
        finish_out()


def _mix_prompt(p, x, sinks, gn_w, w_out_b, final_w, final):
    t = p.shape[0]
    C = RET_CHUNK
    rows = MIX_CHUNKS * C
    n_blocks = t // rows
    cur = lambda c: jnp.minimum(c, n_blocks - 1)
    prev = lambda c: jnp.maximum(c - 1, 0)
    sec_spec = lambda s: pl.BlockSpec((rows, W_A), lambda c, s=s: (cur(c), s))
    return pl.pallas_call(
        functools.partial(_mix_prompt_kernel, final=final),
        grid=(n_blocks + 1,),
        in_specs=[
            pl.BlockSpec(memory_space=pltpu.SMEM),
            sec_spec(SEC_QA), sec_spec(SEC_GA), sec_spec(SEC_QB),
            sec_spec(SEC_KB), sec_spec(SEC_VB), sec_spec(SEC_GB),
            pl.BlockSpec((rows, COL_TILE), lambda c: (cur(c), KV_BLOCK)),
            pl.BlockSpec((1, W_B), lambda c: (0, 0)),
            pl.BlockSpec((rows, D_MODEL), lambda c: (prev(c), 0)),
            pl.BlockSpec((D_MODEL, D_MODEL), lambda c: (0, 0), pipeline_mode=pl.Buffered(1)),
            pl.BlockSpec((1, D_MODEL), lambda c: (0, 0)),
        ],
        out_specs=[
            pl.BlockSpec((rows, D_MODEL), lambda c: (prev(c), 0)),
            pl.BlockSpec((N_HEADS_B, HD_B, HD_B), lambda c: (0, 0, 0)),
        ],
        out_shape=[
            jax.ShapeDtypeStruct((t, D_MODEL), F32),
            jax.ShapeDtypeStruct((N_HEADS_B, HD_B, HD_B), F32),
        ],
        scratch_shapes=[pltpu.VMEM((N_HEADS_B, C, LANES), F32)] * 3 + [
            pltpu.VMEM((2 * C, 2 * COL_TILE), BF16),
            pltpu.VMEM((N_KV_A, GROUP_A * C, LANES), BF16),
            pltpu.VMEM((N_KV_A, GROUP_A * C, 2 * LANES), F32),
            pltpu.VMEM((N_KV_A, GROUP_A * C, 2 * LANES), BF16),
            pltpu.VMEM((N_KV_A, GROUP_A * C, LANES), F32),
            pltpu.VMEM((N_HEADS_B, C, HD_B), BF16),
            pltpu.VMEM((N_HEADS_B, C, HD_B), F32),
            pltpu.VMEM((2, rows, D_MODEL), BF16),
        ],
        compiler_params=pltpu.CompilerParams(
            dimension_semantics=("arbitrary",),
            vmem_limit_bytes=VMEM_LIMIT),
        name="mix_prompt",
    )(sinks, p, p, p, p, p, p, p, gn_w, x, w_out_b, final_w)


SAMPLE_BB = 16
LANE_BATCH = LANES // 4


def _mix_sample_kernel(sinkrow_ref, q3_ref, ga3_ref, qb_ref, kb_ref, vb_ref, gb_ref, kvn_ref,
                       kbt_ref, vbg_ref, ck_ref, cv_ref, st_ref, gnw_ref, *rest, t_new):
    ya3_ref, yb_ref, nk_ref, nv_ref, nst_ref, s_ref, p_ref, vall_ref, kt_ref, x_ref, sc_ref = rest[-11:]
    i = pl.program_id(0)
    T = t_new
    BB = SAMPLE_BB
    L = ck_ref.shape[1]
    n_keys = L + T
    QR = T * GROUP_A
    R = BB * T
    head_sl = lambda g: slice(g * GROUP_A, (g + 1) * GROUP_A)
    lane_sl = lambda g: slice(g * HD_A, (g + 1) * HD_A)
    tok_rows = lambda bb: slice(bb * T, (bb + 1) * T)


    for bb in range(BB):
        k_new = kvn_ref[tok_rows(bb), :LANES]
        v_new = kvn_ref[tok_rows(bb), LANES:]
        k_all = jnp.concatenate([ck_ref[bb], k_new], axis=0)
        v_all = jnp.concatenate([cv_ref[bb], v_new], axis=0)
        nk_ref[bb] = k_all[T:, :]
        nv_ref[bb] = v_all[T:, :]
        vall_ref[bb] = v_all.astype(BF16)
        k_all_b = k_all.astype(BF16)
        for g in range(N_KV_A):
            q = q3_ref[tok_rows(bb), head_sl(g), :].reshape(T * GROUP_A, HD_A).astype(BF16)
            s_ref[g, bb * QR:(bb + 1) * QR, :] = _nt_dot(q, k_all_b[:, lane_sl(g)])

    lane_t = (lax.broadcasted_iota(jnp.int32, (1, LANES), 1) % T).astype(F32)
    lane_b = lax.broadcasted_iota(jnp.int32, (HD_B, LANES), 1) // T
    slot0 = (i % (LANE_BATCH // BB)) * BB
    for h in range(N_HEADS_B):
        hc = slice(h * HD_B, (h + 1) * HD_B)
        lg = LOG_G[h]
        kdec = jnp.exp(lg * (T - 1.0 - lane_t))
        kt_dec = kbt_ref[hc, :] * kdec
        for bb in range(BB):
            kt_ref[h, bb * HD_B:(bb + 1) * HD_B, :] = jnp.where(lane_b == slot0 + bb, kt_dec, 0.0).astype(BF16)
        u = jnp.dot(kt_ref[h], vbg_ref[:, hc].astype(BF16), preferred_element_type=F32)
        for bb in range(BB):
            r = st_ref[bb, h]
            pr = slice((bb // 2) * 2 * T, (bb // 2 + 1) * 2 * T)
            x_ref[bb * N_HEADS_B + h] = jnp.dot(qb_ref[pr, hc].astype(BF16), r.astype(BF16),
                                                 preferred_element_type=F32)
            nst_ref[bb, h] = math.exp(lg * T) * r + u[bb * HD_B:(bb + 1) * HD_B, :]

    ri = lax.broadcasted_iota(jnp.int32, (R, R), 0)
    ci = lax.broadcasted_iota(jnp.int32, (R, R), 1)
    same_b = (ri // T) == (ci // T)
    dt = (ri % T - ci % T).astype(F32)
    keep = same_b & (dt >= 0)
    for h in range(N_HEADS_B):
        hc = slice(h * HD_B, (h + 1) * HD_B)
        dmask = jnp.where(keep, jnp.exp(LOG_G[h] * jnp.maximum(dt, 0.0)), 0.0)
        sc = _nt_dot(qb_ref[:, hc].astype(BF16), kb_ref[:, hc].astype(BF16)) * dmask
        sc_ref[h] = sc.astype(BF16)

    rows_q = (lax.broadcasted_iota(jnp.int32, (BB * QR, n_keys), 0) % QR) // GROUP_A
    key = lax.broadcasted_iota(jnp.int32, (BB * QR, n_keys), 1)
    delta = jnp.where(key < L, rows_q + L - key, rows_q - (key - L))
    valid = (delta >= 0) & (delta < WINDOW)
    for g in range(N_KV_A):
        s = jnp.where(valid, s_ref[g] * (HD_A ** -0.5), -jnp.inf)
        sink = sinkrow_ref[g]
        m = jnp.maximum(jnp.max(s, axis=-1, keepdims=True), sink)
        e = jnp.exp(s - m)
        denom = jnp.sum(e, axis=-1, keepdims=True) + jnp.exp(sink - m)
        p_ref[g] = (e / denom).astype(BF16)

    tposf = (lax.broadcasted_iota(jnp.int32, (R, 1), 0) % T).astype(F32)
    upper = (lax.broadcasted_iota(jnp.int32, (2 * T, 1), 0) < T)
    for h in range(N_HEADS_B):
        hc = slice(h * HD_B, (h + 1) * HD_B)
        intra = jnp.dot(sc_ref[h], vb_ref[:, hc].astype(BF16), preferred_element_type=F32)
        inter = jnp.concatenate(
            [jnp.where(upper, x_ref[(2 * pair) * N_HEADS_B + h], x_ref[(2 * pair + 1) * N_HEADS_B + h])
             for pair in range(BB // 2)], axis=0)
        o = intra + inter * jnp.exp(LOG_G[h] * (tposf + 1.0))
        mu = jnp.mean(o, axis=-1, keepdims=True)
        d = o - mu
        var = jnp.mean(d * d, axis=-1, keepdims=True)
        yb_ref[:, hc] = d * lax.rsqrt(var + EPS) * gnw_ref[:, hc] * _silu(gb_ref[:, hc])

    for bb in range(BB):
        for g in range(N_KV_A):
            o = jnp.dot(p_ref[g, bb * QR:(bb + 1) * QR, :], vall_ref[bb, :, lane_sl(g)],
                        preferred_element_type=F32)
            gate = _silu(ga3_ref[tok_rows(bb), head_sl(g), :].reshape(QR, HD_A))
            ya3_ref[tok_rows(bb), head_sl(g), :] = (o * gate).reshape(T, GROUP_A, HD_A)


def _mix_sample(ps, sinks, gn_w, cache_k, cache_v, state, d, prev_out, t_new):
    m = ps.shape[0]
    nb = m // t_new
    L = cache_k.shape[2]
    BB = SAMPLE_BB
    n_alias = len(prev_out)
    rows = BB * t_new
    assert LANE_BATCH * t_new == LANES and LANE_BATCH % BB == 0 and nb % LANE_BATCH == 0
    qa = ps[:, SEC_QA * W_A:(SEC_QA + 1) * W_A].reshape(m, N_HEADS_A, HD_A)
    ga = ps[:, SEC_GA * W_A:(SEC_GA + 1) * W_A].reshape(m, N_HEADS_A, HD_A)
    kbt = ps[:, SEC_KB * W_A:(SEC_KB + 1) * W_A].T
    sinkrow = jnp.tile(sinks.reshape(N_KV_A, 1, GROUP_A), (1, BB * t_new, 1)).reshape(N_KV_A, -1, 1)
    sec_spec = lambda s: pl.BlockSpec((rows, W_A), lambda i, s=s: (i, s))
    steps_per_lane_group = LANE_BATCH // BB
    operands = (sinkrow, qa, ga, ps, ps, ps, ps, ps, kbt, ps, cache_k, cache_v, state, gn_w)
    n_in = len(operands)
    return pl.pallas_call(
        functools.partial(_mix_sample_kernel, t_new=t_new),
        grid=(nb // BB,),
        in_specs=[
            pl.BlockSpec((N_KV_A, BB * t_new * GROUP_A, 1), lambda i: (0, 0, 0)),
            pl.BlockSpec((rows, N_HEADS_A, HD_A), lambda i: (i, 0, 0)),
            pl.BlockSpec((rows, N_HEADS_A, HD_A), lambda i: (i, 0, 0)),
            sec_spec(SEC_QB), sec_spec(SEC_KB), sec_spec(SEC_VB), sec_spec(SEC_GB),
            pl.BlockSpec((rows, COL_TILE), lambda i: (i, KV_BLOCK)),
            pl.BlockSpec((W_B, LANES), lambda i: (0, i // steps_per_lane_group)),
            pl.BlockSpec((LANES, W_B), lambda i: (i // steps_per_lane_group, SEC_VB)),
            pl.BlockSpec((None, BB, L, LANES), lambda i: (d, i, 0, 0)),
            pl.BlockSpec((None, BB, L, LANES), lambda i: (d, i, 0, 0)),
            pl.BlockSpec((None, BB, N_HEADS_B, HD_B, HD_B), lambda i: (d, i, 0, 0, 0)),
            pl.BlockSpec((1, W_B), lambda i: (0, 0)),
        ] + [pl.BlockSpec(memory_space=pl.ANY)] * n_alias,
        out_specs=[
            pl.BlockSpec((rows, N_HEADS_A, HD_A), lambda i: (i, 0, 0)),
            pl.BlockSpec((rows, W_B), lambda i: (i, 0)),
            pl.BlockSpec((None, BB, L, LANES), lambda i: (d, i, 0, 0)),
            pl.BlockSpec((None, BB, L, LANES), lambda i: (d, i, 0, 0)),
            pl.BlockSpec((None, BB, N_HEADS_B, HD_B, HD_B), lambda i: (d, i, 0, 0, 0)),
        ],
        out_shape=[
            jax.ShapeDtypeStruct((m, N_HEADS_A, HD_A), F32),
            jax.ShapeDtypeStruct((m, W_B), F32),
            jax.ShapeDtypeStruct(cache_k.shape, F32),
            jax.ShapeDtypeStruct(cache_v.shape, F32),
            jax.ShapeDtypeStruct(state.shape, F32),
        ],
        input_output_aliases={n_in + a: 2 + a for a in range(n_alias)},
        scratch_shapes=[
            pltpu.VMEM((N_KV_A, BB * t_new * GROUP_A, L + t_new), F32),
            pltpu.VMEM((N_KV_A, BB * t_new * GROUP_A, L + t_new), BF16),
            pltpu.VMEM((BB, L + t_new, LANES), BF16),
            pltpu.VMEM((N_HEADS_B, BB * HD_B, LANES), BF16),
            pltpu.VMEM((BB * N_HEADS_B, 2 * t_new, HD_B), F32),
            pltpu.VMEM((N_HEADS_B, rows, rows), BF16),
        ],
        compiler_params=pltpu.CompilerParams(
            dimension_semantics=("arbitrary",),
            vmem_limit_bytes=VMEM_LIMIT),
        name="mix_sample",
    )(*operands, *prev_out)


def _rope_tables(pos=None, n=None):
    def tab(hd):
        half = hd // 2
        lane = jnp.arange(LANES)
        inv = ROPE_THETA ** (-(lane % half).astype(F32) / half)
        sign = jnp.where((lane % hd) < half, -1.0, 1.0).astype(F32)
        if pos is not None:
            ang = pos.astype(F32)[:, None] * inv[None, :]
            return jnp.cos(ang), jnp.sin(ang) * sign
        hi = (jnp.arange(n // LANES, dtype=F32) * LANES)[:, None, None] * inv
        lo = jnp.arange(LANES, dtype=F32)[None, :, None] * inv
        ch, sh, cl, sl = jnp.cos(hi), jnp.sin(hi), jnp.cos(lo), jnp.sin(lo)
        return ((ch * cl - sh * sl).reshape(n, LANES), ((sh * cl + ch * sl) * sign).reshape(n, LANES))
    return tab(HD_A) + tab(HD_B)


def kernel(x_prompt, x_sample, cache_k_win, cache_v_win, state_ret, norm_w, w_in, attn_sinks,
           ret_norm_w, w_out, final_norm_w):
    bp, tp, _ = x_prompt.shape
    bs, ts, _ = x_sample.shape
    assert bp == 1 and tp % RET_CHUNK == 0 and min(WINDOW, tp) == WINDOW
    L = cache_k_win.shape[2]
    depth = w_in.shape[0]

    tab_p = _rope_tables(n=tp)
    tab_s = _rope_tables(pos=jnp.tile(PAST_LEN + jnp.arange(ts, dtype=jnp.int32), bs))
    fw = final_norm_w.reshape(1, D_MODEL)

    xp = x_prompt.reshape(tp, D_MODEL)
    xs = x_sample.reshape(bs * ts, D_MODEL)
    ck = cache_k_win.reshape(depth, bs, L, N_KV_A * HD_A)
    cv = cache_v_win.reshape(depth, bs, L, N_KV_A * HD_A)

    kp_l, vp_l, rp_l = [], [], []
    sample_out = ()
    for d in range(depth):
        final = d == depth - 1
        nw = norm_w[d].reshape(1, D_MODEL)
        gnw = ret_norm_w[d].reshape(1, W_B)
        ps, _, w_in_b, w_kv_b = _in_proj(xs, nw, w_in, w_in, d, tab_s, F32, tm=bs * ts, emit_w=True)
        ya3, yb, *sample_out = _mix_sample(ps, attn_sinks[d], gnw, ck, cv, state_ret, d, sample_out, ts)
        ys = jnp.concatenate([ya3.reshape(bs * ts, W_A), yb], axis=1).astype(BF16)
        xs, w_out_b = _out_proj(ys, xs, w_out, d, fw, final)
        p, tail = _in_proj(xp, nw, w_in_b, w_kv_b, d, tab_p, BF16, tm=IN_ROWS, emit_w=False)
        xp, r_fin = _mix_prompt(p, xp, attn_sinks[d], gnw, w_out_b, fw, final)
        kp_l.append(tail[:, :LANES].reshape(WINDOW, N_KV_A, HD_A))
        vp_l.append(tail[:, LANES:].reshape(WINDOW, N_KV_A, HD_A))
        rp_l.append(r_fin)

    keep = min(WINDOW, tp)
    nk, nv, nst = sample_out
    return (xp.reshape(bp, tp, D_MODEL), xs.reshape(bs, ts, D_MODEL),
            jnp.stack(kp_l).reshape(depth, bp, keep, N_KV_A, HD_A),
            jnp.stack(vp_l).reshape(depth, bp, keep, N_KV_A, HD_A),
            jnp.stack(rp_l).reshape(depth, bp, N_HEADS_B, HD_B, HD_B),
            nk.reshape(depth, bs, L, N_KV_A, HD_A), nv.reshape(depth, bs, L, N_KV_A, HD_A), nst)
```
